```python
import math
import jax, jax.numpy as jnp
from jax import lax
import numpy as np

D_MODEL = 1024
BATCH = 8
SEQ = 4096
DEPTH = 1

CHUNK = 64
D_MIX = D_MODEL
FOX_HEADS = 8
FOX_HEAD_DIM = 64
FOX_WIDTH = FOX_HEADS * FOX_HEAD_DIM
HGRN_HEADS = 4
HGRN_HEAD_DIM = 128
HGRN_WIDTH = HGRN_HEADS * HGRN_HEAD_DIM
Q_BLOCK = 128
PEER_HEADS = 8
PEER_QUERY_DIM = 256
N_KEYS = 128
N_EXPERTS = N_KEYS * N_KEYS
PEER_TOPK = 16
PEER_TOKEN_BLOCK = 128
N_MOD = 6
EPS = 1e-6
SPLIT_SIZES = (FOX_WIDTH, FOX_WIDTH, FOX_WIDTH, FOX_HEADS,
               HGRN_WIDTH, HGRN_WIDTH, HGRN_WIDTH, HGRN_WIDTH)
D_IN = sum(SPLIT_SIZES)

kernel_name = "hybrid_fox_hgrn2_peer_block"


def _rmsnorm(x, g):
    x32 = x.astype(jnp.float32)
    y = x32 * lax.rsqrt(jnp.mean(x32 * x32, axis=-1, keepdims=True) + EPS)
    return (y * g.astype(jnp.float32)).astype(x.dtype)


def _forgetting_attention(q, k, v, logf):
    B, S, H, Dh = q.shape
    F = jnp.cumsum(logf, axis=1).transpose(0, 2, 1)
    scale = Dh ** -0.5
    outs = []
    for start in range(0, S, Q_BLOCK):
        end = start + Q_BLOCK
        qb, kb, vb = q[:, start:end], k[:, :end], v[:, :end]
        logits = (jnp.einsum('bqhd,bkhd->bhqk', qb, kb).astype(jnp.float32) * scale
                  + F[:, :, start:end, None] - F[:, :, None, :end])
        mask = (start + jnp.arange(Q_BLOCK))[:, None] >= jnp.arange(end)[None, :]
        p = jax.nn.softmax(jnp.where(mask[None, None], logits, -jnp.inf), axis=-1)
        outs.append(jnp.einsum('bhqk,bkhd->bqhd', p.astype(v.dtype), vb))
    return jnp.concatenate(outs, axis=1)


def _hgrn2_chunkwise(q, f, inp):
    B, S, H, Dk = q.shape
    Dv = inp.shape[-1]
    n = S // CHUNK
    f32 = jnp.float32

    def to_chunks(a):
        return a.astype(f32).reshape(B, n, CHUNK, H, a.shape[-1]).transpose(1, 0, 3, 2, 4)

    qc = to_chunks(q) * (Dk ** -0.5)
    kc = to_chunks(1.0 - f)
    lfc = to_chunks(jnp.log(f))
    ic = to_chunks(inp)
    tri = jnp.tril(jnp.ones((CHUNK, CHUNK), dtype=bool))

    def step(state, xs):
        q_, k_, i_, lf_ = xs
        A = jnp.cumsum(lf_, axis=2)
        o_inter = jnp.einsum('bhtk,bhkv->bhtv', q_ * jnp.exp(A), state)
        diff = A[:, :, :, None, :] - A[:, :, None, :, :]
        decay = jnp.exp(jnp.where(tri[:, :, None], diff, -jnp.inf))
        scores = jnp.einsum('bhtk,bhsk,bhtsk->bhts', q_, k_, decay)
        o_intra = jnp.einsum('bhts,bhsv->bhtv', scores, i_)
        A_last = A[:, :, -1:, :]
        new_state = (jnp.exp(A_last[:, :, 0, :])[..., None] * state
                     + jnp.einsum('bhsk,bhsv->bhkv', k_ * jnp.exp(A_last - A), i_))
        return new_state, o_inter + o_intra

    s0 = jnp.zeros((B, H, Dk, Dv), f32)
    _, o = lax.scan(step, s0, (qc, kc, ic, lfc))
    return o.transpose(1, 0, 3, 2, 4).reshape(B, S, H, Dv).astype(inp.dtype)


def _peer(h, w_q, sub_keys, u, v):
    B, S, D = h.shape
    T = B * S
    ht = h.reshape(T, D)
    q = (ht @ w_q).astype(jnp.float32).reshape(T, PEER_HEADS, 2, PEER_QUERY_DIM // 2)
    q = q * lax.rsqrt(jnp.mean(q * q, axis=-1, keepdims=True) + EPS)
    scores = jnp.einsum('thpd,hpnd->thpn', q, sub_keys.astype(jnp.float32))
    s_top, i_top = lax.top_k(scores, PEER_TOPK)
    cand = (s_top[:, :, 0, :, None] + s_top[:, :, 1, None, :]).reshape(T, PEER_HEADS, PEER_TOPK * PEER_TOPK)
    cand_idx = (i_top[:, :, 0, :, None] * N_KEYS + i_top[:, :, 1, None, :]).reshape(T, PEER_HEADS, PEER_TOPK * PEER_TOPK)
    best, pos = lax.top_k(cand, PEER_TOPK)
    idx = jnp.take_along_axis(cand_idx, pos, axis=-1)
    gate = jax.nn.softmax(best, axis=-1)
    E = PEER_HEADS * PEER_TOPK
    nblk = T // PEER_TOKEN_BLOCK

    def expert_block(args):
        xb, ib, gb = args
        a = jnp.einsum('td,ted->te', xb, u[ib])
        act = (gb * jax.nn.gelu(a.astype(jnp.float32), approximate=False)).astype(xb.dtype)
        return jnp.einsum('te,ted->td', act, v[ib])

    out = lax.map(expert_block, (ht.reshape(nblk, PEER_TOKEN_BLOCK, D),
                                 idx.reshape(nblk, PEER_TOKEN_BLOCK, E),
                                 gate.reshape(nblk, PEER_TOKEN_BLOCK, E)))
    return out.reshape(B, S, D)


def setup_inputs(seed: int = 0) -> dict:
    key = jax.random.key(seed)
    ks = jax.random.split(key, 17)
    nrm = jax.random.normal
    D = D_MODEL
    return {
        'x': nrm(ks[0], (BATCH, SEQ, D), jnp.float32),
        'c': nrm(ks[1], (BATCH, D), jnp.float32),
        'w_ada': nrm(ks[2], (DEPTH, D, N_MOD * D), jnp.float32) * (0.1 * D ** -0.5),
        'b_ada': nrm(ks[3], (DEPTH, N_MOD * D), jnp.float32) * 0.3,
        'pre_mix_g': 1.0 + 0.05 * nrm(ks[4], (DEPTH, D), jnp.float32),
        'post_mix_g': 1.0 + 0.05 * nrm(ks[5], (DEPTH, D), jnp.float32),
        'w_in': nrm(ks[6], (DEPTH, D, D_IN), jnp.float32) * D ** -0.5,
        'b_fox_f': 2.0 + 0.5 * nrm(ks[7], (DEPTH, FOX_HEADS), jnp.float32),
        'hgrn_gamma': 0.5 * nrm(ks[8], (DEPTH + 1, HGRN_WIDTH), jnp.float32),
        'hgrn_norm_g': 1.0 + 0.05 * nrm(ks[9], (DEPTH, HGRN_WIDTH), jnp.float32),
        'w_out': nrm(ks[10], (DEPTH, D_MIX, D), jnp.float32) * D_MIX ** -0.5,
        'pre_ffn_g': 1.0 + 0.05 * nrm(ks[11], (DEPTH, D), jnp.float32),
        'post_ffn_g': 1.0 + 0.05 * nrm(ks[12], (DEPTH, D), jnp.float32),
        'peer_w_q': nrm(ks[13], (DEPTH, D, PEER_HEADS * PEER_QUERY_DIM), jnp.float32) * D ** -0.5,
        'peer_sub_keys': nrm(ks[14], (DEPTH, PEER_HEADS, 2, N_KEYS, PEER_QUERY_DIM // 2), jnp.float32) * (PEER_QUERY_DIM // 2) ** -0.5,
        'peer_u': nrm(ks[15], (DEPTH, N_EXPERTS, D), jnp.float32) * D ** -0.5,
        'peer_v': nrm(ks[16], (DEPTH, N_EXPERTS, D), jnp.float32) * D ** -0.5,
    }


def reference(x, c, w_ada, b_ada, pre_mix_g, post_mix_g, w_in, b_fox_f, hgrn_gamma,
              hgrn_norm_g, w_out, pre_ffn_g, post_ffn_g, peer_w_q, peer_sub_keys,
              peer_u, peer_v):
    B, S, D = x.shape
    lower_bounds = jnp.cumsum(jax.nn.softmax(hgrn_gamma.astype(jnp.float32), axis=0), axis=0)
    cond = jax.nn.silu(c)
    split_idx = [int(s) for s in np.cumsum(SPLIT_SIZES)[:-1]]
    for l in range(DEPTH):
        mod = cond @ w_ada[l] + b_ada[l]
        shift_a, scale_a, gate_a, shift_f, scale_f, gate_f = [m[:, None, :] for m in jnp.split(mod, N_MOD, axis=-1)]

        h = _rmsnorm(x, pre_mix_g[l]) * (1.0 + scale_a) + shift_a
        proj = h @ w_in[l]
        fq, fk, fv, ff, hq, hf, hi, hg = jnp.split(proj, split_idx, axis=-1)

        logf = jax.nn.log_sigmoid(ff.astype(jnp.float32) + b_fox_f[l].astype(jnp.float32))
        shp_f = (B, S, FOX_HEADS, FOX_HEAD_DIM)
        o_fox = _forgetting_attention(fq.reshape(shp_f), fk.reshape(shp_f), fv.reshape(shp_f), logf)
        o_fox = o_fox.reshape(B, S, FOX_WIDTH)

        lb = lower_bounds[l]
        f = lb + (1.0 - lb) * jax.nn.sigmoid(hf.astype(jnp.float32))
        shp_h = (B, S, HGRN_HEADS, HGRN_HEAD_DIM)
        o_h = _hgrn2_chunkwise(hq.reshape(shp_h), f.reshape(shp_h), hi.reshape(shp_h))
        o_h = _rmsnorm(o_h, hgrn_norm_g[l].reshape(HGRN_HEADS, HGRN_HEAD_DIM)).reshape(B, S, HGRN_WIDTH)
        o_h = o_h * jax.nn.silu(hg)

        mix = jnp.concatenate([o_fox, o_h], axis=-1) @ w_out[l]
        x = x + gate_a * _rmsnorm(mix, post_mix_g[l])

        h2 = _rmsnorm(x, pre_ffn_g[l]) * (1.0 + scale_f) + shift_f
        y = _peer(h2, peer_w_q[l], peer_sub_keys[l], peer_u[l], peer_v[l])
        x = x + gate_f * _rmsnorm(y, post_ffn_g[l])
    return x
```

```python
import functools
import math

import jax
import jax.numpy as jnp
from jax import lax
from jax.experimental import pallas as pl
from jax.experimental.pallas import tpu as pltpu

F32 = jnp.float32
BF16 = jnp.bfloat16
EPS = 1e-6
HIGHEST = lax.Precision.HIGHEST

FOX_HEADS = 8
FOX_HEAD_DIM = 64
FOX_WIDTH = FOX_HEADS * FOX_HEAD_DIM
HGRN_HEADS = 4
HGRN_HEAD_DIM = 128
HGRN_WIDTH = HGRN_HEADS * HGRN_HEAD_DIM
HGRN_CHUNK = 16
PEER_HEADS = 8
PEER_QUERY_DIM = 256
PEER_HALF = PEER_QUERY_DIM // 2
N_KEYS = 128
PEER_TOPK = 16
PEER_E = PEER_HEADS * PEER_TOPK
N_MOD = 6

LANES = 128
SUBLANES = 8
VMEM_LIMIT = 56 * 1024 * 1024


def _cparams(n_axes, vmem=VMEM_LIMIT):
    return pltpu.CompilerParams(dimension_semantics=("arbitrary",) * n_axes, vmem_limit_bytes=vmem)


def _dot(a, b):
    return jnp.dot(a, b, preferred_element_type=F32)


def _dot_nt(a, b):
    return lax.dot_general(a, b, (((1,), (1,)), ((), ())), preferred_element_type=F32)


def _dot_tn(a, b):
    return lax.dot_general(a, b, (((0,), (0,)), ((), ())), preferred_element_type=F32)


def _rms(x, g):
    return x * lax.rsqrt(jnp.mean(x * x, axis=-1, keepdims=True) + EPS) * g


def _ada_kernel(c_ref, w_ref, b_ref, o_ref):
    c = c_ref[...]
    cond = c * jax.nn.sigmoid(c)
    o_ref[...] = jnp.dot(cond, w_ref[...], preferred_element_type=F32, precision=HIGHEST) + b_ref[...]


def _ada(c, w, b):
    B, D = c.shape
    N = w.shape[1]
    tn = 1024
    return pl.pallas_call(
        _ada_kernel,
        grid=(N // tn,),
        in_specs=[pl.BlockSpec((B, D), lambda j: (0, 0)),
                  pl.BlockSpec((D, tn), lambda j: (0, j)),
                  pl.BlockSpec((1, tn), lambda j: (0, j))],
        out_specs=pl.BlockSpec((B, tn), lambda j: (0, j)),
        out_shape=jax.ShapeDtypeStruct((B, N), F32),
        compiler_params=_cparams(1),
        name="ada",
    )(c, w, b.reshape(1, N))


def _proj_kernel(x_ref, mod_ref, g_ref, wqkv_ref, wf_ref, bf_ref, wh_ref, tri_ref,
                 q_ref, k_ref, v_ref, fc_ref, hq_ref, hf_ref, hi_ref, hg_ref, carry_ref,
                 *, tiles_per_batch):
    i = pl.program_id(0)
    D = x_ref.shape[1]
    tm = x_ref.shape[0]
    x = x_ref[...]
    shift = mod_ref[0, :, 0:D]
    scale = mod_ref[0, :, D:2 * D]
    h = (_rms(x, g_ref[...]) * (1.0 + scale) + shift).astype(BF16)

    qkv = _dot(h, wqkv_ref[...])
    qscale = FOX_HEAD_DIM ** -0.5
    for hd in range(FOX_HEADS):
        lo = hd * FOX_HEAD_DIM
        q_ref[0, hd] = (qkv[:, lo:lo + FOX_HEAD_DIM] * qscale).astype(BF16)
        k_ref[0, hd] = qkv[:, FOX_WIDTH + lo:FOX_WIDTH + lo + FOX_HEAD_DIM].astype(BF16)
        v_ref[0, hd] = qkv[:, 2 * FOX_WIDTH + lo:2 * FOX_WIDTH + lo + FOX_HEAD_DIM].astype(BF16)

    ff = _dot(h, wf_ref[...]) + bf_ref[...]
    logf = jnp.minimum(ff, 0.0) - jnp.log(1.0 + jnp.exp(-jnp.abs(ff)))

    @pl.when(i % tiles_per_batch == 0)
    def _():
        carry_ref[...] = jnp.zeros_like(carry_ref)

    cs = jnp.dot(tri_ref[...], logf, preferred_element_type=F32, precision=HIGHEST) + carry_ref[...]
    carry_ref[...] = cs[tm - 1:tm, :]
    fc_ref[...] = cs[:, 0:FOX_HEADS]

    hh = _dot(h, wh_ref[...])
    hq_ref[...] = hh[:, 0:HGRN_WIDTH]
    hf_ref[...] = hh[:, HGRN_WIDTH:2 * HGRN_WIDTH]
    hi_ref[...] = hh[:, 2 * HGRN_WIDTH:3 * HGRN_WIDTH]
    hg_ref[...] = hh[:, 3 * HGRN_WIDTH:4 * HGRN_WIDTH]


def _proj(x2d, mod3, g, wqkv, wf, bf, wh, B, S):
    T, D = x2d.shape
    tm = 256
    tpb = S // tm
    tri = (jnp.arange(tm)[:, None] >= jnp.arange(tm)[None, :]).astype(F32)
    hshape = jax.ShapeDtypeStruct((B, FOX_HEADS, S, FOX_HEAD_DIM), BF16)
    hspec = pl.BlockSpec((1, FOX_HEADS, tm, FOX_HEAD_DIM), lambda i: (i // tpb, 0, i % tpb, 0))
    wide = jax.ShapeDtypeStruct((T, HGRN_WIDTH), F32)
    wspec = pl.BlockSpec((tm, HGRN_WIDTH), lambda i: (i, 0))
    const = lambda shape: pl.BlockSpec(shape, lambda i: (0,) * len(shape))
    return pl.pallas_call(
        functools.partial(_proj_kernel, tiles_per_batch=tpb),
        grid=(T // tm,),
        in_specs=[pl.BlockSpec((tm, D), lambda i: (i, 0)),
                  pl.BlockSpec((1, 1, N_MOD * D), lambda i: (i // tpb, 0, 0)),
                  const((1, D)), const(wqkv.shape), const(wf.shape), const(bf.shape), const(wh.shape),
                  const((tm, tm))],
        out_specs=[hspec, hspec, hspec, pl.BlockSpec((tm, FOX_HEADS), lambda i: (i, 0)),
                   wspec, wspec, wspec, wspec],
        out_shape=[hshape, hshape, hshape, jax.ShapeDtypeStruct((T, FOX_HEADS), F32), wide, wide, wide, wide],
        scratch_shapes=[pltpu.VMEM((1, LANES), F32)],
        compiler_params=_cparams(1),
        name="proj",
    )(x2d, mod3, g, wqkv, wf, bf, wh, tri)


def _fox_kernel(q_ref, k_ref, v_ref, fcol_ref, frow_ref, o_ref, m_ref, l_ref, acc_ref, *, tq):
    qi = pl.program_id(2)
    q = q_ref[0, 0]
    fq = fcol_ref[0, 0]
    m_ref[...] = jnp.full_like(m_ref, -jnp.inf)
    l_ref[...] = jnp.zeros_like(l_ref)
    acc_ref[...] = jnp.zeros_like(acc_ref)

    def step(j, masked):
        k = k_ref[0, 0, pl.ds(j * tq, tq), :]
        v = v_ref[0, 0, pl.ds(j * tq, tq), :]
        fk = frow_ref[0, 0, :, pl.ds(j * tq, tq)]
        s = _dot_nt(q, k) + (fq - fk)
        if masked:
            r = lax.broadcasted_iota(jnp.int32, (tq, tq), 0)
            c = lax.broadcasted_iota(jnp.int32, (tq, tq), 1)
            s = jnp.where(r >= c, s, -jnp.inf)
        m_old = m_ref[...]
        m_new = jnp.maximum(m_old, jnp.max(s, axis=-1, keepdims=True))
        p = jnp.exp(s - m_new)
        alpha = jnp.exp(m_old - m_new)
        l_ref[...] = alpha * l_ref[...] + jnp.sum(p, axis=-1, keepdims=True)
        acc_ref[...] = alpha * acc_ref[...] + _dot(p.astype(BF16), v)
        m_ref[...] = m_new

    step(qi, True)

    def body(j, _):
        step(j, False)
        return 0

    lax.fori_loop(0, qi, body, 0)
    o_ref[0, 0] = (acc_ref[...] / l_ref[...]).astype(o_ref.dtype)


def _fox(q, k, v, fcol, frow):
    B, H, S, Dh = q.shape
    tq = 256
    blk = pl.BlockSpec((1, 1, tq, Dh), lambda b, h, i: (b, h, i, 0))
    full = pl.BlockSpec((1, 1, S, Dh), lambda b, h, i: (b, h, 0, 0))
    return pl.pallas_call(
        functools.partial(_fox_kernel, tq=tq),
        grid=(B, H, S // tq),
        in_specs=[blk, full, full,
                  pl.BlockSpec((1, 1, tq, 1), lambda b, h, i: (b, h, i, 0)),
                  pl.BlockSpec((1, 1, 1, S), lambda b, h, i: (b, h, 0, 0))],
        out_specs=blk,
        out_shape=jax.ShapeDtypeStruct((B, H, S, Dh), BF16),
        scratch_shapes=[pltpu.VMEM((tq, 1), F32), pltpu.VMEM((tq, 1), F32), pltpu.VMEM((tq, Dh), F32)],
        compiler_params=_cparams(3),
        name="fox",
    )(q, k, v, fcol, frow)


def _hgrn_kernel(hq_ref, hf_ref, hi_ref, hg_ref, gam_ref, ng_ref, bd_ref, bl_ref, o_ref, st_ref, os_ref):
    j = pl.program_id(1)
    tb = hq_ref.shape[0]
    C = HGRN_CHUNK
    Dk = HGRN_HEAD_DIM

    @pl.when(j == 0)
    def _():
        st_ref[...] = jnp.zeros_like(st_ref)

    gam = gam_ref[...]
    e = jnp.exp(gam - jnp.max(gam, axis=0, keepdims=True))
    lb = e[0:1, :] / jnp.sum(e, axis=0, keepdims=True)
    f = lb + (1.0 - lb) * jax.nn.sigmoid(hf_ref[...])
    lf = jnp.log(f)
    A = jnp.dot(bd_ref[...], lf, preferred_element_type=F32, precision=HIGHEST)
    AL = jnp.dot(bl_ref[...], lf, preferred_element_type=F32, precision=HIGHEST)
    q = hq_ref[...] * (Dk ** -0.5)
    kk = 1.0 - f
    qa = q * jnp.exp(A)
    ka = kk * jnp.exp(AL - A)
    eal = jnp.exp(AL)
    inp = hi_ref[...]

    tri3 = (lax.broadcasted_iota(jnp.int32, (C, C, Dk), 0) >= lax.broadcasted_iota(jnp.int32, (C, C, Dk), 1))
    for hd in range(HGRN_HEADS):
        lo = hd * Dk
        st = st_ref[hd]
        for c in range(tb // C):
            r0 = c * C
            Ac = A[r0:r0 + C, lo:lo + Dk]
            qc = q[r0:r0 + C, lo:lo + Dk]
            kc = kk[r0:r0 + C, lo:lo + Dk]
            ic = inp[r0:r0 + C, lo:lo + Dk]
            diff = Ac[:, None, :] - Ac[None, :, :]
            dec = jnp.exp(jnp.where(tri3, diff, -jnp.inf))
            sc = jnp.sum(qc[:, None, :] * kc[None, :, :] * dec, axis=-1)
            o = _dot_nt(qa[r0:r0 + C, lo:lo + Dk], st) + _dot(sc, ic)
            st = st * eal[r0:r0 + 1, lo:lo + Dk] + _dot_tn(ic, ka[r0:r0 + C, lo:lo + Dk])
            os_ref[r0:r0 + C, lo:lo + Dk] = o
        st_ref[hd] = st

    o_all = os_ref[...]
    hg = hg_ref[...]
    ng = ng_ref[...]
    for hd in range(HGRN_HEADS):
        lo = hd * Dk
        oh = _rms(o_all[:, lo:lo + Dk], ng[:, lo:lo + Dk])
        g = hg[:, lo:lo + Dk]
        o_ref[:, lo:lo + Dk] = (oh * (g * jax.nn.sigmoid(g))).astype(o_ref.dtype)


def _hgrn(hq, hf, hi, hg, gamma, norm_g, B, S):
    T, W = hq.shape
    tb = 128
    nb = S // tb
    t = jnp.arange(tb)
    same = (t[:, None] // HGRN_CHUNK) == (t[None, :] // HGRN_CHUNK)
    bd = (same & (t[:, None] >= t[None, :])).astype(F32)
    bl = same.astype(F32)
    wspec = pl.BlockSpec((tb, W), lambda b, j: (b * nb + j, 0))
    const = lambda shape: pl.BlockSpec(shape, lambda b, j: (0,) * len(shape))
    return pl.pallas_call(
        _hgrn_kernel,
        grid=(B, nb),
        in_specs=[wspec, wspec, wspec, wspec, const(gamma.shape), const((1, W)), const((tb, tb)), const((tb, tb))],
        out_specs=wspec,
        out_shape=jax.ShapeDtypeStruct((T, W), BF16),
        scratch_shapes=[pltpu.VMEM((HGRN_HEADS, HGRN_HEAD_DIM, HGRN_HEAD_DIM), F32), pltpu.VMEM((tb, W), F32)],
        compiler_params=_cparams(2),
        name="hgrn",
    )(hq, hf, hi, hg, gamma, norm_g, bd, bl)


def _outproj_kernel(x_ref, of_ref, oh_ref, mod_ref, wfo_ref, who_ref, pmg_ref, pfg_ref, wq_ref, keys_ref,
                    x1_ref, h2_ref, sc_ref):
    D = x_ref.shape[1]
    mix = _dot(oh_ref[...], who_ref[...])
    for hd in range(FOX_HEADS):
        mix = mix + _dot(of_ref[0, hd], wfo_ref[hd])
    gate_a = mod_ref[0, :, 2 * D:3 * D]
    shift_f = mod_ref[0, :, 3 * D:4 * D]
    scale_f = mod_ref[0, :, 4 * D:5 * D]
    x1 = x_ref[...] + gate_a * _rms(mix, pmg_ref[...])
    x1_ref[...] = x1
    h2 = _rms(x1, pfg_ref[...]) * (1.0 + scale_f) + shift_f
    h2_ref[...] = h2
    qp = _dot(h2.astype(BF16), wq_ref[...])
    for g in range(2 * PEER_HEADS):
        qg = qp[:, g * PEER_HALF:(g + 1) * PEER_HALF]
        qn = qg * lax.rsqrt(jnp.mean(qg * qg, axis=-1, keepdims=True) + EPS)
        sc_ref[g] = _dot_nt(keys_ref[g], qn.astype(BF16))


def _outproj(x2d, ofox, oh, mod3, wfo, who, pmg, pfg, wq, keys, B, S):
    T, D = x2d.shape
    tm = 256
    tpb = S // tm
    G = 2 * PEER_HEADS
    const = lambda shape: pl.BlockSpec(shape, lambda i: (0,) * len(shape))
    row = pl.BlockSpec((tm, D), lambda i: (i, 0))
    return pl.pallas_call(
        _outproj_kernel,
        grid=(T // tm,),
        in_specs=[row,
                  pl.BlockSpec((1, FOX_HEADS, tm, FOX_HEAD_DIM), lambda i: (i // tpb, 0, i % tpb, 0)),
                  pl.BlockSpec((tm, HGRN_WIDTH), lambda i: (i, 0)),
                  pl.BlockSpec((1, 1, N_MOD * D), lambda i: (i // tpb, 0, 0)),
                  const(wfo.shape), const(who.shape), const((1, D)), const((1, D)), const(wq.shape), const(keys.shape)],
        out_specs=[row, row, pl.BlockSpec((G, N_KEYS, tm), lambda i: (0, 0, i))],
        out_shape=[jax.ShapeDtypeStruct((T, D), F32), jax.ShapeDtypeStruct((T, D), F32),
                   jax.ShapeDtypeStruct((G, N_KEYS, T), F32)],
        compiler_params=_cparams(1),
        name="outproj",
    )(x2d, ofox, oh, mod3, wfo, who, pmg, pfg, wq, keys)


def _topk_rows(s, k, payload=None):
    n = s.shape[0]
    iota = lax.broadcasted_iota(jnp.int32, s.shape, 0)
    vals, picks = [], []
    for _ in range(k):
        m = jnp.max(s, axis=0, keepdims=True)
        ix = jnp.min(jnp.where(s == m, iota, n), axis=0, keepdims=True)
        sel = iota == ix
        vals.append(m)
        if payload is None:
            picks.append(ix)
        else:
            picks.append(jnp.sum(jnp.where(sel, payload, 0), axis=0, keepdims=True))
        s = jnp.where(sel, -jnp.inf, s)
    return jnp.concatenate(vals, axis=0), jnp.concatenate(picks, axis=0)


def _topk_kernel(sc_ref, idx_ref, gate_ref):
    K = PEER_TOPK
    idx_rows, gate_rows = [], []
    for hd in range(PEER_HEADS):
        s1, i1 = _topk_rows(sc_ref[2 * hd], K)
        s2, i2 = _topk_rows(sc_ref[2 * hd + 1], K)
        cand = jnp.concatenate([s1[a:a + 1, :] + s2 for a in range(K)], axis=0)
        cidx = jnp.concatenate([i1[a:a + 1, :] * N_KEYS + i2 for a in range(K)], axis=0)
        best, eidx = _topk_rows(cand, K, payload=cidx)
        p = jnp.exp(best - best[0:1, :])
        gate_rows.append(p / jnp.sum(p, axis=0, keepdims=True))
        idx_rows.append(eidx)
    idx_ref[...] = jnp.concatenate(idx_rows, axis=0).T
    gate_ref[...] = jnp.concatenate(gate_rows, axis=0).T


def _topk(scores):
    G, NK, T = scores.shape
    tm = 256
    out = pl.BlockSpec((tm, PEER_E), lambda i: (i, 0))
    return pl.pallas_call(
        _topk_kernel,
        grid=(T // tm,),
        in_specs=[pl.BlockSpec((G, NK, tm), lambda i: (0, 0, i))],
        out_specs=[out, out],
        out_shape=[jax.ShapeDtypeStruct((T, PEER_E), jnp.int32), jax.ShapeDtypeStruct((T, PEER_E), F32)],
        compiler_params=_cparams(1),
        name="topk",
    )(scores)


def _pack_table(w):
    n, d = w.shape
    bits = lax.bitcast_convert_type(w.astype(BF16), jnp.uint16).astype(jnp.uint32).reshape(n // 2, 2, d)
    packed = lax.shift_left(bits[:, 0, :], jnp.uint32(16)) | bits[:, 1, :]
    return packed.reshape(n // 2, d // LANES, LANES)


def _table_row(tab_ref, e):
    w = tab_ref[lax.shift_right_logical(e, 1)]
    sh = lax.shift_left(e & 1, 4).astype(jnp.uint32)
    return lax.bitcast_convert_type(lax.shift_left(w, sh) & jnp.uint32(0xFFFF0000), F32)


_BITREV8 = (0, 4, 2, 6, 1, 5, 3, 7)


def _fold8(ps):
    r = lax.broadcasted_iota(jnp.int32, (SUBLANES, LANES), 0)
    m4 = r < 4
    m2 = (r & 3) < 2
    m1 = (r & 1) == 0
    roll = lambda x, s: pltpu.roll(x, s % SUBLANES, axis=0)
    c = []
    for a, b in ((0, 1), (2, 3), (4, 5), (6, 7)):
        c.append(jnp.where(m4, ps[a], ps[b]) + roll(jnp.where(m4, ps[b], ps[a]), 4))
    d = []
    for a, b in ((0, 1), (2, 3)):
        d.append(jnp.where(m2, c[a], roll(c[b], 2)) + jnp.where(m2, roll(c[a], -2), c[b]))
    return jnp.where(m1, d[0], roll(d[1], 1)) + jnp.where(m1, roll(d[0], -1), d[1])


def _peer_u_kernel(idx_ref, x_ref, gate_ref, tab_ref, act_ref, a_ref):
    tm = x_ref.shape[0]
    ones = jnp.ones((SUBLANES, LANES), F32)

    def tok(t, _):
        x = x_ref[t]
        rows = []
        for g in range(PEER_E // SUBLANES):
            ps = [_table_row(tab_ref, idx_ref[t, g * SUBLANES + _BITREV8[s]]) * x for s in range(SUBLANES)]
            rows.append(_fold8(ps))
        part = jnp.concatenate(rows, axis=0)
        a = lax.dot_general(ones, part, (((1,), (1,)), ((), ())), preferred_element_type=F32, precision=HIGHEST)
        a_ref[pl.ds(t, 1), :] = a[0:1, :]
        return 0

    lax.fori_loop(0, tm, tok, 0)
    a = a_ref[...]
    gelu = 0.5 * a * (1.0 + lax.erf(a * (2.0 ** -0.5)))
    act_ref[...] = gate_ref[...] * gelu


def _peer_u(idx, h2f, gate, tab):
    T = idx.shape[0]
    tm = 64
    return pl.pallas_call(
        _peer_u_kernel,
        grid=(T // tm,),
        in_specs=[pl.BlockSpec((tm, PEER_E), lambda i: (i, 0), memory_space=pltpu.SMEM),
                  pl.BlockSpec((tm, SUBLANES, LANES), lambda i: (i, 0, 0)),
                  pl.BlockSpec((tm, PEER_E), lambda i: (i, 0)),
                  pl.BlockSpec(tab.shape, lambda i: (0, 0, 0), pipeline_mode=pl.Buffered(1))],
        out_specs=pl.BlockSpec((tm, PEER_E), lambda i: (i, 0)),
        out_shape=jax.ShapeDtypeStruct((T, PEER_E), F32),
        scratch_shapes=[pltpu.VMEM((tm, PEER_E), F32)],
        compiler_params=_cparams(1),
        name="peer_u",
    )(idx, h2f, gate, tab)


def _peer_v_kernel(idx_ref, act_ref, tab_ref, y_ref):
    tm = y_ref.shape[0]

    def tok(t, _):
        accs = [jnp.zeros((SUBLANES, LANES), F32) for _ in range(4)]
        for j in range(PEER_E):
            accs[j % 4] = accs[j % 4] + act_ref[t, j] * _table_row(tab_ref, idx_ref[t, j])
        y_ref[t] = (accs[0] + accs[1]) + (accs[2] + accs[3])
        return 0

    lax.fori_loop(0, tm, tok, 0)


def _peer_v(idx, act, tab):
    T = idx.shape[0]
    tm = 64
    smem = pl.BlockSpec((tm, PEER_E), lambda i: (i, 0), memory_space=pltpu.SMEM)
    return pl.pallas_call(
        _peer_v_kernel,
        grid=(T // tm,),
        in_specs=[smem, smem, pl.BlockSpec(tab.shape, lambda i: (0, 0, 0), pipeline_mode=pl.Buffered(1))],
        out_specs=pl.BlockSpec((tm, SUBLANES, LANES), lambda i: (i, 0, 0)),
        out_shape=jax.ShapeDtypeStruct((T, SUBLANES, LANES), F32),
        compiler_params=_cparams(1),
        name="peer_v",
    )(idx, act, tab)


def _final_kernel(x1_ref, y_ref, mod_ref, g_ref, o_ref):
    D = x1_ref.shape[1]
    gate_f = mod_ref[0, :, 5 * D:6 * D]
    o_ref[...] = x1_ref[...] + gate_f * _rms(y_ref[...], g_ref[...])


def _final(x1, y, mod3, g, B, S):
    T, D = x1.shape
    tm = 512
    tpb = S // tm
    row = pl.BlockSpec((tm, D), lambda i: (i, 0))
    return pl.pallas_call(
        _final_kernel,
        grid=(T // tm,),
        in_specs=[row, row, pl.BlockSpec((1, 1, N_MOD * D), lambda i: (i // tpb, 0, 0)),
                  pl.BlockSpec((1, D), lambda i: (0, 0))],
        out_specs=row,
        out_shape=jax.ShapeDtypeStruct((T, D), F32),
        compiler_params=_cparams(1),
        name="final",
    )(x1, y, mod3, g)


def kernel(x, c, w_ada, b_ada, pre_mix_g, post_mix_g, w_in, b_fox_f, hgrn_gamma, hgrn_norm_g, w_out, pre_ffn_g,
           post_ffn_g, peer_w_q, peer_sub_keys, peer_u, peer_v):
    B, S, D = x.shape
    T = B * S
    depth = w_in.shape[0]
    assert depth == 1, "single-layer block"
    l = 0
    x2d = x.reshape(T, D)

    mod3 = _ada(c, w_ada[l], b_ada[l]).reshape(B, 1, N_MOD * D)

    w = w_in[l]
    o1 = 3 * FOX_WIDTH
    wqkv = w[:, :o1].astype(BF16)
    wf = jnp.pad(w[:, o1:o1 + FOX_HEADS], ((0, 0), (0, LANES - FOX_HEADS))).astype(BF16)
    bf = jnp.pad(b_fox_f[l].astype(F32), (0, LANES - FOX_HEADS)).reshape(1, LANES)
    wh = w[:, o1 + FOX_HEADS:].astype(BF16)

    q, k, v, fc, hq, hf, hi, hg = _proj(x2d, mod3, pre_mix_g[l].reshape(1, D), wqkv, wf, bf, wh, B, S)

    frow = fc.reshape(B, S, FOX_HEADS).transpose(0, 2, 1)
    o_fox = _fox(q, k, v, frow.reshape(B, FOX_HEADS, S, 1), frow.reshape(B, FOX_HEADS, 1, S))

    o_h = _hgrn(hq, hf, hi, hg, hgrn_gamma[l:l + 2].astype(F32), hgrn_norm_g[l].reshape(1, HGRN_WIDTH), B, S)

    wo = w_out[l].astype(BF16)
    wfo = wo[:FOX_WIDTH].reshape(FOX_HEADS, FOX_HEAD_DIM, D)
    who = wo[FOX_WIDTH:]
    keys = peer_sub_keys[l].reshape(2 * PEER_HEADS, N_KEYS, PEER_HALF).astype(BF16)
    x1, h2, scores = _outproj(x2d, o_fox, o_h, mod3, wfo, who, post_mix_g[l].reshape(1, D),
                              pre_ffn_g[l].reshape(1, D), peer_w_q[l].astype(BF16), keys, B, S)

    idx, gate = _topk(scores)

    act = _peer_u(idx, h2.reshape(T, SUBLANES, LANES), gate, _pack_table(peer_u[l]))
    y = _peer_v(idx, act, _pack_table(peer_v[l]))

    out = _final(x1, y.reshape(T, D), mod3, post_ffn_g[l].reshape(1, D), B, S)
    return out.reshape(B, S, D)
```

```python
import functools
import math

import jax
import jax.numpy as jnp
from jax import lax
from jax.experimental import pallas as pl
from jax.experimental.pallas import tpu as pltpu

F32 = jnp.float32
BF16 = jnp.bfloat16
EPS = 1e-6
HIGHEST = lax.Precision.HIGHEST

FOX_HEADS = 8
FOX_HEAD_DIM = 64
FOX_WIDTH = FOX_HEADS * FOX_HEAD_DIM
HGRN_HEADS = 4
HGRN_HEAD_DIM = 128
HGRN_WIDTH = HGRN_HEADS * HGRN_HEAD_DIM
HGRN_CHUNK = 16
PEER_HEADS = 8
PEER_QUERY_DIM = 256
PEER_HALF = PEER_QUERY_DIM // 2
N_KEYS = 128
PEER_TOPK = 16
PEER_E = PEER_HEADS * PEER_TOPK
N_MOD = 6

LANES = 128
SUBLANES = 8
VMEM_LIMIT = 56 * 1024 * 1024


def _cparams(n_axes, vmem=VMEM_LIMIT):
    return pltpu.CompilerParams(dimension_semantics=("arbitrary",) * n_axes, vmem_limit_bytes=vmem)


def _dot(a, b):
    return jnp.dot(a, b, preferred_element_type=F32)


def _dot_nt(a, b):
    return lax.dot_general(a, b, (((1,), (1,)), ((), ())), preferred_element_type=F32)


def _dot_tn(a, b):
    return lax.dot_general(a, b, (((0,), (0,)), ((), ())), preferred_element_type=F32)


def _rms(x, g):
    return x * lax.rsqrt(jnp.mean(x * x, axis=-1, keepdims=True) + EPS) * g


def _ada_kernel(c_ref, w_ref, b_ref, o_ref):
    c = c_ref[...]
    cond = c * jax.nn.sigmoid(c)
    o_ref[...] = jnp.dot(cond, w_ref[...], preferred_element_type=F32, precision=HIGHEST) + b_ref[...]


def _ada(c, w, b):
    B, D = c.shape
    N = w.shape[1]
    tn = 1024
    return pl.pallas_call(
        _ada_kernel,
        grid=(N // tn,),
        in_specs=[pl.BlockSpec((B, D), lambda j: (0, 0)),
                  pl.BlockSpec((D, tn), lambda j: (0, j)),
                  pl.BlockSpec((1, tn), lambda j: (0, j))],
        out_specs=pl.BlockSpec((B, tn), lambda j: (0, j)),
        out_shape=jax.ShapeDtypeStruct((B, N), F32),
        compiler_params=_cparams(1),
        name="ada",
    )(c, w, b.reshape(1, N))


def _proj_kernel(x_ref, mod_ref, g_ref, wqkv_ref, wf_ref, bf_ref, wh_ref, tri_ref,
                 q_ref, k_ref, v_ref, fc_ref, hq_ref, hf_ref, hi_ref, hg_ref, carry_ref,
                 *, tiles_per_batch):
    i = pl.program_id(0)
    D = x_ref.shape[1]
    tm = x_ref.shape[0]
    x = x_ref[...]
    shift = mod_ref[0, :, 0:D]
    scale = mod_ref[0, :, D:2 * D]
    h = (_rms(x, g_ref[...]) * (1.0 + scale) + shift).astype(BF16)

    qkv = _dot(h, wqkv_ref[...])
    qscale = FOX_HEAD_DIM ** -0.5
    for hd in range(FOX_HEADS):
        lo = hd * FOX_HEAD_DIM
        q_ref[0, hd] = (qkv[:, lo:lo + FOX_HEAD_DIM] * qscale).astype(BF16)
        k_ref[0, hd] = qkv[:, FOX_WIDTH + lo:FOX_WIDTH + lo + FOX_HEAD_DIM].astype(BF16)
        v_ref[0, hd] = qkv[:, 2 * FOX_WIDTH + lo:2 * FOX_WIDTH + lo + FOX_HEAD_DIM].astype(BF16)

    ff = _dot(h, wf_ref[...]) + bf_ref[...]
    logf = jnp.minimum(ff, 0.0) - jnp.log(1.0 + jnp.exp(-jnp.abs(ff)))

    @pl.when(i % tiles_per_batch == 0)
    def _():
        carry_ref[...] = jnp.zeros_like(carry_ref)

    cs = jnp.dot(tri_ref[...], logf, preferred_element_type=F32, precision=HIGHEST) + carry_ref[...]
    carry_ref[...] = cs[tm - 1:tm, :]
    fc_ref[...] = cs[:, 0:FOX_HEADS]

    hh = _dot(h, wh_ref[...])
    hq_ref[...] = hh[:, 0:HGRN_WIDTH]
    hf_ref[...] = hh[:, HGRN_WIDTH:2 * HGRN_WIDTH]
    hi_ref[...] = hh[:, 2 * HGRN_WIDTH:3 * HGRN_WIDTH]
    hg_ref[...] = hh[:, 3 * HGRN_WIDTH:4 * HGRN_WIDTH]


def _proj(x2d, mod3, g, wqkv, wf, bf, wh, B, S):
    T, D = x2d.shape
    tm = 256
    tpb = S // tm
    tri = (jnp.arange(tm)[:, None] >= jnp.arange(tm)[None, :]).astype(F32)
    hshape = jax.ShapeDtypeStruct((B, FOX_HEADS, S, FOX_HEAD_DIM), BF16)
    hspec = pl.BlockSpec((1, FOX_HEADS, tm, FOX_HEAD_DIM), lambda i: (i // tpb, 0, i % tpb, 0))
    wide = jax.ShapeDtypeStruct((T, HGRN_WIDTH), F32)
    wspec = pl.BlockSpec((tm, HGRN_WIDTH), lambda i: (i, 0))
    const = lambda shape: pl.BlockSpec(shape, lambda i: (0,) * len(shape))
    return pl.pallas_call(
        functools.partial(_proj_kernel, tiles_per_batch=tpb),
        grid=(T // tm,),
        in_specs=[pl.BlockSpec((tm, D), lambda i: (i, 0)),
                  pl.BlockSpec((1, 1, N_MOD * D), lambda i: (i // tpb, 0, 0)),
                  const((1, D)), const(wqkv.shape), const(wf.shape), const(bf.shape), const(wh.shape),
                  const((tm, tm))],
        out_specs=[hspec, hspec, hspec, pl.BlockSpec((tm, FOX_HEADS), lambda i: (i, 0)),
                   wspec, wspec, wspec, wspec],
        out_shape=[hshape, hshape, hshape, jax.ShapeDtypeStruct((T, FOX_HEADS), F32), wide, wide, wide, wide],
        scratch_shapes=[pltpu.VMEM((1, LANES), F32)],
        compiler_params=_cparams(1),
        name="proj",
    )(x2d, mod3, g, wqkv, wf, bf, wh, tri)


def _fox_kernel(q_ref, k_ref, v_ref, fcol_ref, frow_ref, o_ref, m_ref, l_ref, acc_ref, *, tq):
    qi = pl.program_id(2)
    q = q_ref[0, 0]
    fq = fcol_ref[0, 0]
    m_ref[...] = jnp.full_like(m_ref, -jnp.inf)
    l_ref[...] = jnp.zeros_like(l_ref)
    acc_ref[...] = jnp.zeros_like(acc_ref)

    def step(j, masked):
        k = k_ref[0, 0, pl.ds(j * tq, tq), :]
        v = v_ref[0, 0, pl.ds(j * tq, tq), :]
        fk = frow_ref[0, 0, :, pl.ds(j * tq, tq)]
        s = _dot_nt(q, k) + (fq - fk)
        if masked:
            r = lax.broadcasted_iota(jnp.int32, (tq, tq), 0)
            c = lax.broadcasted_iota(jnp.int32, (tq, tq), 1)
            s = jnp.where(r >= c, s, -jnp.inf)
        m_old = m_ref[...]
        m_new = jnp.maximum(m_old, jnp.max(s, axis=-1, keepdims=True))
        p = jnp.exp(s - m_new)
        alpha = jnp.exp(m_old - m_new)
        l_ref[...] = alpha * l_ref[...] + jnp.sum(p, axis=-1, keepdims=True)
        acc_ref[...] = alpha * acc_ref[...] + _dot(p.astype(BF16), v)
        m_ref[...] = m_new

    step(qi, True)

    def body(j, _):
        step(j, False)
        return 0

    lax.fori_loop(0, qi, body, 0)
    o_ref[0, 0] = (acc_ref[...] / l_ref[...]).astype(o_ref.dtype)


def _fox(q, k, v, fcol, frow):
    B, H, S, Dh = q.shape
    tq = 256
    blk = pl.BlockSpec((1, 1, tq, Dh), lambda b, h, i: (b, h, i, 0))
    full = pl.BlockSpec((1, 1, S, Dh), lambda b, h, i: (b, h, 0, 0))
    return pl.pallas_call(
        functools.partial(_fox_kernel, tq=tq),
        grid=(B, H, S // tq),
        in_specs=[blk, full, full,
                  pl.BlockSpec((1, 1, tq, 1), lambda b, h, i: (b, h, i, 0)),
                  pl.BlockSpec((1, 1, 1, S), lambda b, h, i: (b, h, 0, 0))],
        out_specs=blk,
        out_shape=jax.ShapeDtypeStruct((B, H, S, Dh), BF16),
        scratch_shapes=[pltpu.VMEM((tq, 1), F32), pltpu.VMEM((tq, 1), F32), pltpu.VMEM((tq, Dh), F32)],
        compiler_params=_cparams(3),
        name="fox",
    )(q, k, v, fcol, frow)


def _hgrn_kernel(hq_ref, hf_ref, hi_ref, hg_ref, gam_ref, ng_ref, bd_ref, bl_ref, o_ref, st_ref, os_ref):
    j = pl.program_id(1)
    tb = hq_ref.shape[0]
    C = HGRN_CHUNK
    Dk = HGRN_HEAD_DIM

    @pl.when(j == 0)
    def _():
        st_ref[...] = jnp.zeros_like(st_ref)

    gam = gam_ref[...]
    e = jnp.exp(gam - jnp.max(gam, axis=0, keepdims=True))
    lb = e[0:1, :] / jnp.sum(e, axis=0, keepdims=True)
    f = lb + (1.0 - lb) * jax.nn.sigmoid(hf_ref[...])
    lf = jnp.log(f)
    A = jnp.dot(bd_ref[...], lf, preferred_element_type=F32, precision=HIGHEST)
    AL = jnp.dot(bl_ref[...], lf, preferred_element_type=F32, precision=HIGHEST)
    q = hq_ref[...] * (Dk ** -0.5)
    kk = 1.0 - f
    qa = q * jnp.exp(A)
    ka = kk * jnp.exp(AL - A)
    eal = jnp.exp(AL)
    inp = hi_ref[...]

    tri3 = (lax.broadcasted_iota(jnp.int32, (C, C, Dk), 0) >= lax.broadcasted_iota(jnp.int32, (C, C, Dk), 1))
    for hd in range(HGRN_HEADS):
        lo = hd * Dk
        st = st_ref[hd]
        for c in range(tb // C):
            r0 = c * C
            Ac = A[r0:r0 + C, lo:lo + Dk]
            qc = q[r0:r0 + C, lo:lo + Dk]
            kc = kk[r0:r0 + C, lo:lo + Dk]
            ic = inp[r0:r0 + C, lo:lo + Dk]
            diff = Ac[:, None, :] - Ac[None, :, :]
            dec = jnp.exp(jnp.where(tri3, diff, -jnp.inf))
            sc = jnp.sum(qc[:, None, :] * kc[None, :, :] * dec, axis=-1)
            o = _dot_nt(qa[r0:r0 + C, lo:lo + Dk], st) + _dot(sc, ic)
            st = st * eal[r0:r0 + 1, lo:lo + Dk] + _dot_tn(ic, ka[r0:r0 + C, lo:lo + Dk])
            os_ref[r0:r0 + C, lo:lo + Dk] = o
        st_ref[hd] = st

    o_all = os_ref[...]
    hg = hg_ref[...]
    ng = ng_ref[...]
    for hd in range(HGRN_HEADS):
        lo = hd * Dk
        oh = _rms(o_all[:, lo:lo + Dk], ng[:, lo:lo + Dk])
        g = hg[:, lo:lo + Dk]
        o_ref[:, lo:lo + Dk] = (oh * (g * jax.nn.sigmoid(g))).astype(o_ref.dtype)


def _hgrn(hq, hf, hi, hg, gamma, norm_g, B, S):
    T, W = hq.shape
    tb = 128
    nb = S // tb
    t = jnp.arange(tb)
    same = (t[:, None] // HGRN_CHUNK) == (t[None, :] // HGRN_CHUNK)
    bd = (same & (t[:, None] >= t[None, :])).astype(F32)
    bl = same.astype(F32)
    wspec = pl.BlockSpec((tb, W), lambda b, j: (b * nb + j, 0))
    const = lambda shape: pl.BlockSpec(shape, lambda b, j: (0,) * len(shape))
    return pl.pallas_call(
        _hgrn_kernel,
        grid=(B, nb),
        in_specs=[wspec, wspec, wspec, wspec, const(gamma.shape), const((1, W)), const((tb, tb)), const((tb, tb))],
        out_specs=wspec,
        out_shape=jax.ShapeDtypeStruct((T, W), BF16),
        scratch_shapes=[pltpu.VMEM((HGRN_HEADS, HGRN_HEAD_DIM, HGRN_HEAD_DIM), F32), pltpu.VMEM((tb, W), F32)],
        compiler_params=_cparams(2),
        name="hgrn",
    )(hq, hf, hi, hg, gamma, norm_g, bd, bl)


def _outproj_kernel(x_ref, of_ref, oh_ref, mod_ref, wfo_ref, who_ref, pmg_ref, pfg_ref, wq_ref, keys_ref,
                    x1_ref, h2_ref, sc_ref):
    D = x_ref.shape[1]
    mix = _dot(oh_ref[...], who_ref[...])
    for hd in range(FOX_HEADS):
        mix = mix + _dot(of_ref[0, hd], wfo_ref[hd])
    gate_a = mod_ref[0, :, 2 * D:3 * D]
    shift_f = mod_ref[0, :, 3 * D:4 * D]
    scale_f = mod_ref[0, :, 4 * D:5 * D]
    x1 = x_ref[...] + gate_a * _rms(mix, pmg_ref[...])
    x1_ref[...] = x1
    h2 = _rms(x1, pfg_ref[...]) * (1.0 + scale_f) + shift_f
    h2b = h2.astype(BF16)
    h2_ref[...] = h2b
    qp = _dot(h2b, wq_ref[...])
    for g in range(2 * PEER_HEADS):
        qg = qp[:, g * PEER_HALF:(g + 1) * PEER_HALF]
        qn = qg * lax.rsqrt(jnp.mean(qg * qg, axis=-1, keepdims=True) + EPS)
        sc_ref[g] = _dot_nt(keys_ref[g], qn.astype(BF16))


def _outproj(x2d, ofox, oh, mod3, wfo, who, pmg, pfg, wq, keys, B, S):
    T, D = x2d.shape
    tm = 256
    tpb = S // tm
    G = 2 * PEER_HEADS
    const = lambda shape: pl.BlockSpec(shape, lambda i: (0,) * len(shape))
    row = pl.BlockSpec((tm, D), lambda i: (i, 0))
    return pl.pallas_call(
        _outproj_kernel,
        grid=(T // tm,),
        in_specs=[row,
                  pl.BlockSpec((1, FOX_HEADS, tm, FOX_HEAD_DIM), lambda i: (i // tpb, 0, i % tpb, 0)),
                  pl.BlockSpec((tm, HGRN_WIDTH), lambda i: (i, 0)),
                  pl.BlockSpec((1, 1, N_MOD * D), lambda i: (i // tpb, 0, 0)),
                  const(wfo.shape), const(who.shape), const((1, D)), const((1, D)), const(wq.shape), const(keys.shape)],
        out_specs=[row, row, pl.BlockSpec((G, N_KEYS, tm), lambda i: (0, 0, i))],
        out_shape=[jax.ShapeDtypeStruct((T, D), F32), jax.ShapeDtypeStruct((T, D), BF16),
                   jax.ShapeDtypeStruct((G, N_KEYS, T), F32)],
        compiler_params=_cparams(1),
        name="outproj",
    )(x2d, ofox, oh, mod3, wfo, who, pmg, pfg, wq, keys)


def _topk_rows(s, k, payload=None):
    n = s.shape[0]
    iota = lax.broadcasted_iota(jnp.int32, s.shape, 0)
    vals, picks = [], []
    for _ in range(k):
        m = jnp.max(s, axis=0, keepdims=True)
        ix = jnp.min(jnp.where(s == m, iota, n), axis=0, keepdims=True)
        sel = iota == ix
        vals.append(m)
        if payload is None:
            picks.append(ix)
        else:
            picks.append(jnp.sum(jnp.where(sel, payload, 0), axis=0, keepdims=True))
        s = jnp.where(sel, -jnp.inf, s)
    return jnp.concatenate(vals, axis=0), jnp.concatenate(picks, axis=0)


def _topk_kernel(sc_ref, off_ref, par_ref, gate_ref):
    K = PEER_TOPK
    idx_rows, gate_rows = [], []
    for hd in range(PEER_HEADS):
        s1, i1 = _topk_rows(sc_ref[2 * hd], K)
        s2, i2 = _topk_rows(sc_ref[2 * hd + 1], K)
        cand = jnp.concatenate([s1[a:a + 1, :] + s2 for a in range(K)], axis=0)
        cidx = jnp.concatenate([i1[a:a + 1, :] * N_KEYS + i2 for a in range(K)], axis=0)
        best, eidx = _topk_rows(cand, K, payload=cidx)
        p = jnp.exp(best - best[0:1, :])
        gate_rows.append(p / jnp.sum(p, axis=0, keepdims=True))
        idx_rows.append(eidx)
    idx = jnp.concatenate(idx_rows, axis=0).T
    off_ref[...] = lax.shift_right_logical(idx, 1) * SUBLANES
    par_ref[...] = (idx & 1).astype(F32)
    gate_ref[...] = jnp.concatenate(gate_rows, axis=0).T


def _topk(scores):
    G, NK, T = scores.shape
    tm = 256
    out = pl.BlockSpec((tm, PEER_E), lambda i: (i, 0))
    return pl.pallas_call(
        _topk_kernel,
        grid=(T // tm,),
        in_specs=[pl.BlockSpec((G, NK, tm), lambda i: (0, 0, i))],
        out_specs=[out, out, out],
        out_shape=[jax.ShapeDtypeStruct((T, PEER_E), jnp.int32), jax.ShapeDtypeStruct((T, PEER_E), F32),
                   jax.ShapeDtypeStruct((T, PEER_E), F32)],
        compiler_params=_cparams(1),
        name="topk",
    )(scores)


TILE_ROWS = 2 * SUBLANES
STACK = PEER_E * TILE_ROWS


def _pack_table(w):
    n, d = w.shape
    bits = lax.bitcast_convert_type(w.astype(BF16), jnp.uint16).astype(jnp.uint32).reshape(n // 2, 2, d)
    packed = lax.shift_left(bits[:, 1, :], jnp.uint32(16)) | bits[:, 0, :]
    return packed.reshape(n // 2 * (d // LANES), LANES)


def _peer_consts():
    c = jnp.arange(STACK)
    expand = (c[None, :] // TILE_ROWS == jnp.arange(PEER_E)[:, None])
    fold = ((c[None, :] % TILE_ROWS) // 2 == jnp.arange(SUBLANES)[:, None])
    half = (c % 2).reshape(1, STACK)
    return expand.astype(BF16), expand.T.astype(F32), fold.astype(F32), half.astype(F32)


def _stack2(tab_ref, off_ref, t):
    def tile(o):
        return pltpu.bitcast(tab_ref[pl.ds(pl.multiple_of(o, SUBLANES), SUBLANES), :], BF16)
    w0 = jnp.concatenate([tile(off_ref[t, j]) for j in range(PEER_E)], axis=0)
    w1 = jnp.concatenate([tile(off_ref[t + 1, j]) for j in range(PEER_E)], axis=0)
    return jnp.concatenate([w0, w1], axis=1)


def _expand_sel(par_ref, expand_ref, half_ref):
    par = _dot(par_ref[...].astype(BF16), expand_ref[...])
    return jnp.where(par == half_ref[...], 1.0, 0.0)


def _peer_u_kernel(off_ref, par_ref, x_ref, gate_ref, expand_ref, collapse_ref, fold_ref, half_ref, tab_ref,
                   act_ref, sel_ref, g_ref):
    tm = x_ref.shape[0]
    sel_ref[...] = _expand_sel(par_ref, expand_ref, half_ref)
    fold = fold_ref[...]
    zero = jnp.zeros((SUBLANES, LANES), BF16)

    def pair(i, _):
        t = 2 * i
        w2 = _stack2(tab_ref, off_ref, t)
        lhs = jnp.concatenate([jnp.concatenate([x_ref[t], zero], axis=1),
                               jnp.concatenate([zero, x_ref[t + 1]], axis=1)], axis=0)
        g = _dot_nt(lhs, w2)
        g_ref[pl.ds(t, 1), :] = jnp.sum(g[0:SUBLANES] * fold, axis=0, keepdims=True) * sel_ref[pl.ds(t, 1), :]
        g_ref[pl.ds(t + 1, 1), :] = (jnp.sum(g[SUBLANES:] * fold, axis=0, keepdims=True)
                                     * sel_ref[pl.ds(t + 1, 1), :])
        return 0

    lax.fori_loop(0, tm // 2, pair, 0, unroll=2)
    a = jnp.dot(g_ref[...], collapse_ref[...], preferred_element_type=F32, precision=HIGHEST)
    gelu = 0.5 * a * (1.0 + lax.erf(a * (2.0 ** -0.5)))
    act_ref[...] = gate_ref[...] * gelu


def _peer_v_kernel(off_ref, par_ref, act_ref, expand_ref, fold_ref, half_ref, tab_ref, y_ref, ce_ref):
    tm = y_ref.shape[0]
    ce_ref[...] = _dot(act_ref[...].astype(BF16), expand_ref[...]) * _expand_sel(par_ref, expand_ref, half_ref)
    fold = fold_ref[...]

    def pair(i, _):
        t = 2 * i
        w2 = _stack2(tab_ref, off_ref, t)
        coef = jnp.concatenate([(ce_ref[pl.ds(t, 1), :] * fold).astype(BF16),
                                (ce_ref[pl.ds(t + 1, 1), :] * fold).astype(BF16)], axis=0)
        out = _dot(coef, w2)
        y_ref[t] = out[0:SUBLANES, 0:LANES]
        y_ref[t + 1] = out[SUBLANES:, LANES:]
        return 0

    lax.fori_loop(0, tm // 2, pair, 0, unroll=2)


PEER_TM = 64


def _peer_specs(tab):
    tm = PEER_TM
    smem = pl.BlockSpec((tm, PEER_E), lambda i: (i, 0), memory_space=pltpu.SMEM)
    row = pl.BlockSpec((tm, PEER_E), lambda i: (i, 0))
    const = lambda shape: pl.BlockSpec(shape, lambda i: (0,) * len(shape))
    table = pl.BlockSpec(tab.shape, lambda i: (0, 0), pipeline_mode=pl.Buffered(1))
    return tm, smem, row, const, table


def _peer_u(off, par, h2f, gate, tab):
    T = off.shape[0]
    tm, smem, row, const, table = _peer_specs(tab)
    expand, collapse, fold, half = _peer_consts()
    return pl.pallas_call(
        _peer_u_kernel,
        grid=(T // tm,),
        in_specs=[smem, row, pl.BlockSpec((tm, SUBLANES, LANES), lambda i: (i, 0, 0)), row,
                  const(expand.shape), const(collapse.shape), const(fold.shape), const(half.shape), table],
        out_specs=row,
        out_shape=jax.ShapeDtypeStruct((T, PEER_E), F32),
        scratch_shapes=[pltpu.VMEM((tm, STACK), F32), pltpu.VMEM((tm, STACK), F32)],
        compiler_params=_cparams(1),
        name="peer_u",
    )(off, par, h2f, gate, expand, collapse, fold, half, tab)


def _peer_v(off, par, act, tab):
    T = off.shape[0]
    tm, smem, row, const, table = _peer_specs(tab)
    expand, _, fold, half = _peer_consts()
    return pl.pallas_call(
        _peer_v_kernel,
        grid=(T // tm,),
        in_specs=[smem, row, row, const(expand.shape), const(fold.shape), const(half.shape), table],
        out_specs=pl.BlockSpec((tm, SUBLANES, LANES), lambda i: (i, 0, 0)),
        out_shape=jax.ShapeDtypeStruct((T, SUBLANES, LANES), F32),
        scratch_shapes=[pltpu.VMEM((tm, STACK), F32)],
        compiler_params=_cparams(1),
        name="peer_v",
    )(off, par, act, expand, fold, half, tab)


def _final_kernel(x1_ref, y_ref, mod_ref, g_ref, o_ref):
    D = x1_ref.shape[1]
    gate_f = mod_ref[0, :, 5 * D:6 * D]
    o_ref[...] = x1_ref[...] + gate_f * _rms(y_ref[...], g_ref[...])


def _final(x1, y, mod3, g, B, S):
    T, D = x1.shape
    tm = 512
    tpb = S // tm
    row = pl.BlockSpec((tm, D), lambda i: (i, 0))
    return pl.pallas_call(
        _final_kernel,
        grid=(T // tm,),
        in_specs=[row, row, pl.BlockSpec((1, 1, N_MOD * D), lambda i: (i // tpb, 0, 0)),
                  pl.BlockSpec((1, D), lambda i: (0, 0))],
        out_specs=row,
        out_shape=jax.ShapeDtypeStruct((T, D), F32),
        compiler_params=_cparams(1),
        name="final",
    )(x1, y, mod3, g)


def kernel(x, c, w_ada, b_ada, pre_mix_g, post_mix_g, w_in, b_fox_f, hgrn_gamma, hgrn_norm_g, w_out, pre_ffn_g,
           post_ffn_g, peer_w_q, peer_sub_keys, peer_u, peer_v):
    B, S, D = x.shape
    T = B * S
    depth = w_in.shape[0]
    assert depth == 1, "single-layer block"
    l = 0
    x2d = x.reshape(T, D)

    mod3 = _ada(c, w_ada[l], b_ada[l]).reshape(B, 1, N_MOD * D)

    w = w_in[l]
    o1 = 3 * FOX_WIDTH
    wqkv = w[:, :o1].astype(BF16)
    wf = jnp.pad(w[:, o1:o1 + FOX_HEADS], ((0, 0), (0, LANES - FOX_HEADS))).astype(BF16)
    bf = jnp.pad(b_fox_f[l].astype(F32), (0, LANES - FOX_HEADS)).reshape(1, LANES)
    wh = w[:, o1 + FOX_HEADS:].astype(BF16)

    q, k, v, fc, hq, hf, hi, hg = _proj(x2d, mod3, pre_mix_g[l].reshape(1, D), wqkv, wf, bf, wh, B, S)

    frow = fc.reshape(B, S, FOX_HEADS).transpose(0, 2, 1)
    o_fox = _fox(q, k, v, frow.reshape(B, FOX_HEADS, S, 1), frow.reshape(B, FOX_HEADS, 1, S))

    o_h = _hgrn(hq, hf, hi, hg, hgrn_gamma[l:l + 2].astype(F32), hgrn_norm_g[l].reshape(1, HGRN_WIDTH), B, S)

    wo = w_out[l].astype(BF16)
    wfo = wo[:FOX_WIDTH].reshape(FOX_HEADS, FOX_HEAD_DIM, D)
    who = wo[FOX_WIDTH:]
    keys = peer_sub_keys[l].reshape(2 * PEER_HEADS, N_KEYS, PEER_HALF).astype(BF16)
    x1, h2, scores = _outproj(x2d, o_fox, o_h, mod3, wfo, who, post_mix_g[l].reshape(1, D),
                              pre_ffn_g[l].reshape(1, D), peer_w_q[l].astype(BF16), keys, B, S)

    off, par, gate = _topk(scores)

    act = _peer_u(off, par, h2.reshape(T, SUBLANES, LANES), gate, _pack_table(peer_u[l]))
    y = _peer_v(off, par, act, _pack_table(peer_v[l]))

    out = _final(x1, y.reshape(T, D), mod3, post_ffn_g[l].reshape(1, D), B, S)
    return out.reshape(B, S, D)
```

```python
import functools
import math

import jax
import jax.numpy as jnp
from jax import lax
from jax.experimental import pallas as pl
from jax.experimental.pallas import tpu as pltpu

F32 = jnp.float32
BF16 = jnp.bfloat16
EPS = 1e-6
HIGHEST = lax.Precision.HIGHEST

FOX_HEADS = 8
FOX_HEAD_DIM = 64
FOX_WIDTH = FOX_HEADS * FOX_HEAD_DIM
HGRN_HEADS = 4
HGRN_HEAD_DIM = 128
HGRN_WIDTH = HGRN_HEADS * HGRN_HEAD_DIM
HGRN_CHUNK = 16
PEER_HEADS = 8
PEER_QUERY_DIM = 256
PEER_HALF = PEER_QUERY_DIM // 2
N_KEYS = 128
PEER_TOPK = 16
PEER_E = PEER_HEADS * PEER_TOPK
N_MOD = 6

LANES = 128
SUBLANES = 8
VMEM_LIMIT = 56 * 1024 * 1024


def _cparams(n_axes, vmem=VMEM_LIMIT):
    return pltpu.CompilerParams(dimension_semantics=("arbitrary",) * n_axes, vmem_limit_bytes=vmem)


def _dot(a, b):
    return jnp.dot(a, b, preferred_element_type=F32)


def _dot_nt(a, b):
    return lax.dot_general(a, b, (((1,), (1,)), ((), ())), preferred_element_type=F32)


def _dot_tn(a, b):
    return lax.dot_general(a, b, (((0,), (0,)), ((), ())), preferred_element_type=F32)


def _rms(x, g):
    return x * lax.rsqrt(jnp.mean(x * x, axis=-1, keepdims=True) + EPS) * g


def _ada_kernel(c_ref, w_ref, b_ref, o_ref):
    c = c_ref[...]
    cond = c * jax.nn.sigmoid(c)
    o_ref[...] = jnp.dot(cond, w_ref[...], preferred_element_type=F32, precision=HIGHEST) + b_ref[...]


def _ada(c, w, b):
    B, D = c.shape
    N = w.shape[1]
    tn = 1024
    return pl.pallas_call(
        _ada_kernel,
        grid=(N // tn,),
        in_specs=[pl.BlockSpec((B, D), lambda j: (0, 0)),
                  pl.BlockSpec((D, tn), lambda j: (0, j)),
                  pl.BlockSpec((1, tn), lambda j: (0, j))],
        out_specs=pl.BlockSpec((B, tn), lambda j: (0, j)),
        out_shape=jax.ShapeDtypeStruct((B, N), F32),
        compiler_params=_cparams(1),
        name="ada",
    )(c, w, b.reshape(1, N))


def _split3(f):
    hi = f.astype(BF16).astype(F32)
    r = f - hi
    mid = r.astype(BF16).astype(F32)
    return hi, mid, r - mid


def _proj_kernel(x_ref, mod_ref, g_ref, wqkv_ref, wf_ref, bf_ref, wh_ref, tri_ref,
                 qt_ref, ka_ref, vt_ref, hq_ref, hf_ref, hi_ref, hg_ref, carry_ref,
                 *, tiles_per_batch):
    i = pl.program_id(0)
    D = x_ref.shape[1]
    tm = x_ref.shape[0]
    Dh = FOX_HEAD_DIM
    x = x_ref[...]
    shift = mod_ref[0, :, 0:D]
    scale = mod_ref[0, :, D:2 * D]
    h = (_rms(x, g_ref[...]) * (1.0 + scale) + shift).astype(BF16)

    ff = _dot(h, wf_ref[...]) + bf_ref[...]
    logf = jnp.minimum(ff, 0.0) - jnp.log(1.0 + jnp.exp(-jnp.abs(ff)))

    @pl.when(i % tiles_per_batch == 0)
    def _():
        carry_ref[...] = jnp.zeros_like(carry_ref)

    cs = jnp.dot(tri_ref[...], logf, preferred_element_type=F32, precision=HIGHEST) + carry_ref[...]
    carry_ref[...] = cs[tm - 1:tm, :]

    qkv = _dot(h, wqkv_ref[...])
    lane = lax.broadcasted_iota(jnp.int32, (tm, Dh), 1)
    zpad = jnp.zeros((tm, Dh), F32)
    for hd in range(FOX_HEADS):
        lo = hd * Dh
        fhi, fmid, flo = _split3(cs[:, hd:hd + 1])
        pieces = lambda o: jnp.where(lane == o, fhi, jnp.where(lane == o + 1, fmid, jnp.where(lane == o + 2, flo, 0.0)))
        q_aux = jnp.where(lane < 3, -1.0, pieces(3))
        k_aux = jnp.where((lane >= 3) & (lane < 6), 1.0, pieces(0))
        qa = jnp.concatenate([qkv[:, lo:lo + Dh] * (Dh ** -0.5), q_aux], axis=1)
        ka = jnp.concatenate([qkv[:, FOX_WIDTH + lo:FOX_WIDTH + lo + Dh], k_aux], axis=1)
        va = jnp.concatenate([qkv[:, 2 * FOX_WIDTH + lo:2 * FOX_WIDTH + lo + Dh], zpad], axis=1)
        qt_ref[0, hd] = qa.T.astype(BF16)
        ka_ref[0, hd] = ka.astype(BF16)
        vt_ref[0, hd] = va.T[0:Dh, :].astype(BF16)

    hh = _dot(h, wh_ref[...])
    hq_ref[...] = hh[:, 0:HGRN_WIDTH]
    hf_ref[...] = hh[:, HGRN_WIDTH:2 * HGRN_WIDTH]
    hi_ref[...] = hh[:, 2 * HGRN_WIDTH:3 * HGRN_WIDTH]
    hg_ref[...] = hh[:, 3 * HGRN_WIDTH:4 * HGRN_WIDTH]


def _proj(x2d, mod3, g, wqkv, wf, bf, wh, B, S):
    T, D = x2d.shape
    tm = 256
    tpb = S // tm
    tri = (jnp.arange(tm)[:, None] >= jnp.arange(tm)[None, :]).astype(F32)
    H, Dh = FOX_HEADS, FOX_HEAD_DIM
    fox_shapes = [jax.ShapeDtypeStruct((B, H, LANES, S), BF16), jax.ShapeDtypeStruct((B, H, S, LANES), BF16),
                  jax.ShapeDtypeStruct((B, H, Dh, S), BF16)]
    fox_specs = [pl.BlockSpec((1, H, LANES, tm), lambda i: (i // tpb, 0, 0, i % tpb)),
                 pl.BlockSpec((1, H, tm, LANES), lambda i: (i // tpb, 0, i % tpb, 0)),
                 pl.BlockSpec((1, H, Dh, tm), lambda i: (i // tpb, 0, 0, i % tpb))]
    wide = jax.ShapeDtypeStruct((T, HGRN_WIDTH), F32)
    wspec = pl.BlockSpec((tm, HGRN_WIDTH), lambda i: (i, 0))
    const = lambda shape: pl.BlockSpec(shape, lambda i: (0,) * len(shape))
    return pl.pallas_call(
        functools.partial(_proj_kernel, tiles_per_batch=tpb),
        grid=(T // tm,),
        in_specs=[pl.BlockSpec((tm, D), lambda i: (i, 0)),
                  pl.BlockSpec((1, 1, N_MOD * D), lambda i: (i // tpb, 0, 0)),
                  const((1, D)), const(wqkv.shape), const(wf.shape), const(bf.shape), const(wh.shape),
                  const((tm, tm))],
        out_specs=fox_specs + [wspec, wspec, wspec, wspec],
        out_shape=fox_shapes + [wide, wide, wide, wide],
        scratch_shapes=[pltpu.VMEM((1, LANES), F32)],
        compiler_params=_cparams(1),
        name="proj",
    )(x2d, mod3, g, wqkv, wf, bf, wh, tri)


def _fox_kernel(qt_ref, ka_ref, vt_ref, o_ref, m_ref, l_ref, acc_ref, *, tq):
    qi = pl.program_id(2)
    qt = qt_ref[0, 0]
    m_ref[...] = jnp.full_like(m_ref, -jnp.inf)
    l_ref[...] = jnp.zeros_like(l_ref)
    acc_ref[...] = jnp.zeros_like(acc_ref)

    def step(j, masked):
        off = pl.multiple_of(j * tq, tq)
        s = _dot(ka_ref[0, 0, pl.ds(off, tq), :], qt)
        if masked:
            key = lax.broadcasted_iota(jnp.int32, (tq, tq), 0)
            qry = lax.broadcasted_iota(jnp.int32, (tq, tq), 1)
            s = jnp.where(key <= qry, s, -jnp.inf)
        m_old = m_ref[...]
        m_new = jnp.maximum(m_old, jnp.max(s, axis=0, keepdims=True))
        p = jnp.exp(s - m_new)
        alpha = jnp.exp(m_old - m_new)
        l_ref[...] = alpha * l_ref[...] + jnp.sum(p, axis=0, keepdims=True)
        acc_ref[...] = alpha * acc_ref[...] + _dot(vt_ref[0, 0, :, pl.ds(off, tq)], p.astype(BF16))
        m_ref[...] = m_new

    step(qi, True)

    def body(j, _):
        step(j, False)
        return 0

    lax.fori_loop(0, qi, body, 0)
    o_ref[0, 0] = (acc_ref[...] / l_ref[...]).T.astype(o_ref.dtype)


def _fox(qt, ka, vt):
    B, H, S, _ = ka.shape
    Dh = vt.shape[2]
    tq = 512
    return pl.pallas_call(
        functools.partial(_fox_kernel, tq=tq),
        grid=(B, H, S // tq),
        in_specs=[pl.BlockSpec((1, 1, LANES, tq), lambda b, h, i: (b, h, 0, i)),
                  pl.BlockSpec((1, 1, S, LANES), lambda b, h, i: (b, h, 0, 0)),
                  pl.BlockSpec((1, 1, Dh, S), lambda b, h, i: (b, h, 0, 0))],
        out_specs=pl.BlockSpec((1, 1, tq, Dh), lambda b, h, i: (b, h, i, 0)),
        out_shape=jax.ShapeDtypeStruct((B, H, S, Dh), BF16),
        scratch_shapes=[pltpu.VMEM((1, tq), F32), pltpu.VMEM((1, tq), F32), pltpu.VMEM((Dh, tq), F32)],
        compiler_params=_cparams(3),
        name="fox",
    )(qt, ka, vt)


def _hgrn_kernel(hq_ref, hf_ref, hi_ref, hg_ref, gam_ref, ng_ref, bd_ref, bl_ref, o_ref, st_ref, os_ref):
    j = pl.program_id(1)
    tb = hq_ref.shape[0]
    C = HGRN_CHUNK
    Dk = HGRN_HEAD_DIM

    @pl.when(j == 0)
    def _():
        st_ref[...] = jnp.zeros_like(st_ref)

    gam = gam_ref[...]
    e = jnp.exp(gam - jnp.max(gam, axis=0, keepdims=True))
    lb = e[0:1, :] / jnp.sum(e, axis=0, keepdims=True)
    f = lb + (1.0 - lb) * jax.nn.sigmoid(hf_ref[...])
    lf = jnp.log(f)
    A = jnp.dot(bd_ref[...], lf, preferred_element_type=F32, precision=HIGHEST)
    AL = jnp.dot(bl_ref[...], lf, preferred_element_type=F32, precision=HIGHEST)
    q = hq_ref[...] * (Dk ** -0.5)
    kk = 1.0 - f
    qa = q * jnp.exp(A)
    ka = kk * jnp.exp(AL - A)
    eal = jnp.exp(AL)
    inp = hi_ref[...]

    tri3 = (lax.broadcasted_iota(jnp.int32, (C, C, Dk), 0) >= lax.broadcasted_iota(jnp.int32, (C, C, Dk), 1))
    for hd in range(HGRN_HEADS):
        lo = hd * Dk
        st = st_ref[hd]
        for c in range(tb // C):
            r0 = c * C
            Ac = A[r0:r0 + C, lo:lo + Dk]
            qc = q[r0:r0 + C, lo:lo + Dk]
            kc = kk[r0:r0 + C, lo:lo + Dk]
            ic = inp[r0:r0 + C, lo:lo + Dk]
            diff = Ac[:, None, :] - Ac[None, :, :]
            dec = jnp.exp(jnp.where(tri3, diff, -jnp.inf))
            sc = jnp.sum(qc[:, None, :] * kc[None, :, :] * dec, axis=-1)
            o = _dot_nt(qa[r0:r0 + C, lo:lo + Dk], st) + _dot(sc, ic)
            st = st * eal[r0:r0 + 1, lo:lo + Dk] + _dot_tn(ic, ka[r0:r0 + C, lo:lo + Dk])
            os_ref[r0:r0 + C, lo:lo + Dk] = o
        st_ref[hd] = st

    o_all = os_ref[...]
    hg = hg_ref[...]
    ng = ng_ref[...]
    for hd in range(HGRN_HEADS):
        lo = hd * Dk
        oh = _rms(o_all[:, lo:lo + Dk], ng[:, lo:lo + Dk])
        g = hg[:, lo:lo + Dk]
        o_ref[:, lo:lo + Dk] = (oh * (g * jax.nn.sigmoid(g))).astype(o_ref.dtype)


def _hgrn(hq, hf, hi, hg, gamma, norm_g, B, S):
    T, W = hq.shape
    tb = 128
    nb = S // tb
    t = jnp.arange(tb)
    same = (t[:, None] // HGRN_CHUNK) == (t[None, :] // HGRN_CHUNK)
    bd = (same & (t[:, None] >= t[None, :])).astype(F32)
    bl = same.astype(F32)
    wspec = pl.BlockSpec((tb, W), lambda b, j: (b * nb + j, 0))
    const = lambda shape: pl.BlockSpec(shape, lambda b, j: (0,) * len(shape))
    return pl.pallas_call(
        _hgrn_kernel,
        grid=(B, nb),
        in_specs=[wspec, wspec, wspec, wspec, const(gamma.shape), const((1, W)), const((tb, tb)), const((tb, tb))],
        out_specs=wspec,
        out_shape=jax.ShapeDtypeStruct((T, W), BF16),
        scratch_shapes=[pltpu.VMEM((HGRN_HEADS, HGRN_HEAD_DIM, HGRN_HEAD_DIM), F32), pltpu.VMEM((tb, W), F32)],
        compiler_params=_cparams(2),
        name="hgrn",
    )(hq, hf, hi, hg, gamma, norm_g, bd, bl)


def _outproj_kernel(x_ref, of_ref, oh_ref, mod_ref, wfo_ref, who_ref, pmg_ref, pfg_ref, wq_ref, keys_ref,
                    x1_ref, h2_ref, sc_ref):
    D = x_ref.shape[1]
    mix = _dot(oh_ref[...], who_ref[...])
    for hd in range(FOX_HEADS):
        mix = mix + _dot(of_ref[0, hd], wfo_ref[hd])
    gate_a = mod_ref[0, :, 2 * D:3 * D]
    shift_f = mod_ref[0, :, 3 * D:4 * D]
    scale_f = mod_ref[0, :, 4 * D:5 * D]
    x1 = x_ref[...] + gate_a * _rms(mix, pmg_ref[...])
    x1_ref[...] = x1
    h2 = _rms(x1, pfg_ref[...]) * (1.0 + scale_f) + shift_f
    h2b = h2.astype(BF16)
    h2_ref[...] = h2b
    qp = _dot(h2b, wq_ref[...])
    for g in range(2 * PEER_HEADS):
        qg = qp[:, g * PEER_HALF:(g + 1) * PEER_HALF]
        qn = qg * lax.rsqrt(jnp.mean(qg * qg, axis=-1, keepdims=True) + EPS)
        sc_ref[g] = _dot_nt(keys_ref[g], qn.astype(BF16))


def _outproj(x2d, ofox, oh, mod3, wfo, who, pmg, pfg, wq, keys, B, S):
    T, D = x2d.shape
    tm = 256
    tpb = S // tm
    G = 2 * PEER_HEADS
    const = lambda shape: pl.BlockSpec(shape, lambda i: (0,) * len(shape))
    row = pl.BlockSpec((tm, D), lambda i: (i, 0))
    return pl.pallas_call(
        _outproj_kernel,
        grid=(T // tm,),
        in_specs=[row,
                  pl.BlockSpec((1, FOX_HEADS, tm, FOX_HEAD_DIM), lambda i: (i // tpb, 0, i % tpb, 0)),
                  pl.BlockSpec((tm, HGRN_WIDTH), lambda i: (i, 0)),
                  pl.BlockSpec((1, 1, N_MOD * D), lambda i: (i // tpb, 0, 0)),
                  const(wfo.shape), const(who.shape), const((1, D)), const((1, D)), const(wq.shape), const(keys.shape)],
        out_specs=[row, row, pl.BlockSpec((G, N_KEYS, tm), lambda i: (0, 0, i))],
        out_shape=[jax.ShapeDtypeStruct((T, D), F32), jax.ShapeDtypeStruct((T, D), BF16),
                   jax.ShapeDtypeStruct((G, N_KEYS, T), F32)],
        compiler_params=_cparams(1),
        name="outproj",
    )(x2d, ofox, oh, mod3, wfo, who, pmg, pfg, wq, keys)


def _topk_rows(s, k, payload=None):
    n = s.shape[0]
    iota = lax.broadcasted_iota(jnp.int32, s.shape, 0)
    vals, picks = [], []
    for _ in range(k):
        m = jnp.max(s, axis=0, keepdims=True)
        ix = jnp.min(jnp.where(s == m, iota, n), axis=0, keepdims=True)
        sel = iota == ix
        vals.append(m)
        if payload is None:
            picks.append(ix)
        else:
            picks.append(jnp.sum(jnp.where(sel, payload, 0), axis=0, keepdims=True))
        s = jnp.where(sel, -jnp.inf, s)
    return jnp.concatenate(vals, axis=0), jnp.concatenate(picks, axis=0)


def _topk_kernel(sc_ref, off_ref, par_ref, gate_ref):
    K = PEER_TOPK
    idx_rows, gate_rows = [], []
    for hd in range(PEER_HEADS):
        s1, i1 = _topk_rows(sc_ref[2 * hd], K)
        s2, i2 = _topk_rows(sc_ref[2 * hd + 1], K)
        cand = jnp.concatenate([s1[a:a + 1, :] + s2 for a in range(K)], axis=0)
        cidx = jnp.concatenate([i1[a:a + 1, :] * N_KEYS + i2 for a in range(K)], axis=0)
        best, eidx = _topk_rows(cand, K, payload=cidx)
        p = jnp.exp(best - best[0:1, :])
        gate_rows.append(p / jnp.sum(p, axis=0, keepdims=True))
        idx_rows.append(eidx)
    idx = jnp.concatenate(idx_rows, axis=0).T
    off_ref[...] = lax.shift_right_logical(idx, 1) * SUBLANES
    par_ref[...] = (idx & 1).astype(F32)
    gate_ref[...] = jnp.concatenate(gate_rows, axis=0).T


def _topk(scores):
    G, NK, T = scores.shape
    tm = 256
    out = pl.BlockSpec((tm, PEER_E), lambda i: (i, 0))
    return pl.pallas_call(
        _topk_kernel,
        grid=(T // tm,),
        in_specs=[pl.BlockSpec((G, NK, tm), lambda i: (0, 0, i))],
        out_specs=[out, out, out],
        out_shape=[jax.ShapeDtypeStruct((T, PEER_E), jnp.int32), jax.ShapeDtypeStruct((T, PEER_E), F32),
                   jax.ShapeDtypeStruct((T, PEER_E), F32)],
        compiler_params=_cparams(1),
        name="topk",
    )(scores)


TILE_ROWS = 2 * SUBLANES
STACK = PEER_E * TILE_ROWS


def _pack_table(w):
    n, d = w.shape
    bits = lax.bitcast_convert_type(w.astype(BF16), jnp.uint16).astype(jnp.uint32).reshape(n // 2, 2, d)
    packed = lax.shift_left(bits[:, 1, :], jnp.uint32(16)) | bits[:, 0, :]
    return packed.reshape(n // 2 * (d // LANES), LANES)


def _peer_consts():
    c = jnp.arange(STACK)
    expand = (c[None, :] // TILE_ROWS == jnp.arange(PEER_E)[:, None])
    fold = ((c[None, :] % TILE_ROWS) // 2 == jnp.arange(SUBLANES)[:, None])
    half = (c % 2).reshape(1, STACK)
    return expand.astype(BF16), expand.T.astype(F32), fold.astype(F32), half.astype(F32)


def _stack2(tab_ref, off_ref, t):
    def tile(o):
        return pltpu.bitcast(tab_ref[pl.ds(pl.multiple_of(o, SUBLANES), SUBLANES), :], BF16)
    w0 = jnp.concatenate([tile(off_ref[t, j]) for j in range(PEER_E)], axis=0)
    w1 = jnp.concatenate([tile(off_ref[t + 1, j]) for j in range(PEER_E)], axis=0)
    return jnp.concatenate([w0, w1], axis=1)


def _expand_sel(par_ref, expand_ref, half_ref):
    par = _dot(par_ref[...].astype(BF16), expand_ref[...])
    return jnp.where(par == half_ref[...], 1.0, 0.0)


def _peer_u_kernel(off_ref, par_ref, x_ref, gate_ref, expand_ref, collapse_ref, fold_ref, half_ref, tab_ref,
                   act_ref, sel_ref, g_ref):
    tm = x_ref.shape[0]
    sel_ref[...] = _expand_sel(par_ref, expand_ref, half_ref)
    fold = fold_ref[...]
    zero = jnp.zeros((SUBLANES, LANES), BF16)

    def pair(i, _):
        t = 2 * i
        w2 = _stack2(tab_ref, off_ref, t)
        lhs = jnp.concatenate([jnp.concatenate([x_ref[t], zero], axis=1),
                               jnp.concatenate([zero, x_ref[t + 1]], axis=1)], axis=0)
        g = _dot_nt(lhs, w2)
        g_ref[pl.ds(t, 1), :] = jnp.sum(g[0:SUBLANES] * fold, axis=0, keepdims=True) * sel_ref[pl.ds(t, 1), :]
        g_ref[pl.ds(t + 1, 1), :] = (jnp.sum(g[SUBLANES:] * fold, axis=0, keepdims=True)
                                     * sel_ref[pl.ds(t + 1, 1), :])
        return 0

    lax.fori_loop(0, tm // 2, pair, 0, unroll=2)
    a = jnp.dot(g_ref[...], collapse_ref[...], preferred_element_type=F32, precision=HIGHEST)
    gelu = 0.5 * a * (1.0 + lax.erf(a * (2.0 ** -0.5)))
    act_ref[...] = gate_ref[...] * gelu


def _peer_v_kernel(off_ref, par_ref, act_ref, expand_ref, fold_ref, half_ref, tab_ref, y_ref, ce_ref):
    tm = y_ref.shape[0]
    ce_ref[...] = _dot(act_ref[...].astype(BF16), expand_ref[...]) * _expand_sel(par_ref, expand_ref, half_ref)
    fold = fold_ref[...]

    def pair(i, _):
        t = 2 * i
        w2 = _stack2(tab_ref, off_ref, t)
        coef = jnp.concatenate([(ce_ref[pl.ds(t, 1), :] * fold).astype(BF16),
                                (ce_ref[pl.ds(t + 1, 1), :] * fold).astype(BF16)], axis=0)
        out = _dot(coef, w2)
        y_ref[t] = out[0:SUBLANES, 0:LANES]
        y_ref[t + 1] = out[SUBLANES:, LANES:]
        return 0

    lax.fori_loop(0, tm // 2, pair, 0, unroll=2)


PEER_TM = 64


def _peer_specs(tab):
    tm = PEER_TM
    smem = pl.BlockSpec((tm, PEER_E), lambda i: (i, 0), memory_space=pltpu.SMEM)
    row = pl.BlockSpec((tm, PEER_E), lambda i: (i, 0))
    const = lambda shape: pl.BlockSpec(shape, lambda i: (0,) * len(shape))
    table = pl.BlockSpec(tab.shape, lambda i: (0, 0), pipeline_mode=pl.Buffered(1))
    return tm, smem, row, const, table


def _peer_u(off, par, h2f, gate, tab):
    T = off.shape[0]
    tm, smem, row, const, table = _peer_specs(tab)
    expand, collapse, fold, half = _peer_consts()
    return pl.pallas_call(
        _peer_u_kernel,
        grid=(T // tm,),
        in_specs=[smem, row, pl.BlockSpec((tm, SUBLANES, LANES), lambda i: (i, 0, 0)), row,
                  const(expand.shape), const(collapse.shape), const(fold.shape), const(half.shape), table],
        out_specs=row,
        out_shape=jax.ShapeDtypeStruct((T, PEER_E), F32),
        scratch_shapes=[pltpu.VMEM((tm, STACK), F32), pltpu.VMEM((tm, STACK), F32)],
        compiler_params=_cparams(1),
        name="peer_u",
    )(off, par, h2f, gate, expand, collapse, fold, half, tab)


def _peer_v(off, par, act, tab):
    T = off.shape[0]
    tm, smem, row, const, table = _peer_specs(tab)
    expand, _, fold, half = _peer_consts()
    return pl.pallas_call(
        _peer_v_kernel,
        grid=(T // tm,),
        in_specs=[smem, row, row, const(expand.shape), const(fold.shape), const(half.shape), table],
        out_specs=pl.BlockSpec((tm, SUBLANES, LANES), lambda i: (i, 0, 0)),
        out_shape=jax.ShapeDtypeStruct((T, SUBLANES, LANES), F32),
        scratch_shapes=[pltpu.VMEM((tm, STACK), F32)],
        compiler_params=_cparams(1),
        name="peer_v",
    )(off, par, act, expand, fold, half, tab)


def _final_kernel(x1_ref, y_ref, mod_ref, g_ref, o_ref):
    D = x1_ref.shape[1]
    gate_f = mod_ref[0, :, 5 * D:6 * D]
    o_ref[...] = x1_ref[...] + gate_f * _rms(y_ref[...], g_ref[...])


def _final(x1, y, mod3, g, B, S):
    T, D = x1.shape
    tm = 512
    tpb = S // tm
    row = pl.BlockSpec((tm, D), lambda i: (i, 0))
    return pl.pallas_call(
        _final_kernel,
        grid=(T // tm,),
        in_specs=[row, row, pl.BlockSpec((1, 1, N_MOD * D), lambda i: (i // tpb, 0, 0)),
                  pl.BlockSpec((1, D), lambda i: (0, 0))],
        out_specs=row,
        out_shape=jax.ShapeDtypeStruct((T, D), F32),
        compiler_params=_cparams(1),
        name="final",
    )(x1, y, mod3, g)


def kernel(x, c, w_ada, b_ada, pre_mix_g, post_mix_g, w_in, b_fox_f, hgrn_gamma, hgrn_norm_g, w_out, pre_ffn_g,
           post_ffn_g, peer_w_q, peer_sub_keys, peer_u, peer_v):
    B, S, D = x.shape
    T = B * S
    depth = w_in.shape[0]
    assert depth == 1, "single-layer block"
    l = 0
    x2d = x.reshape(T, D)

    mod3 = _ada(c, w_ada[l], b_ada[l]).reshape(B, 1, N_MOD * D)

    w = w_in[l]
    o1 = 3 * FOX_WIDTH
    wqkv = w[:, :o1].astype(BF16)
    wf = jnp.pad(w[:, o1:o1 + FOX_HEADS], ((0, 0), (0, LANES - FOX_HEADS))).astype(BF16)
    bf = jnp.pad(b_fox_f[l].astype(F32), (0, LANES - FOX_HEADS)).reshape(1, LANES)
    wh = w[:, o1 + FOX_HEADS:].astype(BF16)

    qt, ka, vt, hq, hf, hi, hg = _proj(x2d, mod3, pre_mix_g[l].reshape(1, D), wqkv, wf, bf, wh, B, S)

    o_fox = _fox(qt, ka, vt)

    o_h = _hgrn(hq, hf, hi, hg, hgrn_gamma[l:l + 2].astype(F32), hgrn_norm_g[l].reshape(1, HGRN_WIDTH), B, S)

    wo = w_out[l].astype(BF16)
    wfo = wo[:FOX_WIDTH].reshape(FOX_HEADS, FOX_HEAD_DIM, D)
    who = wo[FOX_WIDTH:]
    keys = peer_sub_keys[l].reshape(2 * PEER_HEADS, N_KEYS, PEER_HALF).astype(BF16)
    x1, h2, scores = _outproj(x2d, o_fox, o_h, mod3, wfo, who, post_mix_g[l].reshape(1, D),
                              pre_ffn_g[l].reshape(1, D), peer_w_q[l].astype(BF16), keys, B, S)

    off, par, gate = _topk(scores)

    act = _peer_u(off, par, h2.reshape(T, SUBLANES, LANES), gate, _pack_table(peer_u[l]))
    y = _peer_v(off, par, act, _pack_table(peer_v[l]))

    out = _final(x1, y.reshape(T, D), mod3, post_ffn_g[l].reshape(1, D), B, S)
    return out.reshape(B, S, D)
```

```python
import functools
import math

import jax
import jax.numpy as jnp
from jax import lax
from jax.experimental import pallas as pl
from jax.experimental.pallas import tpu as pltpu

F32 = jnp.float32
BF16 = jnp.bfloat16
EPS = 1e-6
HIGHEST = lax.Precision.HIGHEST

FOX_HEADS = 8
FOX_HEAD_DIM = 64
FOX_WIDTH = FOX_HEADS * FOX_HEAD_DIM
HGRN_HEADS = 4
HGRN_HEAD_DIM = 128
HGRN_WIDTH = HGRN_HEADS * HGRN_HEAD_DIM
HGRN_CHUNK = 16
PEER_HEADS = 8
PEER_QUERY_DIM = 256
PEER_HALF = PEER_QUERY_DIM // 2
N_KEYS = 128
PEER_TOPK = 16
PEER_E = PEER_HEADS * PEER_TOPK
N_MOD = 6

LANES = 128
SUBLANES = 8
VMEM_LIMIT = 56 * 1024 * 1024


def _cparams(n_axes, vmem=VMEM_LIMIT):
    return pltpu.CompilerParams(dimension_semantics=("arbitrary",) * n_axes, vmem_limit_bytes=vmem)


def _dot(a, b):
    return jnp.dot(a, b, preferred_element_type=F32)


def _dot_nt(a, b):
    return lax.dot_general(a, b, (((1,), (1,)), ((), ())), preferred_element_type=F32)


def _dot_tn(a, b):
    return lax.dot_general(a, b, (((0,), (0,)), ((), ())), preferred_element_type=F32)


def _rms(x, g):
    return x * lax.rsqrt(jnp.mean(x * x, axis=-1, keepdims=True) + EPS) * g


def _ada_kernel(c_ref, w_ref, b_ref, o_ref):
    c = c_ref[...]
    cond = c * jax.nn.sigmoid(c)
    o_ref[...] = jnp.dot(cond, w_ref[...], preferred_element_type=F32, precision=HIGHEST) + b_ref[...]


def _ada(c, w, b):
    B, D = c.shape
    N = w.shape[1]
    tn = 1024
    return pl.pallas_call(
        _ada_kernel,
        grid=(N // tn,),
        in_specs=[pl.BlockSpec((B, D), lambda j: (0, 0)),
                  pl.BlockSpec((D, tn), lambda j: (0, j)),
                  pl.BlockSpec((1, tn), lambda j: (0, j))],
        out_specs=pl.BlockSpec((B, tn), lambda j: (0, j)),
        out_shape=jax.ShapeDtypeStruct((B, N), F32),
        compiler_params=_cparams(1),
        name="ada",
    )(c, w, b.reshape(1, N))


def _split3(f):
    hi = f.astype(BF16).astype(F32)
    r = f - hi
    mid = r.astype(BF16).astype(F32)
    return hi, mid, r - mid


def _proj_kernel(x_ref, mod_ref, g_ref, wqkv_ref, wf_ref, bf_ref, wh_ref, tri_ref,
                 qt_ref, ka_ref, vt_ref, hq_ref, hf_ref, hi_ref, hg_ref, carry_ref,
                 *, tiles_per_batch):
    i = pl.program_id(0)
    D = x_ref.shape[1]
    tm = x_ref.shape[0]
    Dh = FOX_HEAD_DIM
    x = x_ref[...]
    shift = mod_ref[0, :, 0:D]
    scale = mod_ref[0, :, D:2 * D]
    h = (_rms(x, g_ref[...]) * (1.0 + scale) + shift).astype(BF16)

    ff = _dot(h, wf_ref[...]) + bf_ref[...]
    logf = jnp.minimum(ff, 0.0) - jnp.log(1.0 + jnp.exp(-jnp.abs(ff)))

    @pl.when(i % tiles_per_batch == 0)
    def _():
        carry_ref[...] = jnp.zeros_like(carry_ref)

    cs = jnp.dot(tri_ref[...], logf, preferred_element_type=F32, precision=HIGHEST) + carry_ref[...]
    carry_ref[...] = cs[tm - 1:tm, :]

    qkv = _dot(h, wqkv_ref[...])
    lane = lax.broadcasted_iota(jnp.int32, (tm, Dh), 1)
    zpad = jnp.zeros((tm, Dh), F32)
    for hd in range(FOX_HEADS):
        lo = hd * Dh
        fhi, fmid, flo = _split3(cs[:, hd:hd + 1])
        pieces = lambda o: jnp.where(lane == o, fhi, jnp.where(lane == o + 1, fmid, jnp.where(lane == o + 2, flo, 0.0)))
        q_aux = jnp.where(lane < 3, -1.0, pieces(3))
        k_aux = jnp.where((lane >= 3) & (lane < 6), 1.0, pieces(0))
        qa = jnp.concatenate([qkv[:, lo:lo + Dh] * (Dh ** -0.5), q_aux], axis=1)
        ka = jnp.concatenate([qkv[:, FOX_WIDTH + lo:FOX_WIDTH + lo + Dh], k_aux], axis=1)
        va = jnp.concatenate([qkv[:, 2 * FOX_WIDTH + lo:2 * FOX_WIDTH + lo + Dh], zpad], axis=1)
        qt_ref[0, hd] = qa.T.astype(BF16)
        ka_ref[0, hd] = ka.astype(BF16)
        vt_ref[0, hd] = va.T[0:Dh, :].astype(BF16)

    hh = _dot(h, wh_ref[...])
    hq_ref[...] = hh[:, 0:HGRN_WIDTH]
    hf_ref[...] = hh[:, HGRN_WIDTH:2 * HGRN_WIDTH]
    hi_ref[...] = hh[:, 2 * HGRN_WIDTH:3 * HGRN_WIDTH]
    hg_ref[...] = hh[:, 3 * HGRN_WIDTH:4 * HGRN_WIDTH]


def _proj(x2d, mod3, g, wqkv, wf, bf, wh, B, S):
    T, D = x2d.shape
    tm = 256
    tpb = S // tm
    tri = (jnp.arange(tm)[:, None] >= jnp.arange(tm)[None, :]).astype(F32)
    H, Dh = FOX_HEADS, FOX_HEAD_DIM
    fox_shapes = [jax.ShapeDtypeStruct((B, H, LANES, S), BF16), jax.ShapeDtypeStruct((B, H, S, LANES), BF16),
                  jax.ShapeDtypeStruct((B, H, Dh, S), BF16)]
    fox_specs = [pl.BlockSpec((1, H, LANES, tm), lambda i: (i // tpb, 0, 0, i % tpb)),
                 pl.BlockSpec((1, H, tm, LANES), lambda i: (i // tpb, 0, i % tpb, 0)),
                 pl.BlockSpec((1, H, Dh, tm), lambda i: (i // tpb, 0, 0, i % tpb))]
    wide = jax.ShapeDtypeStruct((T, HGRN_WIDTH), F32)
    wspec = pl.BlockSpec((tm, HGRN_WIDTH), lambda i: (i, 0))
    const = lambda shape: pl.BlockSpec(shape, lambda i: (0,) * len(shape))
    return pl.pallas_call(
        functools.partial(_proj_kernel, tiles_per_batch=tpb),
        grid=(T // tm,),
        in_specs=[pl.BlockSpec((tm, D), lambda i: (i, 0)),
                  pl.BlockSpec((1, 1, N_MOD * D), lambda i: (i // tpb, 0, 0)),
                  const((1, D)), const(wqkv.shape), const(wf.shape), const(bf.shape), const(wh.shape),
                  const((tm, tm))],
        out_specs=fox_specs + [wspec, wspec, wspec, wspec],
        out_shape=fox_shapes + [wide, wide, wide, wide],
        scratch_shapes=[pltpu.VMEM((1, LANES), F32)],
        compiler_params=_cparams(1),
        name="proj",
    )(x2d, mod3, g, wqkv, wf, bf, wh, tri)


def _fox_kernel(qt_ref, ka_ref, vt_ref, o_ref, m_ref, l_ref, acc_ref, *, tq):
    qi = pl.program_id(2)
    qt = qt_ref[0, 0]
    m_ref[...] = jnp.full_like(m_ref, -jnp.inf)
    l_ref[...] = jnp.zeros_like(l_ref)
    acc_ref[...] = jnp.zeros_like(acc_ref)

    def step(j, masked):
        off = pl.multiple_of(j * tq, tq)
        s = _dot(ka_ref[0, 0, pl.ds(off, tq), :], qt)
        if masked:
            key = lax.broadcasted_iota(jnp.int32, (tq, tq), 0)
            qry = lax.broadcasted_iota(jnp.int32, (tq, tq), 1)
            s = jnp.where(key <= qry, s, -jnp.inf)
        m_old = m_ref[...]
        m_new = jnp.maximum(m_old, jnp.max(s, axis=0, keepdims=True))
        p = jnp.exp(s - m_new)
        alpha = jnp.exp(m_old - m_new)
        l_ref[...] = alpha * l_ref[...] + jnp.sum(p, axis=0, keepdims=True)
        acc_ref[...] = alpha * acc_ref[...] + _dot(vt_ref[0, 0, :, pl.ds(off, tq)], p.astype(BF16))
        m_ref[...] = m_new

    step(qi, True)

    def body(j, _):
        step(j, False)
        return 0

    lax.fori_loop(0, qi, body, 0)
    o_ref[0, 0] = (acc_ref[...] / l_ref[...]).T.astype(o_ref.dtype)


def _fox(qt, ka, vt):
    B, H, S, _ = ka.shape
    Dh = vt.shape[2]
    tq = 512
    return pl.pallas_call(
        functools.partial(_fox_kernel, tq=tq),
        grid=(B, H, S // tq),
        in_specs=[pl.BlockSpec((1, 1, LANES, tq), lambda b, h, i: (b, h, 0, i)),
                  pl.BlockSpec((1, 1, S, LANES), lambda b, h, i: (b, h, 0, 0)),
                  pl.BlockSpec((1, 1, Dh, S), lambda b, h, i: (b, h, 0, 0))],
        out_specs=pl.BlockSpec((1, 1, tq, Dh), lambda b, h, i: (b, h, i, 0)),
        out_shape=jax.ShapeDtypeStruct((B, H, S, Dh), BF16),
        scratch_shapes=[pltpu.VMEM((1, tq), F32), pltpu.VMEM((1, tq), F32), pltpu.VMEM((Dh, tq), F32)],
        compiler_params=_cparams(3),
        name="fox",
    )(qt, ka, vt)


def _hgrn_kernel(hq_ref, hf_ref, hi_ref, hg_ref, gam_ref, ng_ref, bd_ref, bl_ref, o_ref, st_ref, os_ref):
    j = pl.program_id(1)
    tb = hq_ref.shape[0]
    C = HGRN_CHUNK
    Dk = HGRN_HEAD_DIM

    @pl.when(j == 0)
    def _():
        st_ref[...] = jnp.zeros_like(st_ref)

    gam = gam_ref[...]
    e = jnp.exp(gam - jnp.max(gam, axis=0, keepdims=True))
    lb = e[0:1, :] / jnp.sum(e, axis=0, keepdims=True)
    f = lb + (1.0 - lb) * jax.nn.sigmoid(hf_ref[...])
    lf = jnp.log(f)
    A = jnp.dot(bd_ref[...], lf, preferred_element_type=F32, precision=HIGHEST)
    AL = jnp.dot(bl_ref[...], lf, preferred_element_type=F32, precision=HIGHEST)
    q = hq_ref[...] * (Dk ** -0.5)
    kk = 1.0 - f
    qa = q * jnp.exp(A)
    ka = kk * jnp.exp(AL - A)
    eal = jnp.exp(AL)
    inp = hi_ref[...]

    tri3 = (lax.broadcasted_iota(jnp.int32, (C, C, Dk), 0) >= lax.broadcasted_iota(jnp.int32, (C, C, Dk), 1))
    for hd in range(HGRN_HEADS):
        lo = hd * Dk
        st = st_ref[hd]
        for c in range(tb // C):
            r0 = c * C
            Ac = A[r0:r0 + C, lo:lo + Dk]
            qc = q[r0:r0 + C, lo:lo + Dk]
            kc = kk[r0:r0 + C, lo:lo + Dk]
            ic = inp[r0:r0 + C, lo:lo + Dk]
            diff = Ac[:, None, :] - Ac[None, :, :]
            dec = jnp.exp(jnp.where(tri3, diff, -jnp.inf))
            sc = jnp.sum(qc[:, None, :] * kc[None, :, :] * dec, axis=-1)
            o = _dot_nt(qa[r0:r0 + C, lo:lo + Dk], st) + _dot(sc, ic)
            st = st * eal[r0:r0 + 1, lo:lo + Dk] + _dot_tn(ic, ka[r0:r0 + C, lo:lo + Dk])
            os_ref[r0:r0 + C, lo:lo + Dk] = o
        st_ref[hd] = st

    o_all = os_ref[...]
    hg = hg_ref[...]
    ng = ng_ref[...]
    for hd in range(HGRN_HEADS):
        lo = hd * Dk
        oh = _rms(o_all[:, lo:lo + Dk], ng[:, lo:lo + Dk])
        g = hg[:, lo:lo + Dk]
        o_ref[:, lo:lo + Dk] = (oh * (g * jax.nn.sigmoid(g))).astype(o_ref.dtype)


def _hgrn(hq, hf, hi, hg, gamma, norm_g, B, S):
    T, W = hq.shape
    tb = 128
    nb = S // tb
    t = jnp.arange(tb)
    same = (t[:, None] // HGRN_CHUNK) == (t[None, :] // HGRN_CHUNK)
    bd = (same & (t[:, None] >= t[None, :])).astype(F32)
    bl = same.astype(F32)
    wspec = pl.BlockSpec((tb, W), lambda b, j: (b * nb + j, 0))
    const = lambda shape: pl.BlockSpec(shape, lambda b, j: (0,) * len(shape))
    return pl.pallas_call(
        _hgrn_kernel,
        grid=(B, nb),
        in_specs=[wspec, wspec, wspec, wspec, const(gamma.shape), const((1, W)), const((tb, tb)), const((tb, tb))],
        out_specs=wspec,
        out_shape=jax.ShapeDtypeStruct((T, W), BF16),
        scratch_shapes=[pltpu.VMEM((HGRN_HEADS, HGRN_HEAD_DIM, HGRN_HEAD_DIM), F32), pltpu.VMEM((tb, W), F32)],
        compiler_params=_cparams(2),
        name="hgrn",
    )(hq, hf, hi, hg, gamma, norm_g, bd, bl)


def _outproj_kernel(x_ref, of_ref, oh_ref, mod_ref, wfo_ref, who_ref, pmg_ref, pfg_ref, wq_ref, keys_ref,
                    x1_ref, h2_ref, sc_ref):
    D = x_ref.shape[1]
    mix = _dot(oh_ref[...], who_ref[...])
    for hd in range(FOX_HEADS):
        mix = mix + _dot(of_ref[0, hd], wfo_ref[hd])
    gate_a = mod_ref[0, :, 2 * D:3 * D]
    shift_f = mod_ref[0, :, 3 * D:4 * D]
    scale_f = mod_ref[0, :, 4 * D:5 * D]
    x1 = x_ref[...] + gate_a * _rms(mix, pmg_ref[...])
    x1_ref[...] = x1
    h2 = _rms(x1, pfg_ref[...]) * (1.0 + scale_f) + shift_f
    h2b = h2.astype(BF16)
    h2_ref[...] = h2b
    qp = _dot(h2b, wq_ref[...])
    for g in range(2 * PEER_HEADS):
        qg = qp[:, g * PEER_HALF:(g + 1) * PEER_HALF]
        qn = qg * lax.rsqrt(jnp.mean(qg * qg, axis=-1, keepdims=True) + EPS)
        sc_ref[g] = _dot_nt(keys_ref[g], qn.astype(BF16))


def _outproj(x2d, ofox, oh, mod3, wfo, who, pmg, pfg, wq, keys, B, S):
    T, D = x2d.shape
    tm = 256
    tpb = S // tm
    G = 2 * PEER_HEADS
    const = lambda shape: pl.BlockSpec(shape, lambda i: (0,) * len(shape))
    row = pl.BlockSpec((tm, D), lambda i: (i, 0))
    return pl.pallas_call(
        _outproj_kernel,
        grid=(T // tm,),
        in_specs=[row,
                  pl.BlockSpec((1, FOX_HEADS, tm, FOX_HEAD_DIM), lambda i: (i // tpb, 0, i % tpb, 0)),
                  pl.BlockSpec((tm, HGRN_WIDTH), lambda i: (i, 0)),
                  pl.BlockSpec((1, 1, N_MOD * D), lambda i: (i // tpb, 0, 0)),
                  const(wfo.shape), const(who.shape), const((1, D)), const((1, D)), const(wq.shape), const(keys.shape)],
        out_specs=[row, row, pl.BlockSpec((G, N_KEYS, tm), lambda i: (0, 0, i))],
        out_shape=[jax.ShapeDtypeStruct((T, D), F32), jax.ShapeDtypeStruct((T, D), BF16),
                   jax.ShapeDtypeStruct((G, N_KEYS, T), F32)],
        compiler_params=_cparams(1),
        name="outproj",
    )(x2d, ofox, oh, mod3, wfo, who, pmg, pfg, wq, keys)


def _topk_rows(s, k, payload=None):
    n = s.shape[0]
    iota = lax.broadcasted_iota(jnp.int32, s.shape, 0)
    vals, picks = [], []
    for _ in range(k):
        m = jnp.max(s, axis=0, keepdims=True)
        ix = jnp.min(jnp.where(s == m, iota, n), axis=0, keepdims=True)
        sel = iota == ix
        vals.append(m)
        if payload is None:
            picks.append(ix)
        else:
            picks.append(jnp.sum(jnp.where(sel, payload, 0), axis=0, keepdims=True))
        s = jnp.where(sel, -jnp.inf, s)
    return jnp.concatenate(vals, axis=0), jnp.concatenate(picks, axis=0)


def _candidates(s1, i1, s2, i2):
    K = PEER_TOPK
    tm = s1.shape[1]
    row = lax.broadcasted_iota(jnp.int32, (SUBLANES, tm), 0)
    vals = [s1[0:1, :] + s2, s1[1:2, :] + s2[0:SUBLANES]]
    idxs = [i1[0:1, :] * N_KEYS + i2, i1[1:2, :] * N_KEYS + i2[0:SUBLANES]]
    for a in range(2, SUBLANES):
        keep = row < K // (a + 1)
        vals.append(jnp.where(keep, s1[a:a + 1, :] + s2[0:SUBLANES], -jnp.inf))
        idxs.append(jnp.where(keep, i1[a:a + 1, :] * N_KEYS + i2[0:SUBLANES], 0))
    vals.append(s1[SUBLANES:K] + s2[0:1, :])
    idxs.append(i1[SUBLANES:K] * N_KEYS + i2[0:1, :])
    return jnp.concatenate(vals, axis=0), jnp.concatenate(idxs, axis=0)


def _topk_kernel(sc_ref, off_ref, par_ref, gate_ref):
    K = PEER_TOPK
    idx_rows, gate_rows = [], []
    for hd in range(PEER_HEADS):
        s1, i1 = _topk_rows(sc_ref[2 * hd], K)
        s2, i2 = _topk_rows(sc_ref[2 * hd + 1], K)
        cand, cidx = _candidates(s1, i1, s2, i2)
        best, eidx = _topk_rows(cand, K, payload=cidx)
        p = jnp.exp(best - best[0:1, :])
        gate_rows.append(p / jnp.sum(p, axis=0, keepdims=True))
        idx_rows.append(eidx)
    idx = jnp.concatenate(idx_rows, axis=0).T
    off_ref[...] = lax.shift_right_logical(idx, 1) * SUBLANES
    par_ref[...] = (idx & 1).astype(F32)
    gate_ref[...] = jnp.concatenate(gate_rows, axis=0).T


def _topk(scores):
    G, NK, T = scores.shape
    tm = 256
    out = pl.BlockSpec((tm, PEER_E), lambda i: (i, 0))
    return pl.pallas_call(
        _topk_kernel,
        grid=(T // tm,),
        in_specs=[pl.BlockSpec((G, NK, tm), lambda i: (0, 0, i))],
        out_specs=[out, out, out],
        out_shape=[jax.ShapeDtypeStruct((T, PEER_E), jnp.int32), jax.ShapeDtypeStruct((T, PEER_E), F32),
                   jax.ShapeDtypeStruct((T, PEER_E), F32)],
        compiler_params=_cparams(1),
        name="topk",
    )(scores)


PEER_TM = 64
PEER_UNROLL = 4
TILE_ROWS = 2 * SUBLANES
STACK = PEER_E * TILE_ROWS


def _pack_table(w):
    n, d = w.shape
    bits = lax.bitcast_convert_type(w.astype(BF16), jnp.uint16).astype(jnp.uint32).reshape(n // 2, 2, d)
    packed = lax.shift_left(bits[:, 1, :], jnp.uint32(16)) | bits[:, 0, :]
    return packed.reshape(n // 2 * (d // LANES), LANES)


def _peer_consts():
    c = jnp.arange(STACK)
    expand = (c[None, :] // TILE_ROWS == jnp.arange(PEER_E)[:, None])
    fold = ((c[None, :] % TILE_ROWS) // 2 == jnp.arange(SUBLANES)[:, None])
    half = (c % 2).reshape(1, STACK)
    return expand.astype(BF16), expand.T.astype(F32), fold.astype(F32), half.astype(F32)


def _stack2(tab_ref, off_ref, t):
    def tile(o):
        return pltpu.bitcast(tab_ref[pl.ds(pl.multiple_of(o, SUBLANES), SUBLANES), :], BF16)
    w0 = jnp.concatenate([tile(off_ref[t, j]) for j in range(PEER_E)], axis=0)
    w1 = jnp.concatenate([tile(off_ref[t + 1, j]) for j in range(PEER_E)], axis=0)
    return jnp.concatenate([w0, w1], axis=1)


def _expand_sel(par_ref, expand_ref, half_ref):
    par = _dot(par_ref[...].astype(BF16), expand_ref[...])
    return jnp.where(par == half_ref[...], 1.0, 0.0)


def _peer_u_kernel(off_ref, par_ref, x_ref, gate_ref, expand_ref, collapse_ref, fold_ref, half_ref, tab_ref,
                   act_ref, sel_ref, g_ref):
    tm = x_ref.shape[0]
    sel_ref[...] = _expand_sel(par_ref, expand_ref, half_ref)
    fold = fold_ref[...]
    zero = jnp.zeros((SUBLANES, LANES), BF16)

    def pair(i, _):
        t = 2 * i
        w2 = _stack2(tab_ref, off_ref, t)
        lhs = jnp.concatenate([jnp.concatenate([x_ref[t], zero], axis=1),
                               jnp.concatenate([zero, x_ref[t + 1]], axis=1)], axis=0)
        g = _dot_nt(lhs, w2)
        g_ref[pl.ds(t, 1), :] = jnp.sum(g[0:SUBLANES] * fold, axis=0, keepdims=True) * sel_ref[pl.ds(t, 1), :]
        g_ref[pl.ds(t + 1, 1), :] = (jnp.sum(g[SUBLANES:] * fold, axis=0, keepdims=True)
                                     * sel_ref[pl.ds(t + 1, 1), :])
        return 0

    lax.fori_loop(0, tm // 2, pair, 0, unroll=PEER_UNROLL)
    a = jnp.dot(g_ref[...], collapse_ref[...], preferred_element_type=F32, precision=HIGHEST)
    gelu = 0.5 * a * (1.0 + lax.erf(a * (2.0 ** -0.5)))
    act_ref[...] = gate_ref[...] * gelu


def _peer_v_kernel(off_ref, par_ref, act_ref, expand_ref, fold_ref, half_ref, tab_ref, y_ref, ce_ref):
    tm = y_ref.shape[0]
    ce_ref[...] = _dot(act_ref[...].astype(BF16), expand_ref[...]) * _expand_sel(par_ref, expand_ref, half_ref)
    fold = fold_ref[...]

    def pair(i, _):
        t = 2 * i
        w2 = _stack2(tab_ref, off_ref, t)
        coef = jnp.concatenate([(ce_ref[pl.ds(t, 1), :] * fold).astype(BF16),
                                (ce_ref[pl.ds(t + 1, 1), :] * fold).astype(BF16)], axis=0)
        out = _dot(coef, w2)
        y_ref[t] = out[0:SUBLANES, 0:LANES]
        y_ref[t + 1] = out[SUBLANES:, LANES:]
        return 0

    lax.fori_loop(0, tm // 2, pair, 0, unroll=PEER_UNROLL)


def _peer_specs(tab):
    tm = PEER_TM
    smem = pl.BlockSpec((tm, PEER_E), lambda i: (i, 0), memory_space=pltpu.SMEM)
    row = pl.BlockSpec((tm, PEER_E), lambda i: (i, 0))
    const = lambda shape: pl.BlockSpec(shape, lambda i: (0,) * len(shape))
    table = pl.BlockSpec(tab.shape, lambda i: (0, 0), pipeline_mode=pl.Buffered(1))
    return tm, smem, row, const, table


def _peer_u(off, par, h2f, gate, tab):
    T = off.shape[0]
    tm, smem, row, const, table = _peer_specs(tab)
    expand, collapse, fold, half = _peer_consts()
    return pl.pallas_call(
        _peer_u_kernel,
        grid=(T // tm,),
        in_specs=[smem, row, pl.BlockSpec((tm, SUBLANES, LANES), lambda i: (i, 0, 0)), row,
                  const(expand.shape), const(collapse.shape), const(fold.shape), const(half.shape), table],
        out_specs=row,
        out_shape=jax.ShapeDtypeStruct((T, PEER_E), F32),
        scratch_shapes=[pltpu.VMEM((tm, STACK), F32), pltpu.VMEM((tm, STACK), F32)],
        compiler_params=_cparams(1),
        name="peer_u",
    )(off, par, h2f, gate, expand, collapse, fold, half, tab)


def _peer_v(off, par, act, tab):
    T = off.shape[0]
    tm, smem, row, const, table = _peer_specs(tab)
    expand, _, fold, half = _peer_consts()
    return pl.pallas_call(
        _peer_v_kernel,
        grid=(T // tm,),
        in_specs=[smem, row, row, const(expand.shape), const(fold.shape), const(half.shape), table],
        out_specs=pl.BlockSpec((tm, SUBLANES, LANES), lambda i: (i, 0, 0)),
        out_shape=jax.ShapeDtypeStruct((T, SUBLANES, LANES), F32),
        scratch_shapes=[pltpu.VMEM((tm, STACK), F32)],
        compiler_params=_cparams(1),
        name="peer_v",
    )(off, par, act, expand, fold, half, tab)


def _final_kernel(x1_ref, y_ref, mod_ref, g_ref, o_ref):
    D = x1_ref.shape[1]
    gate_f = mod_ref[0, :, 5 * D:6 * D]
    o_ref[...] = x1_ref[...] + gate_f * _rms(y_ref[...], g_ref[...])


def _final(x1, y, mod3, g, B, S):
    T, D = x1.shape
    tm = 512
    tpb = S // tm
    row = pl.BlockSpec((tm, D), lambda i: (i, 0))
    return pl.pallas_call(
        _final_kernel,
        grid=(T // tm,),
        in_specs=[row, row, pl.BlockSpec((1, 1, N_MOD * D), lambda i: (i // tpb, 0, 0)),
                  pl.BlockSpec((1, D), lambda i: (0, 0))],
        out_specs=row,
        out_shape=jax.ShapeDtypeStruct((T, D), F32),
        compiler_params=_cparams(1),
        name="final",
    )(x1, y, mod3, g)


def kernel(x, c, w_ada, b_ada, pre_mix_g, post_mix_g, w_in, b_fox_f, hgrn_gamma, hgrn_norm_g, w_out, pre_ffn_g,
           post_ffn_g, peer_w_q, peer_sub_keys, peer_u, peer_v):
    B, S, D = x.shape
    T = B * S
    depth = w_in.shape[0]
    assert depth == 1, "single-layer block"
    l = 0
    x2d = x.reshape(T, D)

    mod3 = _ada(c, w_ada[l], b_ada[l]).reshape(B, 1, N_MOD * D)

    w = w_in[l]
    o1 = 3 * FOX_WIDTH
    wqkv = w[:, :o1].astype(BF16)
    wf = jnp.pad(w[:, o1:o1 + FOX_HEADS], ((0, 0), (0, LANES - FOX_HEADS))).astype(BF16)
    bf = jnp.pad(b_fox_f[l].astype(F32), (0, LANES - FOX_HEADS)).reshape(1, LANES)
    wh = w[:, o1 + FOX_HEADS:].astype(BF16)

    qt, ka, vt, hq, hf, hi, hg = _proj(x2d, mod3, pre_mix_g[l].reshape(1, D), wqkv, wf, bf, wh, B, S)

    o_fox = _fox(qt, ka, vt)

    o_h = _hgrn(hq, hf, hi, hg, hgrn_gamma[l:l + 2].astype(F32), hgrn_norm_g[l].reshape(1, HGRN_WIDTH), B, S)

    wo = w_out[l].astype(BF16)
    wfo = wo[:FOX_WIDTH].reshape(FOX_HEADS, FOX_HEAD_DIM, D)
    who = wo[FOX_WIDTH:]
    keys = peer_sub_keys[l].reshape(2 * PEER_HEADS, N_KEYS, PEER_HALF).astype(BF16)
    x1, h2, scores = _outproj(x2d, o_fox, o_h, mod3, wfo, who, post_mix_g[l].reshape(1, D),
                              pre_ffn_g[l].reshape(1, D), peer_w_q[l].astype(BF16), keys, B, S)

    off, par, gate = _topk(scores)

    act = _peer_u(off, par, h2.reshape(T, SUBLANES, LANES), gate, _pack_table(peer_u[l]))
    y = _peer_v(off, par, act, _pack_table(peer_v[l]))

    out = _final(x1, y.reshape(T, D), mod3, post_ffn_g[l].reshape(1, D), B, S)
    return out.reshape(B, S, D)
```

```python
import functools
import math

import jax
import jax.numpy as jnp
from jax import lax
from jax.experimental import pallas as pl
from jax.experimental.pallas import tpu as pltpu

F32 = jnp.float32
BF16 = jnp.bfloat16
EPS = 1e-6
HIGHEST = lax.Precision.HIGHEST
LOG2E = math.log2(math.e)

FOX_HEADS = 8
FOX_HEAD_DIM = 64
FOX_WIDTH = FOX_HEADS * FOX_HEAD_DIM
HGRN_HEADS = 4
HGRN_HEAD_DIM = 128
HGRN_WIDTH = HGRN_HEADS * HGRN_HEAD_DIM
HGRN_CHUNK = 16
PEER_HEADS = 8
PEER_QUERY_DIM = 256
PEER_HALF = PEER_QUERY_DIM // 2
N_KEYS = 128
PEER_TOPK = 16
PEER_E = PEER_HEADS * PEER_TOPK
N_MOD = 6

LANES = 128
SUBLANES = 8
VMEM_LIMIT = 56 * 1024 * 1024


def _cparams(n_axes, vmem=VMEM_LIMIT):
    return pltpu.CompilerParams(dimension_semantics=("arbitrary",) * n_axes, vmem_limit_bytes=vmem)


def _dot(a, b):
    return jnp.dot(a, b, preferred_element_type=F32)


def _dot_nt(a, b):
    return lax.dot_general(a, b, (((1,), (1,)), ((), ())), preferred_element_type=F32)


def _dot_tn(a, b):
    return lax.dot_general(a, b, (((0,), (0,)), ((), ())), preferred_element_type=F32)


def _rms(x, g):
    return x * lax.rsqrt(jnp.mean(x * x, axis=-1, keepdims=True) + EPS) * g


def _ada_kernel(c_ref, w_ref, b_ref, o_ref):
    c = c_ref[...]
    cond = c * jax.nn.sigmoid(c)
    o_ref[...] = jnp.dot(cond, w_ref[...], preferred_element_type=F32, precision=HIGHEST) + b_ref[...]


def _ada(c, w, b):
    B, D = c.shape
    N = w.shape[1]
    tn = 1024
    return pl.pallas_call(
        _ada_kernel,
        grid=(N // tn,),
        in_specs=[pl.BlockSpec((B, D), lambda j: (0, 0)),
                  pl.BlockSpec((D, tn), lambda j: (0, j)),
                  pl.BlockSpec((1, tn), lambda j: (0, j))],
        out_specs=pl.BlockSpec((B, tn), lambda j: (0, j)),
        out_shape=jax.ShapeDtypeStruct((B, N), F32),
        compiler_params=_cparams(1),
        name="ada",
    )(c, w, b.reshape(1, N))


def _split3(f):
    hi = f.astype(BF16).astype(F32)
    r = f - hi
    mid = r.astype(BF16).astype(F32)
    return hi, mid, r - mid


def _proj_kernel(x_ref, mod_ref, g_ref, wqkv_ref, wf_ref, bf_ref, wh_ref, tri_ref,
                 qt_ref, ka_ref, vt_ref, hq_ref, hf_ref, hi_ref, hg_ref, carry_ref,
                 *, tiles_per_batch):
    i = pl.program_id(0)
    D = x_ref.shape[1]
    tm = x_ref.shape[0]
    Dh = FOX_HEAD_DIM
    x = x_ref[...]
    shift = mod_ref[0, :, 0:D]
    scale = mod_ref[0, :, D:2 * D]
    h = (_rms(x, g_ref[...]) * (1.0 + scale) + shift).astype(BF16)

    ff = _dot(h, wf_ref[...]) + bf_ref[...]
    logf = jnp.minimum(ff, 0.0) - jnp.log(1.0 + jnp.exp(-jnp.abs(ff)))

    @pl.when(i % tiles_per_batch == 0)
    def _():
        carry_ref[...] = jnp.zeros_like(carry_ref)

    cs = jnp.dot(tri_ref[...], logf, preferred_element_type=F32, precision=HIGHEST) + carry_ref[...]
    carry_ref[...] = cs[tm - 1:tm, :]

    qkv = _dot(h, wqkv_ref[...])
    lane = lax.broadcasted_iota(jnp.int32, (tm, Dh), 1)
    zpad = jnp.zeros((tm, Dh), F32)
    for hd in range(FOX_HEADS):
        lo = hd * Dh
        fhi, fmid, flo = _split3(cs[:, hd:hd + 1] * LOG2E)
        pieces = lambda o: jnp.where(lane == o, fhi, jnp.where(lane == o + 1, fmid, jnp.where(lane == o + 2, flo, 0.0)))
        q_aux = jnp.where(lane < 3, -1.0, pieces(3))
        k_aux = jnp.where((lane >= 3) & (lane < 6), 1.0, pieces(0))
        qa = jnp.concatenate([qkv[:, lo:lo + Dh] * (Dh ** -0.5 * LOG2E), q_aux], axis=1)
        ka = jnp.concatenate([qkv[:, FOX_WIDTH + lo:FOX_WIDTH + lo + Dh], k_aux], axis=1)
        va = jnp.concatenate([qkv[:, 2 * FOX_WIDTH + lo:2 * FOX_WIDTH + lo + Dh], zpad], axis=1)
        qt_ref[0, hd] = qa.T.astype(BF16)
        ka_ref[0, hd] = ka.astype(BF16)
        vt_ref[0, hd] = va.T[0:Dh, :].astype(BF16)

    hh = _dot(h, wh_ref[...])
    hq_ref[...] = hh[:, 0:HGRN_WIDTH]
    hf_ref[...] = hh[:, HGRN_WIDTH:2 * HGRN_WIDTH]
    hi_ref[...] = hh[:, 2 * HGRN_WIDTH:3 * HGRN_WIDTH]
    hg_ref[...] = hh[:, 3 * HGRN_WIDTH:4 * HGRN_WIDTH]


def _proj(x2d, mod3, g, wqkv, wf, bf, wh, B, S):
    T, D = x2d.shape
    tm = 256
    tpb = S // tm
    tri = (jnp.arange(tm)[:, None] >= jnp.arange(tm)[None, :]).astype(F32)
    H, Dh = FOX_HEADS, FOX_HEAD_DIM
    fox_shapes = [jax.ShapeDtypeStruct((B, H, LANES, S), BF16), jax.ShapeDtypeStruct((B, H, S, LANES), BF16),
                  jax.ShapeDtypeStruct((B, H, Dh, S), BF16)]
    fox_specs = [pl.BlockSpec((1, H, LANES, tm), lambda i: (i // tpb, 0, 0, i % tpb)),
                 pl.BlockSpec((1, H, tm, LANES), lambda i: (i // tpb, 0, i % tpb, 0)),
                 pl.BlockSpec((1, H, Dh, tm), lambda i: (i // tpb, 0, 0, i % tpb))]
    wide = jax.ShapeDtypeStruct((T, HGRN_WIDTH), F32)
    wspec = pl.BlockSpec((tm, HGRN_WIDTH), lambda i: (i, 0))
    const = lambda shape: pl.BlockSpec(shape, lambda i: (0,) * len(shape))
    return pl.pallas_call(
        functools.partial(_proj_kernel, tiles_per_batch=tpb),
        grid=(T // tm,),
        in_specs=[pl.BlockSpec((tm, D), lambda i: (i, 0)),
                  pl.BlockSpec((1, 1, N_MOD * D), lambda i: (i // tpb, 0, 0)),
                  const((1, D)), const(wqkv.shape), const(wf.shape), const(bf.shape), const(wh.shape),
                  const((tm, tm))],
        out_specs=fox_specs + [wspec, wspec, wspec, wspec],
        out_shape=fox_shapes + [wide, wide, wide, wide],
        scratch_shapes=[pltpu.VMEM((1, LANES), F32)],
        compiler_params=_cparams(1),
        name="proj",
    )(x2d, mod3, g, wqkv, wf, bf, wh, tri)


FOX_HB = 2


def _fox_kernel(qt_ref, ka_ref, vt_ref, o_ref, m_ref, l_ref, acc_ref, *, tq):
    qi = pl.program_id(2)
    m_ref[...] = jnp.full_like(m_ref, -jnp.inf)
    l_ref[...] = jnp.zeros_like(l_ref)
    acc_ref[...] = jnp.zeros_like(acc_ref)

    def step(j, masked):
        off = pl.multiple_of(j * tq, tq)
        hs = range(FOX_HB)
        s = [_dot(ka_ref[0, hh, pl.ds(off, tq), :], qt_ref[0, hh]) for hh in hs]
        if masked:
            key = lax.broadcasted_iota(jnp.int32, (tq, tq), 0)
            qry = lax.broadcasted_iota(jnp.int32, (tq, tq), 1)
            s = [jnp.where(key <= qry, sh, -jnp.inf) for sh in s]
        m_old = [m_ref[hh] for hh in hs]
        m_new = [jnp.maximum(m_old[hh], jnp.max(s[hh], axis=0, keepdims=True)) for hh in hs]
        p = [jnp.exp2(s[hh] - m_new[hh]) for hh in hs]
        alpha = [jnp.exp2(m_old[hh] - m_new[hh]) for hh in hs]
        pv = [_dot(vt_ref[0, hh, :, pl.ds(off, tq)], p[hh].astype(BF16)) for hh in hs]
        for hh in hs:
            l_ref[hh] = alpha[hh] * l_ref[hh] + jnp.sum(p[hh], axis=0, keepdims=True)
            acc_ref[hh] = alpha[hh] * acc_ref[hh] + pv[hh]
            m_ref[hh] = m_new[hh]

    step(qi, True)

    def body(j, _):
        step(j, False)
        return 0

    lax.fori_loop(0, qi, body, 0)
    for hh in range(FOX_HB):
        o_ref[0, hh] = (acc_ref[hh] / l_ref[hh]).T.astype(o_ref.dtype)


def _fox(qt, ka, vt):
    B, H, S, _ = ka.shape
    Dh = vt.shape[2]
    tq = 512
    hb = FOX_HB
    return pl.pallas_call(
        functools.partial(_fox_kernel, tq=tq),
        grid=(B, H // hb, S // tq),
        in_specs=[pl.BlockSpec((1, hb, LANES, tq), lambda b, h, i: (b, h, 0, i)),
                  pl.BlockSpec((1, hb, S, LANES), lambda b, h, i: (b, h, 0, 0)),
                  pl.BlockSpec((1, hb, Dh, S), lambda b, h, i: (b, h, 0, 0))],
        out_specs=pl.BlockSpec((1, hb, tq, Dh), lambda b, h, i: (b, h, i, 0)),
        out_shape=jax.ShapeDtypeStruct((B, H, S, Dh), BF16),
        scratch_shapes=[pltpu.VMEM((hb, 1, tq), F32), pltpu.VMEM((hb, 1, tq), F32), pltpu.VMEM((hb, Dh, tq), F32)],
        compiler_params=_cparams(3),
        name="fox",
    )(qt, ka, vt)


def _hgrn_kernel(hq_ref, hf_ref, hi_ref, hg_ref, gam_ref, ng_ref, bd_ref, bl_ref, o_ref, st_ref, os_ref):
    j = pl.program_id(1)
    tb = hq_ref.shape[0]
    C = HGRN_CHUNK
    Dk = HGRN_HEAD_DIM

    @pl.when(j == 0)
    def _():
        st_ref[...] = jnp.zeros_like(st_ref)

    gam = gam_ref[...]
    e = jnp.exp(gam - jnp.max(gam, axis=0, keepdims=True))
    lb = e[0:1, :] / jnp.sum(e, axis=0, keepdims=True)
    f = lb + (1.0 - lb) * jax.nn.sigmoid(hf_ref[...])
    lf = jnp.log(f)
    A = jnp.dot(bd_ref[...], lf, preferred_element_type=F32, precision=HIGHEST)
    AL = jnp.dot(bl_ref[...], lf, preferred_element_type=F32, precision=HIGHEST)
    q = hq_ref[...] * (Dk ** -0.5)
    kk = 1.0 - f
    qa = q * jnp.exp(A)
    ka = kk * jnp.exp(AL - A)
    eal = jnp.exp(AL)
    inp = hi_ref[...]

    tri3 = (lax.broadcasted_iota(jnp.int32, (C, C, Dk), 0) >= lax.broadcasted_iota(jnp.int32, (C, C, Dk), 1))
    for hd in range(HGRN_HEADS):
        lo = hd * Dk
        st = st_ref[hd]
        for c in range(tb // C):
            r0 = c * C
            Ac = A[r0:r0 + C, lo:lo + Dk]
            qc = q[r0:r0 + C, lo:lo + Dk]
            kc = kk[r0:r0 + C, lo:lo + Dk]
            ic = inp[r0:r0 + C, lo:lo + Dk]
            diff = Ac[:, None, :] - Ac[None, :, :]
            dec = jnp.exp(jnp.where(tri3, diff, -jnp.inf))
            sc = jnp.sum(qc[:, None, :] * kc[None, :, :] * dec, axis=-1)
            o = _dot_nt(qa[r0:r0 + C, lo:lo + Dk], st) + _dot(sc, ic)
            st = st * eal[r0:r0 + 1, lo:lo + Dk] + _dot_tn(ic, ka[r0:r0 + C, lo:lo + Dk])
            os_ref[r0:r0 + C, lo:lo + Dk] = o
        st_ref[hd] = st

    o_all = os_ref[...]
    hg = hg_ref[...]
    ng = ng_ref[...]
    for hd in range(HGRN_HEADS):
        lo = hd * Dk
        oh = _rms(o_all[:, lo:lo + Dk], ng[:, lo:lo + Dk])
        g = hg[:, lo:lo + Dk]
        o_ref[:, lo:lo + Dk] = (oh * (g * jax.nn.sigmoid(g))).astype(o_ref.dtype)


def _hgrn(hq, hf, hi, hg, gamma, norm_g, B, S):
    T, W = hq.shape
    tb = 128
    nb = S // tb
    t = jnp.arange(tb)
    same = (t[:, None] // HGRN_CHUNK) == (t[None, :] // HGRN_CHUNK)
    bd = (same & (t[:, None] >= t[None, :])).astype(F32)
    bl = same.astype(F32)
    wspec = pl.BlockSpec((tb, W), lambda b, j: (b * nb + j, 0))
    const = lambda shape: pl.BlockSpec(shape, lambda b, j: (0,) * len(shape))
    return pl.pallas_call(
        _hgrn_kernel,
        grid=(B, nb),
        in_specs=[wspec, wspec, wspec, wspec, const(gamma.shape), const((1, W)), const((tb, tb)), const((tb, tb))],
        out_specs=wspec,
        out_shape=jax.ShapeDtypeStruct((T, W), BF16),
        scratch_shapes=[pltpu.VMEM((HGRN_HEADS, HGRN_HEAD_DIM, HGRN_HEAD_DIM), F32), pltpu.VMEM((tb, W), F32)],
        compiler_params=_cparams(2),
        name="hgrn",
    )(hq, hf, hi, hg, gamma, norm_g, bd, bl)


def _outproj_kernel(x_ref, of_ref, oh_ref, mod_ref, wfo_ref, who_ref, pmg_ref, pfg_ref, wq_ref, keys_ref,
                    x1_ref, h2_ref, sc_ref):
    D = x_ref.shape[1]
    mix = _dot(oh_ref[...], who_ref[...])
    for hd in range(FOX_HEADS):
        mix = mix + _dot(of_ref[0, hd], wfo_ref[hd])
    gate_a = mod_ref[0, :, 2 * D:3 * D]
    shift_f = mod_ref[0, :, 3 * D:4 * D]
    scale_f = mod_ref[0, :, 4 * D:5 * D]
    x1 = x_ref[...] + gate_a * _rms(mix, pmg_ref[...])
    x1_ref[...] = x1
    h2 = _rms(x1, pfg_ref[...]) * (1.0 + scale_f) + shift_f
    h2_ref[...] = h2
    qp = _dot(h2.astype(BF16), wq_ref[...])
    for g in range(2 * PEER_HEADS):
        qg = qp[:, g * PEER_HALF:(g + 1) * PEER_HALF]
        qn = qg * lax.rsqrt(jnp.mean(qg * qg, axis=-1, keepdims=True) + EPS)
        sc_ref[g] = _dot_nt(keys_ref[g], qn.astype(BF16))


def _outproj(x2d, ofox, oh, mod3, wfo, who, pmg, pfg, wq, keys, B, S):
    T, D = x2d.shape
    tm = 256
    tpb = S // tm
    G = 2 * PEER_HEADS
    const = lambda shape: pl.BlockSpec(shape, lambda i: (0,) * len(shape))
    row = pl.BlockSpec((tm, D), lambda i: (i, 0))
    return pl.pallas_call(
        _outproj_kernel,
        grid=(T // tm,),
        in_specs=[row,
                  pl.BlockSpec((1, FOX_HEADS, tm, FOX_HEAD_DIM), lambda i: (i // tpb, 0, i % tpb, 0)),
                  pl.BlockSpec((tm, HGRN_WIDTH), lambda i: (i, 0)),
                  pl.BlockSpec((1, 1, N_MOD * D), lambda i: (i // tpb, 0, 0)),
                  const(wfo.shape), const(who.shape), const((1, D)), const((1, D)), const(wq.shape), const(keys.shape)],
        out_specs=[row, row, pl.BlockSpec((G, N_KEYS, tm), lambda i: (0, 0, i))],
        out_shape=[jax.ShapeDtypeStruct((T, D), F32), jax.ShapeDtypeStruct((T, D), F32),
                   jax.ShapeDtypeStruct((G, N_KEYS, T), F32)],
        compiler_params=_cparams(1),
        name="outproj",
    )(x2d, ofox, oh, mod3, wfo, who, pmg, pfg, wq, keys)


def _topk_rows(s, k, payload=None):
    n = s.shape[0]
    iota = lax.broadcasted_iota(jnp.int32, s.shape, 0)
    vals, picks = [], []
    for _ in range(k):
        m = jnp.max(s, axis=0, keepdims=True)
        ix = jnp.min(jnp.where(s == m, iota, n), axis=0, keepdims=True)
        sel = iota == ix
        vals.append(m)
        if payload is None:
            picks.append(ix)
        else:
            picks.append(jnp.sum(jnp.where(sel, payload, 0), axis=0, keepdims=True))
        s = jnp.where(sel, -jnp.inf, s)
    return jnp.concatenate(vals, axis=0), jnp.concatenate(picks, axis=0)


def _candidates(s1, i1, s2, i2):
    K = PEER_TOPK
    tm = s1.shape[1]
    row = lax.broadcasted_iota(jnp.int32, (SUBLANES, tm), 0)
    vals = [s1[0:1, :] + s2, s1[1:2, :] + s2[0:SUBLANES]]
    idxs = [i1[0:1, :] * N_KEYS + i2, i1[1:2, :] * N_KEYS + i2[0:SUBLANES]]
    for a in range(2, SUBLANES):
        keep = row < K // (a + 1)
        vals.append(jnp.where(keep, s1[a:a + 1, :] + s2[0:SUBLANES], -jnp.inf))
        idxs.append(jnp.where(keep, i1[a:a + 1, :] * N_KEYS + i2[0:SUBLANES], 0))
    vals.append(s1[SUBLANES:K] + s2[0:1, :])
    idxs.append(i1[SUBLANES:K] * N_KEYS + i2[0:1, :])
    return jnp.concatenate(vals, axis=0), jnp.concatenate(idxs, axis=0)


def _topk_kernel(sc_ref, off_ref, par_ref, gate_ref):
    K = PEER_TOPK
    idx_rows, gate_rows = [], []
    for hd in range(PEER_HEADS):
        s1, i1 = _topk_rows(sc_ref[2 * hd], K)
        s2, i2 = _topk_rows(sc_ref[2 * hd + 1], K)
        cand, cidx = _candidates(s1, i1, s2, i2)
        best, eidx = _topk_rows(cand, K, payload=cidx)
        p = jnp.exp(best - best[0:1, :])
        gate_rows.append(p / jnp.sum(p, axis=0, keepdims=True))
        idx_rows.append(eidx)
    idx = jnp.concatenate(idx_rows, axis=0).T
    n_half = N_KEYS * N_KEYS // 2
    off_ref[...] = (idx & (n_half - 1)) * SUBLANES
    par_ref[...] = (idx >= n_half).astype(F32)
    gate_ref[...] = jnp.concatenate(gate_rows, axis=0).T


def _topk(scores):
    G, NK, T = scores.shape
    tm = 256
    out = pl.BlockSpec((tm, PEER_E), lambda i: (i, 0))
    return pl.pallas_call(
        _topk_kernel,
        grid=(T // tm,),
        in_specs=[pl.BlockSpec((G, NK, tm), lambda i: (0, 0, i))],
        out_specs=[out, out, out],
        out_shape=[jax.ShapeDtypeStruct((T, PEER_E), jnp.int32), jax.ShapeDtypeStruct((T, PEER_E), F32),
                   jax.ShapeDtypeStruct((T, PEER_E), F32)],
        compiler_params=_cparams(1),
        name="topk",
    )(scores)


PEER_TM = 64
PEER_UNROLL = 4
TILE_ROWS = 2 * SUBLANES
STACK = PEER_E * TILE_ROWS


def _pack_kernel(lo_ref, hi_ref, o_ref):
    nb = lo_ref.shape[0]
    lo = lax.bitcast_convert_type(lo_ref[...].astype(BF16).astype(F32), jnp.uint32)
    hi = lax.bitcast_convert_type(hi_ref[...].astype(BF16).astype(F32), jnp.uint32)
    word = (hi & jnp.uint32(0xFFFF0000)) | lax.shift_right_logical(lo, jnp.uint32(16))
    for s in range(SUBLANES):
        o_ref[pl.ds(s, nb, stride=SUBLANES), :] = word[:, s * LANES:(s + 1) * LANES]


def _pack_table(w):
    n, d = w.shape
    nb = 256
    steps = n // 2 // nb
    return pl.pallas_call(
        _pack_kernel,
        grid=(steps,),
        in_specs=[pl.BlockSpec((nb, d), lambda i: (i, 0)), pl.BlockSpec((nb, d), lambda i: (i + steps, 0))],
        out_specs=pl.BlockSpec((nb * SUBLANES, LANES), lambda i: (i, 0)),
        out_shape=jax.ShapeDtypeStruct((n // 2 * SUBLANES, LANES), jnp.uint32),
        compiler_params=_cparams(1),
        name="pack",
    )(w, w)


def _peer_consts():
    c = jnp.arange(STACK)
    expand = (c[None, :] // TILE_ROWS == jnp.arange(PEER_E)[:, None])
    fold = ((c[None, :] % TILE_ROWS) // 2 == jnp.arange(SUBLANES)[:, None])
    half = (c % 2).reshape(1, STACK)
    return expand.astype(BF16), expand.T.astype(F32), fold.astype(F32), half.astype(F32)


def _stack2(tab_ref, off_ref, t):
    def tile(o):
        return pltpu.bitcast(tab_ref[pl.ds(pl.multiple_of(o, SUBLANES), SUBLANES), :], BF16)
    w0 = jnp.concatenate([tile(off_ref[t, j]) for j in range(PEER_E)], axis=0)
    w1 = jnp.concatenate([tile(off_ref[t + 1, j]) for j in range(PEER_E)], axis=0)
    return jnp.concatenate([w0, w1], axis=1)


def _expand_sel(par_ref, expand_ref, half_ref):
    par = _dot(par_ref[...].astype(BF16), expand_ref[...])
    return jnp.where(par == half_ref[...], 1.0, 0.0)


def _fold_rows(dst_ref, x, tm):
    for c in range(SUBLANES):
        dst_ref[pl.ds(c, tm, stride=SUBLANES), :] = x[:, c * LANES:(c + 1) * LANES]


def _peer_u_kernel(off_ref, par_ref, x_ref, gate_ref, expand_ref, collapse_ref, fold_ref, half_ref, tab_ref,
                   act_ref, sel_ref, g_ref, xf_ref):
    tm = x_ref.shape[0]
    sel_ref[...] = _expand_sel(par_ref, expand_ref, half_ref)
    _fold_rows(xf_ref, x_ref[...], tm)
    fold = fold_ref[...]
    zero = jnp.zeros((SUBLANES, LANES), BF16)

    def pair(i, _):
        t = 2 * i
        w2 = _stack2(tab_ref, off_ref, t)
        x0 = xf_ref[pl.ds(pl.multiple_of(t * SUBLANES, SUBLANES), SUBLANES), :].astype(BF16)
        x1 = xf_ref[pl.ds(pl.multiple_of((t + 1) * SUBLANES, SUBLANES), SUBLANES), :].astype(BF16)
        lhs = jnp.concatenate([jnp.concatenate([x0, zero], axis=1),
                               jnp.concatenate([zero, x1], axis=1)], axis=0)
        g = _dot_nt(lhs, w2)
        g_ref[pl.ds(t, 1), :] = jnp.sum(g[0:SUBLANES] * fold, axis=0, keepdims=True) * sel_ref[pl.ds(t, 1), :]
        g_ref[pl.ds(t + 1, 1), :] = (jnp.sum(g[SUBLANES:] * fold, axis=0, keepdims=True)
                                     * sel_ref[pl.ds(t + 1, 1), :])
        return 0

    lax.fori_loop(0, tm // 2, pair, 0, unroll=PEER_UNROLL)
    a = jnp.dot(g_ref[...], collapse_ref[...], preferred_element_type=F32, precision=HIGHEST)
    gelu = 0.5 * a * (1.0 + lax.erf(a * (2.0 ** -0.5)))
    act_ref[...] = gate_ref[...] * gelu


def _peer_v_kernel(off_ref, par_ref, act_ref, expand_ref, fold_ref, half_ref, tab_ref, x1_ref, mod_ref, g_ref,
                   o_ref, ce_ref, yf_ref):
    tm, D = x1_ref.shape
    ce_ref[...] = _dot(act_ref[...].astype(BF16), expand_ref[...]) * _expand_sel(par_ref, expand_ref, half_ref)
    fold = fold_ref[...]

    def pair(i, _):
        t = 2 * i
        w2 = _stack2(tab_ref, off_ref, t)
        coef = jnp.concatenate([(ce_ref[pl.ds(t, 1), :] * fold).astype(BF16),
                                (ce_ref[pl.ds(t + 1, 1), :] * fold).astype(BF16)], axis=0)
        out = _dot(coef, w2)
        yf_ref[pl.ds(pl.multiple_of(t * SUBLANES, SUBLANES), SUBLANES), :] = out[0:SUBLANES, 0:LANES]
        yf_ref[pl.ds(pl.multiple_of((t + 1) * SUBLANES, SUBLANES), SUBLANES), :] = out[SUBLANES:, LANES:]
        return 0

    lax.fori_loop(0, tm // 2, pair, 0, unroll=PEER_UNROLL)
    y = jnp.concatenate([yf_ref[pl.ds(c, tm, stride=SUBLANES), :] for c in range(SUBLANES)], axis=1)
    gate_f = mod_ref[0, :, 5 * D:6 * D]
    o_ref[...] = x1_ref[...] + gate_f * _rms(y, g_ref[...])


def _peer_specs(tab):
    tm = PEER_TM
    smem = pl.BlockSpec((tm, PEER_E), lambda i: (i, 0), memory_space=pltpu.SMEM)
    row = pl.BlockSpec((tm, PEER_E), lambda i: (i, 0))
    const = lambda shape: pl.BlockSpec(shape, lambda i: (0,) * len(shape))
    table = pl.BlockSpec(tab.shape, lambda i: (0, 0), pipeline_mode=pl.Buffered(1))
    return tm, smem, row, const, table


def _peer_u(off, par, h2, gate, tab):
    T, D = h2.shape
    tm, smem, row, const, table = _peer_specs(tab)
    expand, collapse, fold, half = _peer_consts()
    return pl.pallas_call(
        _peer_u_kernel,
        grid=(T // tm,),
        in_specs=[smem, row, pl.BlockSpec((tm, D), lambda i: (i, 0)), row,
                  const(expand.shape), const(collapse.shape), const(fold.shape), const(half.shape), table],
        out_specs=row,
        out_shape=jax.ShapeDtypeStruct((T, PEER_E), F32),
        scratch_shapes=[pltpu.VMEM((tm, STACK), F32), pltpu.VMEM((tm, STACK), F32),
                        pltpu.VMEM((tm * SUBLANES, LANES), F32)],
        compiler_params=_cparams(1),
        name="peer_u",
    )(off, par, h2, gate, expand, collapse, fold, half, tab)


def _peer_v(off, par, act, tab, x1, mod3, g, S):
    T, D = x1.shape
    tm, smem, row, const, table = _peer_specs(tab)
    tpb = S // tm
    expand, _, fold, half = _peer_consts()
    wide = pl.BlockSpec((tm, D), lambda i: (i, 0))
    return pl.pallas_call(
        _peer_v_kernel,
        grid=(T // tm,),
        in_specs=[smem, row, row, const(expand.shape), const(fold.shape), const(half.shape), table,
                  wide, pl.BlockSpec((1, 1, N_MOD * D), lambda i: (i // tpb, 0, 0)), const((1, D))],
        out_specs=wide,
        out_shape=jax.ShapeDtypeStruct((T, D), F32),
        scratch_shapes=[pltpu.VMEM((tm, STACK), F32), pltpu.VMEM((tm * SUBLANES, LANES), F32)],
        compiler_params=_cparams(1),
        name="peer_v",
    )(off, par, act, expand, fold, half, tab, x1, mod3, g)


def kernel(x, c, w_ada, b_ada, pre_mix_g, post_mix_g, w_in, b_fox_f, hgrn_gamma, hgrn_norm_g, w_out, pre_ffn_g,
           post_ffn_g, peer_w_q, peer_sub_keys, peer_u, peer_v):
    B, S, D = x.shape
    T = B * S
    depth = w_in.shape[0]
    assert depth == 1, "single-layer block"
    l = 0
    x2d = x.reshape(T, D)

    mod3 = _ada(c, w_ada[l], b_ada[l]).reshape(B, 1, N_MOD * D)

    w = w_in[l]
    o1 = 3 * FOX_WIDTH
    wqkv = w[:, :o1].astype(BF16)
    wf = jnp.pad(w[:, o1:o1 + FOX_HEADS], ((0, 0), (0, LANES - FOX_HEADS))).astype(BF16)
    bf = jnp.pad(b_fox_f[l].astype(F32), (0, LANES - FOX_HEADS)).reshape(1, LANES)
    wh = w[:, o1 + FOX_HEADS:].astype(BF16)

    qt, ka, vt, hq, hf, hi, hg = _proj(x2d, mod3, pre_mix_g[l].reshape(1, D), wqkv, wf, bf, wh, B, S)

    o_fox = _fox(qt, ka, vt)

    o_h = _hgrn(hq, hf, hi, hg, hgrn_gamma[l:l + 2].astype(F32), hgrn_norm_g[l].reshape(1, HGRN_WIDTH), B, S)

    wo = w_out[l].astype(BF16)
    wfo = wo[:FOX_WIDTH].reshape(FOX_HEADS, FOX_HEAD_DIM, D)
    who = wo[FOX_WIDTH:]
    keys = peer_sub_keys[l].reshape(2 * PEER_HEADS, N_KEYS, PEER_HALF).astype(BF16)
    x1, h2, scores = _outproj(x2d, o_fox, o_h, mod3, wfo, who, post_mix_g[l].reshape(1, D),
                              pre_ffn_g[l].reshape(1, D), peer_w_q[l].astype(BF16), keys, B, S)

    off, par, gate = _topk(scores)

    act = _peer_u(off, par, h2, gate, _pack_table(peer_u[l]))
    out = _peer_v(off, par, act, _pack_table(peer_v[l]), x1, mod3, post_ffn_g[l].reshape(1, D), S)
    return out.reshape(B, S, D)
```

```python
import functools
import math

import jax
import jax.numpy as jnp
from jax import lax
from jax.experimental import pallas as pl
from jax.experimental.pallas import tpu as pltpu

F32 = jnp.float32
BF16 = jnp.bfloat16
EPS = 1e-6
HIGHEST = lax.Precision.HIGHEST
LOG2E = math.log2(math.e)

FOX_HEADS = 8
FOX_HEAD_DIM = 64
FOX_WIDTH = FOX_HEADS * FOX_HEAD_DIM
HGRN_HEADS = 4
HGRN_HEAD_DIM = 128
HGRN_WIDTH = HGRN_HEADS * HGRN_HEAD_DIM
HGRN_CHUNK = 16
PEER_HEADS = 8
PEER_QUERY_DIM = 256
PEER_HALF = PEER_QUERY_DIM // 2
N_KEYS = 128
PEER_TOPK = 16
PEER_E = PEER_HEADS * PEER_TOPK
PEER_GROUP = 8
N_MOD = 6

LANES = 128
SUBLANES = 8
VMEM_LIMIT = 56 * 1024 * 1024


def _cparams(n_axes, vmem=VMEM_LIMIT):
    return pltpu.CompilerParams(dimension_semantics=("arbitrary",) * n_axes, vmem_limit_bytes=vmem)


def _dot(a, b):
    return jnp.dot(a, b, preferred_element_type=F32)


def _dot_nt(a, b):
    return lax.dot_general(a, b, (((1,), (1,)), ((), ())), preferred_element_type=F32)


def _dot_tn(a, b):
    return lax.dot_general(a, b, (((0,), (0,)), ((), ())), preferred_element_type=F32)


def _rms(x, g):
    return x * lax.rsqrt(jnp.mean(x * x, axis=-1, keepdims=True) + EPS) * g


def _ada_kernel(c_ref, w_ref, b_ref, o_ref):
    c = c_ref[...]
    cond = c * jax.nn.sigmoid(c)
    o_ref[...] = jnp.dot(cond, w_ref[...], preferred_element_type=F32, precision=HIGHEST) + b_ref[...]


def _ada(c, w, b):
    B, D = c.shape
    N = w.shape[1]
    tn = 1024
    return pl.pallas_call(
        _ada_kernel,
        grid=(N // tn,),
        in_specs=[pl.BlockSpec((B, D), lambda j: (0, 0)),
                  pl.BlockSpec((D, tn), lambda j: (0, j)),
                  pl.BlockSpec((1, tn), lambda j: (0, j))],
        out_specs=pl.BlockSpec((B, tn), lambda j: (0, j)),
        out_shape=jax.ShapeDtypeStruct((B, N), F32),
        compiler_params=_cparams(1),
        name="ada",
    )(c, w, b.reshape(1, N))


def _split3(f):
    hi = f.astype(BF16).astype(F32)
    r = f - hi
    mid = r.astype(BF16).astype(F32)
    return hi, mid, r - mid


def _proj_kernel(x_ref, mod_ref, g_ref, wqkv_ref, wf_ref, bf_ref, wh_ref, tri_ref,
                 qt_ref, ka_ref, vt_ref, hq_ref, hf_ref, hi_ref, hg_ref, carry_ref,
                 *, tiles_per_batch):
    i = pl.program_id(0)
    D = x_ref.shape[1]
    tm = x_ref.shape[0]
    Dh = FOX_HEAD_DIM
    x = x_ref[...]
    shift = mod_ref[0, :, 0:D]
    scale = mod_ref[0, :, D:2 * D]
    h = (_rms(x, g_ref[...]) * (1.0 + scale) + shift).astype(BF16)

    ff = _dot(h, wf_ref[...]) + bf_ref[...]
    logf = jnp.minimum(ff, 0.0) - jnp.log(1.0 + jnp.exp(-jnp.abs(ff)))

    @pl.when(i % tiles_per_batch == 0)
    def _():
        carry_ref[...] = jnp.zeros_like(carry_ref)

    cs = jnp.dot(tri_ref[...], logf, preferred_element_type=F32, precision=HIGHEST) + carry_ref[...]
    carry_ref[...] = cs[tm - 1:tm, :]

    qkv = _dot(h, wqkv_ref[...])
    lane = lax.broadcasted_iota(jnp.int32, (tm, Dh), 1)
    zpad = jnp.zeros((tm, Dh), F32)
    for hd in range(FOX_HEADS):
        lo = hd * Dh
        fhi, fmid, flo = _split3(cs[:, hd:hd + 1] * LOG2E)
        pieces = lambda o: jnp.where(lane == o, fhi, jnp.where(lane == o + 1, fmid, jnp.where(lane == o + 2, flo, 0.0)))
        q_aux = jnp.where(lane < 3, -1.0, pieces(3))
        k_aux = jnp.where((lane >= 3) & (lane < 6), 1.0, pieces(0))
        qa = jnp.concatenate([qkv[:, lo:lo + Dh] * (Dh ** -0.5 * LOG2E), q_aux], axis=1)
        ka = jnp.concatenate([qkv[:, FOX_WIDTH + lo:FOX_WIDTH + lo + Dh], k_aux], axis=1)
        va = jnp.concatenate([qkv[:, 2 * FOX_WIDTH + lo:2 * FOX_WIDTH + lo + Dh], zpad], axis=1)
        qt_ref[0, hd] = qa.T.astype(BF16)
        ka_ref[0, hd] = ka.astype(BF16)
        vt_ref[0, hd] = va.T[0:Dh, :].astype(BF16)

    hh = _dot(h, wh_ref[...])
    hq_ref[...] = hh[:, 0:HGRN_WIDTH]
    hf_ref[...] = hh[:, HGRN_WIDTH:2 * HGRN_WIDTH]
    hi_ref[...] = hh[:, 2 * HGRN_WIDTH:3 * HGRN_WIDTH]
    hg_ref[...] = hh[:, 3 * HGRN_WIDTH:4 * HGRN_WIDTH]


def _proj(x2d, mod3, g, wqkv, wf, bf, wh, B, S):
    T, D = x2d.shape
    tm = 256
    tpb = S // tm
    tri = (jnp.arange(tm)[:, None] >= jnp.arange(tm)[None, :]).astype(F32)
    H, Dh = FOX_HEADS, FOX_HEAD_DIM
    fox_shapes = [jax.ShapeDtypeStruct((B, H, LANES, S), BF16), jax.ShapeDtypeStruct((B, H, S, LANES), BF16),
                  jax.ShapeDtypeStruct((B, H, Dh, S), BF16)]
    fox_specs = [pl.BlockSpec((1, H, LANES, tm), lambda i: (i // tpb, 0, 0, i % tpb)),
                 pl.BlockSpec((1, H, tm, LANES), lambda i: (i // tpb, 0, i % tpb, 0)),
                 pl.BlockSpec((1, H, Dh, tm), lambda i: (i // tpb, 0, 0, i % tpb))]
    wide = jax.ShapeDtypeStruct((T, HGRN_WIDTH), F32)
    wspec = pl.BlockSpec((tm, HGRN_WIDTH), lambda i: (i, 0))
    const = lambda shape: pl.BlockSpec(shape, lambda i: (0,) * len(shape))
    return pl.pallas_call(
        functools.partial(_proj_kernel, tiles_per_batch=tpb),
        grid=(T // tm,),
        in_specs=[pl.BlockSpec((tm, D), lambda i: (i, 0)),
                  pl.BlockSpec((1, 1, N_MOD * D), lambda i: (i // tpb, 0, 0)),
                  const((1, D)), const(wqkv.shape), const(wf.shape), const(bf.shape), const(wh.shape),
                  const((tm, tm))],
        out_specs=fox_specs + [wspec, wspec, wspec, wspec],
        out_shape=fox_shapes + [wide, wide, wide, wide],
        scratch_shapes=[pltpu.VMEM((1, LANES), F32)],
        compiler_params=_cparams(1),
        name="proj",
    )(x2d, mod3, g, wqkv, wf, bf, wh, tri)


FOX_HB = 2


def _fox_kernel(qt_ref, ka_ref, vt_ref, o_ref, m_ref, l_ref, acc_ref, *, tq):
    qi = pl.program_id(2)
    m_ref[...] = jnp.full_like(m_ref, -jnp.inf)
    l_ref[...] = jnp.zeros_like(l_ref)
    acc_ref[...] = jnp.zeros_like(acc_ref)

    def step(j, masked):
        off = pl.multiple_of(j * tq, tq)
        hs = range(FOX_HB)
        s = [_dot(ka_ref[0, hh, pl.ds(off, tq), :], qt_ref[0, hh]) for hh in hs]
        if masked:
            key = lax.broadcasted_iota(jnp.int32, (tq, tq), 0)
            qry = lax.broadcasted_iota(jnp.int32, (tq, tq), 1)
            s = [jnp.where(key <= qry, sh, -jnp.inf) for sh in s]
        m_old = [m_ref[hh] for hh in hs]
        m_new = [jnp.maximum(m_old[hh], jnp.max(s[hh], axis=0, keepdims=True)) for hh in hs]
        p = [jnp.exp2(s[hh] - m_new[hh]) for hh in hs]
        alpha = [jnp.exp2(m_old[hh] - m_new[hh]) for hh in hs]
        pv = [_dot(vt_ref[0, hh, :, pl.ds(off, tq)], p[hh].astype(BF16)) for hh in hs]
        for hh in hs:
            l_ref[hh] = alpha[hh] * l_ref[hh] + jnp.sum(p[hh], axis=0, keepdims=True)
            acc_ref[hh] = alpha[hh] * acc_ref[hh] + pv[hh]
            m_ref[hh] = m_new[hh]

    step(qi, True)

    def body(j, _):
        step(j, False)
        return 0

    lax.fori_loop(0, qi, body, 0)
    for hh in range(FOX_HB):
        o_ref[0, hh] = (acc_ref[hh] / l_ref[hh]).T.astype(o_ref.dtype)


def _fox(qt, ka, vt):
    B, H, S, _ = ka.shape
    Dh = vt.shape[2]
    tq = 512
    hb = FOX_HB
    return pl.pallas_call(
        functools.partial(_fox_kernel, tq=tq),
        grid=(B, H // hb, S // tq),
        in_specs=[pl.BlockSpec((1, hb, LANES, tq), lambda b, h, i: (b, h, 0, i)),
                  pl.BlockSpec((1, hb, S, LANES), lambda b, h, i: (b, h, 0, 0)),
                  pl.BlockSpec((1, hb, Dh, S), lambda b, h, i: (b, h, 0, 0))],
        out_specs=pl.BlockSpec((1, hb, tq, Dh), lambda b, h, i: (b, h, i, 0)),
        out_shape=jax.ShapeDtypeStruct((B, H, S, Dh), BF16),
        scratch_shapes=[pltpu.VMEM((hb, 1, tq), F32), pltpu.VMEM((hb, 1, tq), F32), pltpu.VMEM((hb, Dh, tq), F32)],
        compiler_params=_cparams(3),
        name="fox",
    )(qt, ka, vt)


def _hgrn_kernel(hq_ref, hf_ref, hi_ref, hg_ref, gam_ref, ng_ref, bd_ref, bl_ref, o_ref, st_ref, os_ref):
    j = pl.program_id(1)
    tb = hq_ref.shape[0]
    C = HGRN_CHUNK
    Dk = HGRN_HEAD_DIM

    @pl.when(j == 0)
    def _():
        st_ref[...] = jnp.zeros_like(st_ref)

    gam = gam_ref[...]
    e = jnp.exp(gam - jnp.max(gam, axis=0, keepdims=True))
    lb = e[0:1, :] / jnp.sum(e, axis=0, keepdims=True)
    f = lb + (1.0 - lb) * jax.nn.sigmoid(hf_ref[...])
    lf = jnp.log(f)
    A = jnp.dot(bd_ref[...], lf, preferred_element_type=F32, precision=HIGHEST)
    AL = jnp.dot(bl_ref[...], lf, preferred_element_type=F32, precision=HIGHEST)
    q = hq_ref[...] * (Dk ** -0.5)
    kk = 1.0 - f
    qa = q * jnp.exp(A)
    ka = kk * jnp.exp(AL - A)
    eal = jnp.exp(AL)
    inp = hi_ref[...]

    tri3 = (lax.broadcasted_iota(jnp.int32, (C, C, Dk), 0) >= lax.broadcasted_iota(jnp.int32, (C, C, Dk), 1))
    for hd in range(HGRN_HEADS):
        lo = hd * Dk
        st = st_ref[hd]
        for c in range(tb // C):
            r0 = c * C
            Ac = A[r0:r0 + C, lo:lo + Dk]
            qc = q[r0:r0 + C, lo:lo + Dk]
            kc = kk[r0:r0 + C, lo:lo + Dk]
            ic = inp[r0:r0 + C, lo:lo + Dk]
            diff = Ac[:, None, :] - Ac[None, :, :]
            dec = jnp.exp(jnp.where(tri3, diff, -jnp.inf))
            sc = jnp.sum(qc[:, None, :] * kc[None, :, :] * dec, axis=-1)
            o = _dot_nt(qa[r0:r0 + C, lo:lo + Dk], st) + _dot(sc, ic)
            st = st * eal[r0:r0 + 1, lo:lo + Dk] + _dot_tn(ic, ka[r0:r0 + C, lo:lo + Dk])
            os_ref[r0:r0 + C, lo:lo + Dk] = o
        st_ref[hd] = st

    o_all = os_ref[...]
    hg = hg_ref[...]
    ng = ng_ref[...]
    for hd in range(HGRN_HEADS):
        lo = hd * Dk
        oh = _rms(o_all[:, lo:lo + Dk], ng[:, lo:lo + Dk])
        g = hg[:, lo:lo + Dk]
        o_ref[:, lo:lo + Dk] = (oh * (g * jax.nn.sigmoid(g))).astype(o_ref.dtype)


def _hgrn(hq, hf, hi, hg, gamma, norm_g, B, S):
    T, W = hq.shape
    tb = 128
    nb = S // tb
    t = jnp.arange(tb)
    same = (t[:, None] // HGRN_CHUNK) == (t[None, :] // HGRN_CHUNK)
    bd = (same & (t[:, None] >= t[None, :])).astype(F32)
    bl = same.astype(F32)
    wspec = pl.BlockSpec((tb, W), lambda b, j: (b * nb + j, 0))
    const = lambda shape: pl.BlockSpec(shape, lambda b, j: (0,) * len(shape))
    return pl.pallas_call(
        _hgrn_kernel,
        grid=(B, nb),
        in_specs=[wspec, wspec, wspec, wspec, const(gamma.shape), const((1, W)), const((tb, tb)), const((tb, tb))],
        out_specs=wspec,
        out_shape=jax.ShapeDtypeStruct((T, W), BF16),
        scratch_shapes=[pltpu.VMEM((HGRN_HEADS, HGRN_HEAD_DIM, HGRN_HEAD_DIM), F32), pltpu.VMEM((tb, W), F32)],
        compiler_params=_cparams(2),
        name="hgrn",
    )(hq, hf, hi, hg, gamma, norm_g, bd, bl)


def _outproj_kernel(x_ref, of_ref, oh_ref, mod_ref, wfo_ref, who_ref, pmg_ref, pfg_ref, wq_ref, keys_ref,
                    x1_ref, h2_ref, sc_ref):
    D = x_ref.shape[1]
    mix = _dot(oh_ref[...], who_ref[...])
    for hd in range(FOX_HEADS):
        mix = mix + _dot(of_ref[0, hd], wfo_ref[hd])
    gate_a = mod_ref[0, :, 2 * D:3 * D]
    shift_f = mod_ref[0, :, 3 * D:4 * D]
    scale_f = mod_ref[0, :, 4 * D:5 * D]
    x1 = x_ref[...] + gate_a * _rms(mix, pmg_ref[...])
    x1_ref[...] = x1
    h2 = _rms(x1, pfg_ref[...]) * (1.0 + scale_f) + shift_f
    h2_ref[...] = h2
    qp = _dot(h2.astype(BF16), wq_ref[...])
    for g in range(2 * PEER_HEADS):
        qg = qp[:, g * PEER_HALF:(g + 1) * PEER_HALF]
        qn = qg * lax.rsqrt(jnp.mean(qg * qg, axis=-1, keepdims=True) + EPS)
        sc_ref[g] = _dot_nt(keys_ref[g], qn.astype(BF16))


def _outproj(x2d, ofox, oh, mod3, wfo, who, pmg, pfg, wq, keys, B, S):
    T, D = x2d.shape
    tm = 256
    tpb = S // tm
    G = 2 * PEER_HEADS
    const = lambda shape: pl.BlockSpec(shape, lambda i: (0,) * len(shape))
    row = pl.BlockSpec((tm, D), lambda i: (i, 0))
    return pl.pallas_call(
        _outproj_kernel,
        grid=(T // tm,),
        in_specs=[row,
                  pl.BlockSpec((1, FOX_HEADS, tm, FOX_HEAD_DIM), lambda i: (i // tpb, 0, i % tpb, 0)),
                  pl.BlockSpec((tm, HGRN_WIDTH), lambda i: (i, 0)),
                  pl.BlockSpec((1, 1, N_MOD * D), lambda i: (i // tpb, 0, 0)),
                  const(wfo.shape), const(who.shape), const((1, D)), const((1, D)), const(wq.shape), const(keys.shape)],
        out_specs=[row, row, pl.BlockSpec((G, N_KEYS, tm), lambda i: (0, 0, i))],
        out_shape=[jax.ShapeDtypeStruct((T, D), F32), jax.ShapeDtypeStruct((T, D), F32),
                   jax.ShapeDtypeStruct((G, N_KEYS, T), F32)],
        compiler_params=_cparams(1),
        name="outproj",
    )(x2d, ofox, oh, mod3, wfo, who, pmg, pfg, wq, keys)


def _topk_rows(s, k, payload=None):
    n = s.shape[0]
    iota = lax.broadcasted_iota(jnp.int32, s.shape, 0)
    vals, picks = [], []
    for _ in range(k):
        m = jnp.max(s, axis=0, keepdims=True)
        ix = jnp.min(jnp.where(s == m, iota, n), axis=0, keepdims=True)
        sel = iota == ix
        vals.append(m)
        if payload is None:
            picks.append(ix)
        else:
            picks.append(jnp.sum(jnp.where(sel, payload, 0), axis=0, keepdims=True))
        s = jnp.where(sel, -jnp.inf, s)
    return jnp.concatenate(vals, axis=0), jnp.concatenate(picks, axis=0)


def _candidates(s1, i1, s2, i2):
    K = PEER_TOPK
    tm = s1.shape[1]
    row = lax.broadcasted_iota(jnp.int32, (SUBLANES, tm), 0)
    vals = [s1[0:1, :] + s2, s1[1:2, :] + s2[0:SUBLANES]]
    idxs = [i1[0:1, :] * N_KEYS + i2, i1[1:2, :] * N_KEYS + i2[0:SUBLANES]]
    for a in range(2, SUBLANES):
        keep = row < K // (a + 1)
        vals.append(jnp.where(keep, s1[a:a + 1, :] + s2[0:SUBLANES], -jnp.inf))
        idxs.append(jnp.where(keep, i1[a:a + 1, :] * N_KEYS + i2[0:SUBLANES], 0))
    vals.append(s1[SUBLANES:K] + s2[0:1, :])
    idxs.append(i1[SUBLANES:K] * N_KEYS + i2[0:1, :])
    return jnp.concatenate(vals, axis=0), jnp.concatenate(idxs, axis=0)


def _topk_kernel(sc_ref, off_ref, par_ref, gate_ref, offs_ref):
    K = PEER_TOPK
    idx_rows, gate_rows = [], []
    for hd in range(PEER_HEADS):
        s1, i1 = _topk_rows(sc_ref[2 * hd], K)
        s2, i2 = _topk_rows(sc_ref[2 * hd + 1], K)
        cand, cidx = _candidates(s1, i1, s2, i2)
        best, eidx = _topk_rows(cand, K, payload=cidx)
        p = jnp.exp(best - best[0:1, :])
        gate_rows.append(p / jnp.sum(p, axis=0, keepdims=True))
        idx_rows.append(eidx)
    idx = jnp.concatenate(idx_rows, axis=0).T
    n_half = N_KEYS * N_KEYS // 2
    offs_ref[...] = (idx & (n_half - 1)) * SUBLANES
    tm = offs_ref.shape[0]
    for k in range(PEER_GROUP):
        off_ref[k] = offs_ref[pl.ds(k, tm // PEER_GROUP, stride=PEER_GROUP), :]
    par_ref[...] = (idx >= n_half).astype(F32)
    gate_ref[...] = jnp.concatenate(gate_rows, axis=0).T


def _topk(scores):
    G, NK, T = scores.shape
    tm = 256
    out = pl.BlockSpec((tm, PEER_E), lambda i: (i, 0))
    return pl.pallas_call(
        _topk_kernel,
        grid=(T // tm,),
        in_specs=[pl.BlockSpec((G, NK, tm), lambda i: (0, 0, i))],
        out_specs=[pl.BlockSpec((PEER_GROUP, tm // PEER_GROUP, PEER_E), lambda i: (0, i, 0)), out, out],
        out_shape=[jax.ShapeDtypeStruct((PEER_GROUP, T // PEER_GROUP, PEER_E), jnp.int32),
                   jax.ShapeDtypeStruct((T, PEER_E), F32), jax.ShapeDtypeStruct((T, PEER_E), F32)],
        scratch_shapes=[pltpu.VMEM((tm, PEER_E), jnp.int32)],
        compiler_params=_cparams(1),
        name="topk",
    )(scores)


PEER_TM = 64
TILE_ROWS = 2 * SUBLANES
STACK = PEER_E * TILE_ROWS


def _pack_kernel(lo_ref, hi_ref, o_ref):
    nb = lo_ref.shape[0]
    lo = lax.bitcast_convert_type(lo_ref[...].astype(BF16).astype(F32), jnp.uint32)
    hi = lax.bitcast_convert_type(hi_ref[...].astype(BF16).astype(F32), jnp.uint32)
    word = (hi & jnp.uint32(0xFFFF0000)) | lax.shift_right_logical(lo, jnp.uint32(16))
    for s in range(SUBLANES):
        o_ref[pl.ds(s, nb, stride=SUBLANES), :] = word[:, s * LANES:(s + 1) * LANES]


def _pack_table(w):
    n, d = w.shape
    nb = 256
    steps = n // 2 // nb
    return pl.pallas_call(
        _pack_kernel,
        grid=(steps,),
        in_specs=[pl.BlockSpec((nb, d), lambda i: (i, 0)), pl.BlockSpec((nb, d), lambda i: (i + steps, 0))],
        out_specs=pl.BlockSpec((nb * SUBLANES, LANES), lambda i: (i, 0)),
        out_shape=jax.ShapeDtypeStruct((n // 2 * SUBLANES, LANES), jnp.uint32),
        compiler_params=_cparams(1),
        name="pack",
    )(w, w)


def _peer_consts():
    c = jnp.arange(STACK)
    expand = (c[None, :] // TILE_ROWS == jnp.arange(PEER_E)[:, None])
    fold = ((c[None, :] % TILE_ROWS) // 2 == jnp.arange(SUBLANES)[:, None])
    half = (c % 2).reshape(1, STACK)
    return expand.astype(BF16), expand.T.astype(BF16), fold.astype(F32), half.astype(F32)


def _stack2(tab_ref, off_a, off_b, g):
    def tile(o):
        return pltpu.bitcast(tab_ref[pl.ds(pl.multiple_of(o, SUBLANES), SUBLANES), :], BF16)
    w0 = jnp.concatenate([tile(off_a[g, j]) for j in range(PEER_E)], axis=0)
    w1 = jnp.concatenate([tile(off_b[g, j]) for j in range(PEER_E)], axis=0)
    return jnp.concatenate([w0, w1], axis=1)


def _split2_dot(x, w):
    hi = x.astype(BF16)
    lo = (x - hi.astype(F32)).astype(BF16)
    return _dot(hi, w) + _dot(lo, w)


def _expand_sel(par_ref, expand_ref, half_ref):
    par = _dot(par_ref[...].astype(BF16), expand_ref[...])
    return jnp.where(par == half_ref[...], 1.0, 0.0)


def _fold_rows(dst_ref, x, tm):
    for c in range(SUBLANES):
        dst_ref[pl.ds(c, tm, stride=SUBLANES), :] = x[:, c * LANES:(c + 1) * LANES]


def _peer_u_kernel(*refs):
    offs = refs[:PEER_GROUP]
    (par_ref, x_ref, gate_ref, expand_ref, collapse_ref, fold_ref, half_ref, tab_ref,
     act_ref, sel_ref, g_ref, xf_ref) = refs[PEER_GROUP:]
    tm = x_ref.shape[0]
    sel_ref[...] = _expand_sel(par_ref, expand_ref, half_ref)
    _fold_rows(xf_ref, x_ref[...], tm)
    fold = fold_ref[...]
    zero = jnp.zeros((SUBLANES, LANES), BF16)

    def pair(g, p):
        t = PEER_GROUP * g + 2 * p
        w2 = _stack2(tab_ref, offs[2 * p], offs[2 * p + 1], g)
        x0 = xf_ref[pl.ds(pl.multiple_of(t * SUBLANES, SUBLANES), SUBLANES), :].astype(BF16)
        x1 = xf_ref[pl.ds(pl.multiple_of((t + 1) * SUBLANES, SUBLANES), SUBLANES), :].astype(BF16)
        lhs = jnp.concatenate([jnp.concatenate([x0, zero], axis=1),
                               jnp.concatenate([zero, x1], axis=1)], axis=0)
        gm = _dot_nt(lhs, w2)
        g_ref[pl.ds(t, 1), :] = jnp.sum(gm[0:SUBLANES] * fold, axis=0, keepdims=True) * sel_ref[pl.ds(t, 1), :]
        g_ref[pl.ds(t + 1, 1), :] = (jnp.sum(gm[SUBLANES:] * fold, axis=0, keepdims=True)
                                     * sel_ref[pl.ds(t + 1, 1), :])

    def group(g, _):
        for p in range(PEER_GROUP // 2):
            pair(g, p)
        return 0

    lax.fori_loop(0, tm // PEER_GROUP, group, 0)
    a = _split2_dot(g_ref[...], collapse_ref[...])
    gelu = 0.5 * a * (1.0 + lax.erf(a * (2.0 ** -0.5)))
    act_ref[...] = gate_ref[...] * gelu


def _peer_v_kernel(*refs):
    offs = refs[:PEER_GROUP]
    (par_ref, act_ref, expand_ref, fold_ref, half_ref, tab_ref, x1_ref, mod_ref, g_ref,
     o_ref, ce_ref, yf_ref) = refs[PEER_GROUP:]
    tm, D = x1_ref.shape
    ce_ref[...] = _dot(act_ref[...].astype(BF16), expand_ref[...]) * _expand_sel(par_ref, expand_ref, half_ref)
    fold = fold_ref[...]

    def pair(g, p):
        t = PEER_GROUP * g + 2 * p
        w2 = _stack2(tab_ref, offs[2 * p], offs[2 * p + 1], g)
        coef = jnp.concatenate([(ce_ref[pl.ds(t, 1), :] * fold).astype(BF16),
                                (ce_ref[pl.ds(t + 1, 1), :] * fold).astype(BF16)], axis=0)
        out = _dot(coef, w2)
        yf_ref[pl.ds(pl.multiple_of(t * SUBLANES, SUBLANES), SUBLANES), :] = out[0:SUBLANES, 0:LANES]
        yf_ref[pl.ds(pl.multiple_of((t + 1) * SUBLANES, SUBLANES), SUBLANES), :] = out[SUBLANES:, LANES:]

    def group(g, _):
        for p in range(PEER_GROUP // 2):
            pair(g, p)
        return 0

    lax.fori_loop(0, tm // PEER_GROUP, group, 0)
    y = jnp.concatenate([yf_ref[pl.ds(c, tm, stride=SUBLANES), :] for c in range(SUBLANES)], axis=1)
    gate_f = mod_ref[0, :, 5 * D:6 * D]
    o_ref[...] = x1_ref[...] + gate_f * _rms(y, g_ref[...])


def _peer_specs(tab):
    tm = PEER_TM
    smem = [pl.BlockSpec((None, tm // PEER_GROUP, PEER_E), functools.partial(lambda k, i: (k, i, 0), k),
                         memory_space=pltpu.SMEM) for k in range(PEER_GROUP)]
    row = pl.BlockSpec((tm, PEER_E), lambda i: (i, 0))
    const = lambda shape: pl.BlockSpec(shape, lambda i: (0,) * len(shape))
    table = pl.BlockSpec(tab.shape, lambda i: (0, 0), pipeline_mode=pl.Buffered(1))
    return tm, smem, row, const, table


def _peer_u(off, par, h2, gate, tab):
    T, D = h2.shape
    tm, smem, row, const, table = _peer_specs(tab)
    expand, collapse, fold, half = _peer_consts()
    return pl.pallas_call(
        _peer_u_kernel,
        grid=(T // tm,),
        in_specs=smem + [row, pl.BlockSpec((tm, D), lambda i: (i, 0)), row,
                  const(expand.shape), const(collapse.shape), const(fold.shape), const(half.shape), table],
        out_specs=row,
        out_shape=jax.ShapeDtypeStruct((T, PEER_E), F32),
        scratch_shapes=[pltpu.VMEM((tm, STACK), F32), pltpu.VMEM((tm, STACK), F32),
                        pltpu.VMEM((tm * SUBLANES, LANES), F32)],
        compiler_params=_cparams(1),
        name="peer_u",
    )(*([off] * PEER_GROUP), par, h2, gate, expand, collapse, fold, half, tab)


def _peer_v(off, par, act, tab, x1, mod3, g, S):
    T, D = x1.shape
    tm, smem, row, const, table = _peer_specs(tab)
    tpb = S // tm
    expand, _, fold, half = _peer_consts()
    wide = pl.BlockSpec((tm, D), lambda i: (i, 0))
    return pl.pallas_call(
        _peer_v_kernel,
        grid=(T // tm,),
        in_specs=smem + [row, row, const(expand.shape), const(fold.shape), const(half.shape), table,
                  wide, pl.BlockSpec((1, 1, N_MOD * D), lambda i: (i // tpb, 0, 0)), const((1, D))],
        out_specs=wide,
        out_shape=jax.ShapeDtypeStruct((T, D), F32),
        scratch_shapes=[pltpu.VMEM((tm, STACK), F32), pltpu.VMEM((tm * SUBLANES, LANES), F32)],
        compiler_params=_cparams(1),
        name="peer_v",
    )(*([off] * PEER_GROUP), par, act, expand, fold, half, tab, x1, mod3, g)


def kernel(x, c, w_ada, b_ada, pre_mix_g, post_mix_g, w_in, b_fox_f, hgrn_gamma, hgrn_norm_g, w_out, pre_ffn_g,
           post_ffn_g, peer_w_q, peer_sub_keys, peer_u, peer_v):
    B, S, D = x.shape
    T = B * S
    depth = w_in.shape[0]
    assert depth == 1, "single-layer block"
    l = 0
    x2d = x.reshape(T, D)

    mod3 = _ada(c, w_ada[l], b_ada[l]).reshape(B, 1, N_MOD * D)

    w = w_in[l]
    o1 = 3 * FOX_WIDTH
    wqkv = w[:, :o1].astype(BF16)
    wf = jnp.pad(w[:, o1:o1 + FOX_HEADS], ((0, 0), (0, LANES - FOX_HEADS))).astype(BF16)
    bf = jnp.pad(b_fox_f[l].astype(F32), (0, LANES - FOX_HEADS)).reshape(1, LANES)
    wh = w[:, o1 + FOX_HEADS:].astype(BF16)

    qt, ka, vt, hq, hf, hi, hg = _proj(x2d, mod3, pre_mix_g[l].reshape(1, D), wqkv, wf, bf, wh, B, S)

    o_fox = _fox(qt, ka, vt)

    o_h = _hgrn(hq, hf, hi, hg, hgrn_gamma[l:l + 2].astype(F32), hgrn_norm_g[l].reshape(1, HGRN_WIDTH), B, S)

    wo = w_out[l].astype(BF16)
    wfo = wo[:FOX_WIDTH].reshape(FOX_HEADS, FOX_HEAD_DIM, D)
    who = wo[FOX_WIDTH:]
    keys = peer_sub_keys[l].reshape(2 * PEER_HEADS, N_KEYS, PEER_HALF).astype(BF16)
    x1, h2, scores = _outproj(x2d, o_fox, o_h, mod3, wfo, who, post_mix_g[l].reshape(1, D),
                              pre_ffn_g[l].reshape(1, D), peer_w_q[l].astype(BF16), keys, B, S)

    off, par, gate = _topk(scores)

    act = _peer_u(off, par, h2, gate, _pack_table(peer_u[l]))
    out = _peer_v(off, par, act, _pack_table(peer_v[l]), x1, mod3, post_ffn_g[l].reshape(1, D), S)
    return out.reshape(B, S, D)
```

```python
import functools
import math

import jax
import jax.numpy as jnp
from jax import lax
from jax.experimental import pallas as pl
from jax.experimental.pallas import tpu as pltpu

F32 = jnp.float32
BF16 = jnp.bfloat16
EPS = 1e-6
HIGHEST = lax.Precision.HIGHEST
LOG2E = math.log2(math.e)

FOX_HEADS = 8
FOX_HEAD_DIM = 64
FOX_WIDTH = FOX_HEADS * FOX_HEAD_DIM
HGRN_HEADS = 4
HGRN_HEAD_DIM = 128
HGRN_WIDTH = HGRN_HEADS * HGRN_HEAD_DIM
HGRN_CHUNK = 16
PEER_HEADS = 8
PEER_QUERY_DIM = 256
PEER_HALF = PEER_QUERY_DIM // 2
N_KEYS = 128
PEER_TOPK = 16
PEER_E = PEER_HEADS * PEER_TOPK
PEER_GROUP = 32
N_MOD = 6

LANES = 128
SUBLANES = 8
VMEM_LIMIT = 56 * 1024 * 1024


def _cparams(n_axes, vmem=VMEM_LIMIT):
    return pltpu.CompilerParams(dimension_semantics=("arbitrary",) * n_axes, vmem_limit_bytes=vmem)


def _dot(a, b):
    return jnp.dot(a, b, preferred_element_type=F32)


def _dot_nt(a, b):
    return lax.dot_general(a, b, (((1,), (1,)), ((), ())), preferred_element_type=F32)


def _dot_tn(a, b):
    return lax.dot_general(a, b, (((0,), (0,)), ((), ())), preferred_element_type=F32)


def _rms(x, g):
    return x * lax.rsqrt(jnp.mean(x * x, axis=-1, keepdims=True) + EPS) * g


def _ada_kernel(c_ref, w_ref, b_ref, o_ref):
    c = c_ref[...]
    cond = c * jax.nn.sigmoid(c)
    o_ref[...] = jnp.dot(cond, w_ref[...], preferred_element_type=F32, precision=HIGHEST) + b_ref[...]


def _ada(c, w, b):
    B, D = c.shape
    N = w.shape[1]
    tn = 1024
    return pl.pallas_call(
        _ada_kernel,
        grid=(N // tn,),
        in_specs=[pl.BlockSpec((B, D), lambda j: (0, 0)),
                  pl.BlockSpec((D, tn), lambda j: (0, j)),
                  pl.BlockSpec((1, tn), lambda j: (0, j))],
        out_specs=pl.BlockSpec((B, tn), lambda j: (0, j)),
        out_shape=jax.ShapeDtypeStruct((B, N), F32),
        compiler_params=_cparams(1),
        name="ada",
    )(c, w, b.reshape(1, N))


def _split3(f):
    hi = f.astype(BF16).astype(F32)
    r = f - hi
    mid = r.astype(BF16).astype(F32)
    return hi, mid, r - mid


def _proj_kernel(x_ref, mod_ref, g_ref, wqkv_ref, wf_ref, bf_ref, wh_ref, tri_ref,
                 qt_ref, ka_ref, vt_ref, hq_ref, hf_ref, hi_ref, hg_ref, carry_ref,
                 *, tiles_per_batch):
    i = pl.program_id(0)
    D = x_ref.shape[1]
    tm = x_ref.shape[0]
    Dh = FOX_HEAD_DIM
    x = x_ref[...]
    shift = mod_ref[0, :, 0:D]
    scale = mod_ref[0, :, D:2 * D]
    h = (_rms(x, g_ref[...]) * (1.0 + scale) + shift).astype(BF16)

    ff = _dot(h, wf_ref[...]) + bf_ref[...]
    logf = jnp.minimum(ff, 0.0) - jnp.log(1.0 + jnp.exp(-jnp.abs(ff)))

    @pl.when(i % tiles_per_batch == 0)
    def _():
        carry_ref[...] = jnp.zeros_like(carry_ref)

    cs = jnp.dot(tri_ref[...], logf, preferred_element_type=F32, precision=HIGHEST) + carry_ref[...]
    carry_ref[...] = cs[tm - 1:tm, :]

    qkv = _dot(h, wqkv_ref[...])
    lane = lax.broadcasted_iota(jnp.int32, (tm, Dh), 1)
    zpad = jnp.zeros((tm, Dh), F32)
    for hd in range(FOX_HEADS):
        lo = hd * Dh
        fhi, fmid, flo = _split3(cs[:, hd:hd + 1] * LOG2E)
        pieces = lambda o: jnp.where(lane == o, fhi, jnp.where(lane == o + 1, fmid, jnp.where(lane == o + 2, flo, 0.0)))
        q_aux = jnp.where(lane < 3, -1.0, pieces(3))
        k_aux = jnp.where((lane >= 3) & (lane < 6), 1.0, pieces(0))
        qa = jnp.concatenate([qkv[:, lo:lo + Dh] * (Dh ** -0.5 * LOG2E), q_aux], axis=1)
        ka = jnp.concatenate([qkv[:, FOX_WIDTH + lo:FOX_WIDTH + lo + Dh], k_aux], axis=1)
        va = jnp.concatenate([qkv[:, 2 * FOX_WIDTH + lo:2 * FOX_WIDTH + lo + Dh], zpad], axis=1)
        qt_ref[0, hd] = qa.T.astype(BF16)
        ka_ref[0, hd] = ka.astype(BF16)
        vt_ref[0, hd] = va.T[0:Dh, :].astype(BF16)

    hh = _dot(h, wh_ref[...])
    hq_ref[...] = hh[:, 0:HGRN_WIDTH]
    hf_ref[...] = hh[:, HGRN_WIDTH:2 * HGRN_WIDTH]
    hi_ref[...] = hh[:, 2 * HGRN_WIDTH:3 * HGRN_WIDTH]
    hg_ref[...] = hh[:, 3 * HGRN_WIDTH:4 * HGRN_WIDTH]


def _proj(x2d, mod3, g, wqkv, wf, bf, wh, B, S):
    T, D = x2d.shape
    tm = 256
    tpb = S // tm
    tri = (jnp.arange(tm)[:, None] >= jnp.arange(tm)[None, :]).astype(F32)
    H, Dh = FOX_HEADS, FOX_HEAD_DIM
    fox_shapes = [jax.ShapeDtypeStruct((B, H, LANES, S), BF16), jax.ShapeDtypeStruct((B, H, S, LANES), BF16),
                  jax.ShapeDtypeStruct((B, H, Dh, S), BF16)]
    fox_specs = [pl.BlockSpec((1, H, LANES, tm), lambda i: (i // tpb, 0, 0, i % tpb)),
                 pl.BlockSpec((1, H, tm, LANES), lambda i: (i // tpb, 0, i % tpb, 0)),
                 pl.BlockSpec((1, H, Dh, tm), lambda i: (i // tpb, 0, 0, i % tpb))]
    wide = jax.ShapeDtypeStruct((T, HGRN_WIDTH), F32)
    wspec = pl.BlockSpec((tm, HGRN_WIDTH), lambda i: (i, 0))
    const = lambda shape: pl.BlockSpec(shape, lambda i: (0,) * len(shape))
    return pl.pallas_call(
        functools.partial(_proj_kernel, tiles_per_batch=tpb),
        grid=(T // tm,),
        in_specs=[pl.BlockSpec((tm, D), lambda i: (i, 0)),
                  pl.BlockSpec((1, 1, N_MOD * D), lambda i: (i // tpb, 0, 0)),
                  const((1, D)), const(wqkv.shape), const(wf.shape), const(bf.shape), const(wh.shape),
                  const((tm, tm))],
        out_specs=fox_specs + [wspec, wspec, wspec, wspec],
        out_shape=fox_shapes + [wide, wide, wide, wide],
        scratch_shapes=[pltpu.VMEM((1, LANES), F32)],
        compiler_params=_cparams(1),
        name="proj",
    )(x2d, mod3, g, wqkv, wf, bf, wh, tri)


FOX_HB = 8


def _fox_kernel(qt_ref, ka_ref, vt_ref, o_ref, m_ref, l_ref, acc_ref, *, tq):
    qi = pl.program_id(2)
    m_ref[...] = jnp.full_like(m_ref, -jnp.inf)
    l_ref[...] = jnp.zeros_like(l_ref)
    acc_ref[...] = jnp.zeros_like(acc_ref)

    def step(j, masked):
        off = pl.multiple_of(j * tq, tq)
        hs = range(FOX_HB)
        s = [_dot(ka_ref[0, hh, pl.ds(off, tq), :], qt_ref[0, hh]) for hh in hs]
        if masked:
            key = lax.broadcasted_iota(jnp.int32, (tq, tq), 0)
            qry = lax.broadcasted_iota(jnp.int32, (tq, tq), 1)
            s = [jnp.where(key <= qry, sh, -jnp.inf) for sh in s]
        m_old = [m_ref[hh] for hh in hs]
        m_new = [jnp.maximum(m_old[hh], jnp.max(s[hh], axis=0, keepdims=True)) for hh in hs]
        p = [jnp.exp2(s[hh] - m_new[hh]) for hh in hs]
        alpha = [jnp.exp2(m_old[hh] - m_new[hh]) for hh in hs]
        pv = [_dot(vt_ref[0, hh, :, pl.ds(off, tq)], p[hh].astype(BF16)) for hh in hs]
        for hh in hs:
            l_ref[hh] = alpha[hh] * l_ref[hh] + jnp.sum(p[hh], axis=0, keepdims=True)
            acc_ref[hh] = alpha[hh] * acc_ref[hh] + pv[hh]
            m_ref[hh] = m_new[hh]

    step(qi, True)

    def body(j, _):
        step(j, False)
        return 0

    lax.fori_loop(0, qi, body, 0)
    for hh in range(FOX_HB):
        o_ref[0, hh] = (acc_ref[hh] / l_ref[hh]).T.astype(o_ref.dtype)


def _fox(qt, ka, vt):
    B, H, S, _ = ka.shape
    Dh = vt.shape[2]
    tq = 512
    hb = FOX_HB
    return pl.pallas_call(
        functools.partial(_fox_kernel, tq=tq),
        grid=(B, H // hb, S // tq),
        in_specs=[pl.BlockSpec((1, hb, LANES, tq), lambda b, h, i: (b, h, 0, i)),
                  pl.BlockSpec((1, hb, S, LANES), lambda b, h, i: (b, h, 0, 0)),
                  pl.BlockSpec((1, hb, Dh, S), lambda b, h, i: (b, h, 0, 0))],
        out_specs=pl.BlockSpec((1, hb, tq, Dh), lambda b, h, i: (b, h, i, 0)),
        out_shape=jax.ShapeDtypeStruct((B, H, S, Dh), BF16),
        scratch_shapes=[pltpu.VMEM((hb, 1, tq), F32), pltpu.VMEM((hb, 1, tq), F32), pltpu.VMEM((hb, Dh, tq), F32)],
        compiler_params=_cparams(3),
        name="fox",
    )(qt, ka, vt)


def _hgrn_kernel(hq_ref, hf_ref, hi_ref, hg_ref, gam_ref, ng_ref, bd_ref, bl_ref, o_ref, st_ref, os_ref):
    j = pl.program_id(1)
    tb = hq_ref.shape[0]
    C = HGRN_CHUNK
    Dk = HGRN_HEAD_DIM

    @pl.when(j == 0)
    def _():
        st_ref[...] = jnp.zeros_like(st_ref)

    gam = gam_ref[...]
    e = jnp.exp(gam - jnp.max(gam, axis=0, keepdims=True))
    lb = e[0:1, :] / jnp.sum(e, axis=0, keepdims=True)
    f = lb + (1.0 - lb) * jax.nn.sigmoid(hf_ref[...])
    lf = jnp.log(f)
    A = jnp.dot(bd_ref[...], lf, preferred_element_type=F32, precision=HIGHEST)
    AL = jnp.dot(bl_ref[...], lf, preferred_element_type=F32, precision=HIGHEST)
    q = hq_ref[...] * (Dk ** -0.5)
    kk = 1.0 - f
    qa = q * jnp.exp(A)
    ka = kk * jnp.exp(AL - A)
    eal = jnp.exp(AL)
    inp = hi_ref[...]

    tri3 = (lax.broadcasted_iota(jnp.int32, (C, C, Dk), 0) >= lax.broadcasted_iota(jnp.int32, (C, C, Dk), 1))
    for hd in range(HGRN_HEADS):
        lo = hd * Dk
        st = st_ref[hd]
        for c in range(tb // C):
            r0 = c * C
            Ac = A[r0:r0 + C, lo:lo + Dk]
            qc = q[r0:r0 + C, lo:lo + Dk]
            kc = kk[r0:r0 + C, lo:lo + Dk]
            ic = inp[r0:r0 + C, lo:lo + Dk]
            diff = Ac[:, None, :] - Ac[None, :, :]
            dec = jnp.exp(jnp.where(tri3, diff, -jnp.inf))
            sc = jnp.sum(qc[:, None, :] * kc[None, :, :] * dec, axis=-1)
            o = _dot_nt(qa[r0:r0 + C, lo:lo + Dk], st) + _dot(sc, ic)
            st = st * eal[r0:r0 + 1, lo:lo + Dk] + _dot_tn(ic, ka[r0:r0 + C, lo:lo + Dk])
            os_ref[r0:r0 + C, lo:lo + Dk] = o
        st_ref[hd] = st

    o_all = os_ref[...]
    hg = hg_ref[...]
    ng = ng_ref[...]
    for hd in range(HGRN_HEADS):
        lo = hd * Dk
        oh = _rms(o_all[:, lo:lo + Dk], ng[:, lo:lo + Dk])
        g = hg[:, lo:lo + Dk]
        o_ref[:, lo:lo + Dk] = (oh * (g * jax.nn.sigmoid(g))).astype(o_ref.dtype)


def _hgrn(hq, hf, hi, hg, gamma, norm_g, B, S):
    T, W = hq.shape
    tb = 128
    nb = S // tb
    t = jnp.arange(tb)
    same = (t[:, None] // HGRN_CHUNK) == (t[None, :] // HGRN_CHUNK)
    bd = (same & (t[:, None] >= t[None, :])).astype(F32)
    bl = same.astype(F32)
    wspec = pl.BlockSpec((tb, W), lambda b, j: (b * nb + j, 0))
    const = lambda shape: pl.BlockSpec(shape, lambda b, j: (0,) * len(shape))
    return pl.pallas_call(
        _hgrn_kernel,
        grid=(B, nb),
        in_specs=[wspec, wspec, wspec, wspec, const(gamma.shape), const((1, W)), const((tb, tb)), const((tb, tb))],
        out_specs=wspec,
        out_shape=jax.ShapeDtypeStruct((T, W), BF16),
        scratch_shapes=[pltpu.VMEM((HGRN_HEADS, HGRN_HEAD_DIM, HGRN_HEAD_DIM), F32), pltpu.VMEM((tb, W), F32)],
        compiler_params=_cparams(2),
        name="hgrn",
    )(hq, hf, hi, hg, gamma, norm_g, bd, bl)


def _outproj_kernel(x_ref, of_ref, oh_ref, mod_ref, wfo_ref, who_ref, pmg_ref, pfg_ref, wq_ref, keys_ref,
                    x1_ref, h2_ref, sc_ref):
    D = x_ref.shape[1]
    mix = _dot(oh_ref[...], who_ref[...])
    for hd in range(FOX_HEADS):
        mix = mix + _dot(of_ref[0, hd], wfo_ref[hd])
    gate_a = mod_ref[0, :, 2 * D:3 * D]
    shift_f = mod_ref[0, :, 3 * D:4 * D]
    scale_f = mod_ref[0, :, 4 * D:5 * D]
    x1 = x_ref[...] + gate_a * _rms(mix, pmg_ref[...])
    x1_ref[...] = x1
    h2 = _rms(x1, pfg_ref[...]) * (1.0 + scale_f) + shift_f
    h2_ref[...] = h2
    qp = _dot(h2.astype(BF16), wq_ref[...])
    for g in range(2 * PEER_HEADS):
        qg = qp[:, g * PEER_HALF:(g + 1) * PEER_HALF]
        qn = qg * lax.rsqrt(jnp.mean(qg * qg, axis=-1, keepdims=True) + EPS)
        sc_ref[g] = _dot_nt(keys_ref[g], qn.astype(BF16))


def _outproj(x2d, ofox, oh, mod3, wfo, who, pmg, pfg, wq, keys, B, S):
    T, D = x2d.shape
    tm = 256
    tpb = S // tm
    G = 2 * PEER_HEADS
    const = lambda shape: pl.BlockSpec(shape, lambda i: (0,) * len(shape))
    row = pl.BlockSpec((tm, D), lambda i: (i, 0))
    return pl.pallas_call(
        _outproj_kernel,
        grid=(T // tm,),
        in_specs=[row,
                  pl.BlockSpec((1, FOX_HEADS, tm, FOX_HEAD_DIM), lambda i: (i // tpb, 0, i % tpb, 0)),
                  pl.BlockSpec((tm, HGRN_WIDTH), lambda i: (i, 0)),
                  pl.BlockSpec((1, 1, N_MOD * D), lambda i: (i // tpb, 0, 0)),
                  const(wfo.shape), const(who.shape), const((1, D)), const((1, D)), const(wq.shape), const(keys.shape)],
        out_specs=[row, row, pl.BlockSpec((G, N_KEYS, tm), lambda i: (0, 0, i))],
        out_shape=[jax.ShapeDtypeStruct((T, D), F32), jax.ShapeDtypeStruct((T, D), F32),
                   jax.ShapeDtypeStruct((G, N_KEYS, T), F32)],
        compiler_params=_cparams(1),
        name="outproj",
    )(x2d, ofox, oh, mod3, wfo, who, pmg, pfg, wq, keys)


def _topk_rows(s, k, payload=None):
    n = s.shape[0]
    iota = lax.broadcasted_iota(jnp.int32, s.shape, 0)
    vals, picks = [], []
    for _ in range(k):
        m = jnp.max(s, axis=0, keepdims=True)
        ix = jnp.min(jnp.where(s == m, iota, n), axis=0, keepdims=True)
        sel = iota == ix
        vals.append(m)
        if payload is None:
            picks.append(ix)
        else:
            picks.append(jnp.sum(jnp.where(sel, payload, 0), axis=0, keepdims=True))
        s = jnp.where(sel, -jnp.inf, s)
    return jnp.concatenate(vals, axis=0), jnp.concatenate(picks, axis=0)


def _candidates(s1, i1, s2, i2):
    K = PEER_TOPK
    tm = s1.shape[1]
    row = lax.broadcasted_iota(jnp.int32, (SUBLANES, tm), 0)
    vals = [s1[0:1, :] + s2, s1[1:2, :] + s2[0:SUBLANES]]
    idxs = [i1[0:1, :] * N_KEYS + i2, i1[1:2, :] * N_KEYS + i2[0:SUBLANES]]
    for a in range(2, SUBLANES):
        keep = row < K // (a + 1)
        vals.append(jnp.where(keep, s1[a:a + 1, :] + s2[0:SUBLANES], -jnp.inf))
        idxs.append(jnp.where(keep, i1[a:a + 1, :] * N_KEYS + i2[0:SUBLANES], 0))
    vals.append(s1[SUBLANES:K] + s2[0:1, :])
    idxs.append(i1[SUBLANES:K] * N_KEYS + i2[0:1, :])
    return jnp.concatenate(vals, axis=0), jnp.concatenate(idxs, axis=0)


def _topk_kernel(sc_ref, off_ref, par_ref, gate_ref, offs_ref):
    K = PEER_TOPK
    idx_rows, gate_rows = [], []
    for hd in range(PEER_HEADS):
        s1, i1 = _topk_rows(sc_ref[2 * hd], K)
        s2, i2 = _topk_rows(sc_ref[2 * hd + 1], K)
        cand, cidx = _candidates(s1, i1, s2, i2)
        best, eidx = _topk_rows(cand, K, payload=cidx)
        p = jnp.exp(best - best[0:1, :])
        gate_rows.append(p / jnp.sum(p, axis=0, keepdims=True))
        idx_rows.append(eidx)
    idx = jnp.concatenate(idx_rows, axis=0).T
    n_half = N_KEYS * N_KEYS // 2
    offs_ref[...] = (idx & (n_half - 1)) * SUBLANES
    tm = offs_ref.shape[0]
    for k in range(PEER_GROUP):
        off_ref[k] = offs_ref[pl.ds(k, tm // PEER_GROUP, stride=PEER_GROUP), :]
    par_ref[...] = (idx >= n_half).astype(F32)
    gate_ref[...] = jnp.concatenate(gate_rows, axis=0).T


def _topk(scores):
    G, NK, T = scores.shape
    tm = 256
    out = pl.BlockSpec((tm, PEER_E), lambda i: (i, 0))
    return pl.pallas_call(
        _topk_kernel,
        grid=(T // tm,),
        in_specs=[pl.BlockSpec((G, NK, tm), lambda i: (0, 0, i))],
        out_specs=[pl.BlockSpec((PEER_GROUP, tm // PEER_GROUP, PEER_E), lambda i: (0, i, 0)), out, out],
        out_shape=[jax.ShapeDtypeStruct((PEER_GROUP, T // PEER_GROUP, PEER_E), jnp.int32),
                   jax.ShapeDtypeStruct((T, PEER_E), F32), jax.ShapeDtypeStruct((T, PEER_E), F32)],
        scratch_shapes=[pltpu.VMEM((tm, PEER_E), jnp.int32)],
        compiler_params=_cparams(1),
        name="topk",
    )(scores)


PEER_TM = 8 * PEER_GROUP
TILE_ROWS = 2 * SUBLANES
STACK = PEER_E * TILE_ROWS


def _pack_kernel(lo_ref, hi_ref, o_ref):
    nb = lo_ref.shape[0]
    lo = lax.bitcast_convert_type(lo_ref[...].astype(BF16).astype(F32), jnp.uint32)
    hi = lax.bitcast_convert_type(hi_ref[...].astype(BF16).astype(F32), jnp.uint32)
    word = (hi & jnp.uint32(0xFFFF0000)) | lax.shift_right_logical(lo, jnp.uint32(16))
    for s in range(SUBLANES):
        o_ref[pl.ds(s, nb, stride=SUBLANES), :] = word[:, s * LANES:(s + 1) * LANES]


def _pack_table(w):
    n, d = w.shape
    nb = 256
    steps = n // 2 // nb
    return pl.pallas_call(
        _pack_kernel,
        grid=(steps,),
        in_specs=[pl.BlockSpec((nb, d), lambda i: (i, 0)), pl.BlockSpec((nb, d), lambda i: (i + steps, 0))],
        out_specs=pl.BlockSpec((nb * SUBLANES, LANES), lambda i: (i, 0)),
        out_shape=jax.ShapeDtypeStruct((n // 2 * SUBLANES, LANES), jnp.uint32),
        compiler_params=_cparams(1),
        name="pack",
    )(w, w)


def _peer_consts():
    c = jnp.arange(STACK)
    expand = (c[None, :] // TILE_ROWS == jnp.arange(PEER_E)[:, None])
    fold = ((c[None, :] % TILE_ROWS) // 2 == jnp.arange(SUBLANES)[:, None])
    half = (c % 2).reshape(1, STACK)
    return expand.astype(BF16), expand.T.astype(BF16), fold.astype(F32), half.astype(F32)


def _stack2(tab_ref, off_a, off_b, g):
    def tile(o):
        return pltpu.bitcast(tab_ref[pl.ds(pl.multiple_of(o, SUBLANES), SUBLANES), :], BF16)
    w0 = jnp.concatenate([tile(off_a[g, j]) for j in range(PEER_E)], axis=0)
    w1 = jnp.concatenate([tile(off_b[g, j]) for j in range(PEER_E)], axis=0)
    return jnp.concatenate([w0, w1], axis=1)


def _split2_dot(x, w):
    hi = x.astype(BF16)
    lo = (x - hi.astype(F32)).astype(BF16)
    return _dot(hi, w) + _dot(lo, w)


def _expand_sel(par_ref, expand_ref, half_ref):
    par = _dot(par_ref[...].astype(BF16), expand_ref[...])
    return jnp.where(par == half_ref[...], 1.0, 0.0)


def _fold_rows(dst_ref, x, tm):
    for c in range(SUBLANES):
        dst_ref[pl.ds(c, tm, stride=SUBLANES), :] = x[:, c * LANES:(c + 1) * LANES]


def _peer_u_kernel(*refs):
    offs = refs[:PEER_GROUP]
    (par_ref, x_ref, gate_ref, expand_ref, collapse_ref, fold_ref, half_ref, tab_ref,
     act_ref, sel_ref, g_ref, xf_ref) = refs[PEER_GROUP:]
    tm = x_ref.shape[0]
    sel_ref[...] = _expand_sel(par_ref, expand_ref, half_ref)
    _fold_rows(xf_ref, x_ref[...], tm)
    fold = fold_ref[...]
    zero = jnp.zeros((SUBLANES, LANES), BF16)

    def pair(g, p):
        t = PEER_GROUP * g + 2 * p
        w2 = _stack2(tab_ref, offs[2 * p], offs[2 * p + 1], g)
        x0 = xf_ref[pl.ds(pl.multiple_of(t * SUBLANES, SUBLANES), SUBLANES), :].astype(BF16)
        x1 = xf_ref[pl.ds(pl.multiple_of((t + 1) * SUBLANES, SUBLANES), SUBLANES), :].astype(BF16)
        lhs = jnp.concatenate([jnp.concatenate([x0, zero], axis=1),
                               jnp.concatenate([zero, x1], axis=1)], axis=0)
        gm = _dot_nt(lhs, w2)
        g_ref[pl.ds(t, 1), :] = jnp.sum(gm[0:SUBLANES] * fold, axis=0, keepdims=True) * sel_ref[pl.ds(t, 1), :]
        g_ref[pl.ds(t + 1, 1), :] = (jnp.sum(gm[SUBLANES:] * fold, axis=0, keepdims=True)
                                     * sel_ref[pl.ds(t + 1, 1), :])

    def group(g, _):
        for p in range(PEER_GROUP // 2):
            pair(g, p)
        return 0

    lax.fori_loop(0, tm // PEER_GROUP, group, 0)
    a = _split2_dot(g_ref[...], collapse_ref[...])
    gelu = 0.5 * a * (1.0 + lax.erf(a * (2.0 ** -0.5)))
    act_ref[...] = gate_ref[...] * gelu


def _peer_v_kernel(*refs):
    offs = refs[:PEER_GROUP]
    (par_ref, act_ref, expand_ref, fold_ref, half_ref, tab_ref, x1_ref, mod_ref, g_ref,
     o_ref, ce_ref, yf_ref) = refs[PEER_GROUP:]
    tm, D = x1_ref.shape
    ce_ref[...] = _dot(act_ref[...].astype(BF16), expand_ref[...]) * _expand_sel(par_ref, expand_ref, half_ref)
    fold = fold_ref[...]

    def pair(g, p):
        t = PEER_GROUP * g + 2 * p
        w2 = _stack2(tab_ref, offs[2 * p], offs[2 * p + 1], g)
        coef = jnp.concatenate([(ce_ref[pl.ds(t, 1), :] * fold).astype(BF16),
                                (ce_ref[pl.ds(t + 1, 1), :] * fold).astype(BF16)], axis=0)
        out = _dot(coef, w2)
        yf_ref[pl.ds(pl.multiple_of(t * SUBLANES, SUBLANES), SUBLANES), :] = out[0:SUBLANES, 0:LANES]
        yf_ref[pl.ds(pl.multiple_of((t + 1) * SUBLANES, SUBLANES), SUBLANES), :] = out[SUBLANES:, LANES:]

    def group(g, _):
        for p in range(PEER_GROUP // 2):
            pair(g, p)
        return 0

    lax.fori_loop(0, tm // PEER_GROUP, group, 0)
    y = jnp.concatenate([yf_ref[pl.ds(c, tm, stride=SUBLANES), :] for c in range(SUBLANES)], axis=1)
    gate_f = mod_ref[0, :, 5 * D:6 * D]
    o_ref[...] = x1_ref[...] + gate_f * _rms(y, g_ref[...])


def _peer_specs(tab):
    tm = PEER_TM
    smem = [pl.BlockSpec((None, tm // PEER_GROUP, PEER_E), functools.partial(lambda k, i: (k, i, 0), k),
                         memory_space=pltpu.SMEM) for k in range(PEER_GROUP)]
    row = pl.BlockSpec((tm, PEER_E), lambda i: (i, 0))
    const = lambda shape: pl.BlockSpec(shape, lambda i: (0,) * len(shape))
    table = pl.BlockSpec(tab.shape, lambda i: (0, 0), pipeline_mode=pl.Buffered(1))
    return tm, smem, row, const, table


def _peer_u(off, par, h2, gate, tab):
    T, D = h2.shape
    tm, smem, row, const, table = _peer_specs(tab)
    expand, collapse, fold, half = _peer_consts()
    return pl.pallas_call(
        _peer_u_kernel,
        grid=(T // tm,),
        in_specs=smem + [row, pl.BlockSpec((tm, D), lambda i: (i, 0)), row,
                  const(expand.shape), const(collapse.shape), const(fold.shape), const(half.shape), table],
        out_specs=row,
        out_shape=jax.ShapeDtypeStruct((T, PEER_E), F32),
        scratch_shapes=[pltpu.VMEM((tm, STACK), F32), pltpu.VMEM((tm, STACK), F32),
                        pltpu.VMEM((tm * SUBLANES, LANES), F32)],
        compiler_params=_cparams(1),
        name="peer_u",
    )(*([off] * PEER_GROUP), par, h2, gate, expand, collapse, fold, half, tab)


def _peer_v(off, par, act, tab, x1, mod3, g, S):
    T, D = x1.shape
    tm, smem, row, const, table = _peer_specs(tab)
    tpb = S // tm
    expand, _, fold, half = _peer_consts()
    wide = pl.BlockSpec((tm, D), lambda i: (i, 0))
    return pl.pallas_call(
        _peer_v_kernel,
        grid=(T // tm,),
        in_specs=smem + [row, row, const(expand.shape), const(fold.shape), const(half.shape), table,
                  wide, pl.BlockSpec((1, 1, N_MOD * D), lambda i: (i // tpb, 0, 0)), const((1, D))],
        out_specs=wide,
        out_shape=jax.ShapeDtypeStruct((T, D), F32),
        scratch_shapes=[pltpu.VMEM((tm, STACK), F32), pltpu.VMEM((tm * SUBLANES, LANES), F32)],
        compiler_params=_cparams(1),
        name="peer_v",
    )(*([off] * PEER_GROUP), par, act, expand, fold, half, tab, x1, mod3, g)


def kernel(x, c, w_ada, b_ada, pre_mix_g, post_mix_g, w_in, b_fox_f, hgrn_gamma, hgrn_norm_g, w_out, pre_ffn_g,
           post_ffn_g, peer_w_q, peer_sub_keys, peer_u, peer_v):
    B, S, D = x.shape
    T = B * S
    depth = w_in.shape[0]
    assert depth == 1, "single-layer block"
    l = 0
    x2d = x.reshape(T, D)

    mod3 = _ada(c, w_ada[l], b_ada[l]).reshape(B, 1, N_MOD * D)

    w = w_in[l]
    o1 = 3 * FOX_WIDTH
    wqkv = w[:, :o1].astype(BF16)
    wf = jnp.pad(w[:, o1:o1 + FOX_HEADS], ((0, 0), (0, LANES - FOX_HEADS))).astype(BF16)
    bf = jnp.pad(b_fox_f[l].astype(F32), (0, LANES - FOX_HEADS)).reshape(1, LANES)
    wh = w[:, o1 + FOX_HEADS:].astype(BF16)

    qt, ka, vt, hq, hf, hi, hg = _proj(x2d, mod3, pre_mix_g[l].reshape(1, D), wqkv, wf, bf, wh, B, S)

    o_fox = _fox(qt, ka, vt)

    o_h = _hgrn(hq, hf, hi, hg, hgrn_gamma[l:l + 2].astype(F32), hgrn_norm_g[l].reshape(1, HGRN_WIDTH), B, S)

    wo = w_out[l].astype(BF16)
    wfo = wo[:FOX_WIDTH].reshape(FOX_HEADS, FOX_HEAD_DIM, D)
    who = wo[FOX_WIDTH:]
    keys = peer_sub_keys[l].reshape(2 * PEER_HEADS, N_KEYS, PEER_HALF).astype(BF16)
    x1, h2, scores = _outproj(x2d, o_fox, o_h, mod3, wfo, who, post_mix_g[l].reshape(1, D),
                              pre_ffn_g[l].reshape(1, D), peer_w_q[l].astype(BF16), keys, B, S)

    off, par, gate = _topk(scores)

    act = _peer_u(off, par, h2, gate, _pack_table(peer_u[l]))
    out = _peer_v(off, par, act, _pack_table(peer_v[l]), x1, mod3, post_ffn_g[l].reshape(1, D), S)
    return out.reshape(B, S, D)
```

```python
import functools
import math

import jax
import jax.numpy as jnp
from jax import lax
from jax.experimental import pallas as pl
from jax.experimental.pallas import tpu as pltpu

F32 = jnp.float32
BF16 = jnp.bfloat16
EPS = 1e-6
HIGHEST = lax.Precision.HIGHEST
LOG2E = math.log2(math.e)

FOX_HEADS = 8
FOX_HEAD_DIM = 64
FOX_WIDTH = FOX_HEADS * FOX_HEAD_DIM
HGRN_HEADS = 4
HGRN_HEAD_DIM = 128
HGRN_WIDTH = HGRN_HEADS * HGRN_HEAD_DIM
HGRN_CHUNK = 16
PEER_HEADS = 8
PEER_QUERY_DIM = 256
PEER_HALF = PEER_QUERY_DIM // 2
N_KEYS = 128
PEER_TOPK = 16
PEER_E = PEER_HEADS * PEER_TOPK
PEER_GROUP = 32
N_MOD = 6

LANES = 128
SUBLANES = 8
VMEM_LIMIT = 56 * 1024 * 1024


def _cparams(n_axes, vmem=VMEM_LIMIT):
    return pltpu.CompilerParams(dimension_semantics=("arbitrary",) * n_axes, vmem_limit_bytes=vmem)


def _dot(a, b):
    return jnp.dot(a, b, preferred_element_type=F32)


def _dot_nt(a, b):
    return lax.dot_general(a, b, (((1,), (1,)), ((), ())), preferred_element_type=F32)


def _dot_tn(a, b):
    return lax.dot_general(a, b, (((0,), (0,)), ((), ())), preferred_element_type=F32)


def _rms(x, g):
    return x * lax.rsqrt(jnp.mean(x * x, axis=-1, keepdims=True) + EPS) * g


def _ada_kernel(c_ref, w_ref, b_ref, o_ref):
    c = c_ref[...]
    cond = c * jax.nn.sigmoid(c)
    o_ref[...] = jnp.dot(cond, w_ref[...], preferred_element_type=F32, precision=HIGHEST) + b_ref[...]


def _ada(c, w, b):
    B, D = c.shape
    N = w.shape[1]
    tn = 1024
    return pl.pallas_call(
        _ada_kernel,
        grid=(N // tn,),
        in_specs=[pl.BlockSpec((B, D), lambda j: (0, 0)),
                  pl.BlockSpec((D, tn), lambda j: (0, j)),
                  pl.BlockSpec((1, tn), lambda j: (0, j))],
        out_specs=pl.BlockSpec((B, tn), lambda j: (0, j)),
        out_shape=jax.ShapeDtypeStruct((B, N), F32),
        compiler_params=_cparams(1),
        name="ada",
    )(c, w, b.reshape(1, N))


def _split3(f):
    hi = f.astype(BF16).astype(F32)
    r = f - hi
    mid = r.astype(BF16).astype(F32)
    return hi, mid, r - mid


def _proj_kernel(x_ref, mod_ref, g_ref, wqkv_ref, wf_ref, bf_ref, wh_ref, tri_ref,
                 qt_ref, ka_ref, vt_ref, hq_ref, hf_ref, hi_ref, hg_ref, carry_ref,
                 *, tiles_per_batch):
    i = pl.program_id(0)
    D = x_ref.shape[1]
    tm = x_ref.shape[0]
    Dh = FOX_HEAD_DIM
    x = x_ref[...]
    shift = mod_ref[0, :, 0:D]
    scale = mod_ref[0, :, D:2 * D]
    h = (_rms(x, g_ref[...]) * (1.0 + scale) + shift).astype(BF16)

    ff = _dot(h, wf_ref[...]) + bf_ref[...]
    logf = jnp.minimum(ff, 0.0) - jnp.log(1.0 + jnp.exp(-jnp.abs(ff)))

    @pl.when(i % tiles_per_batch == 0)
    def _():
        carry_ref[...] = jnp.zeros_like(carry_ref)

    cs = jnp.dot(tri_ref[...], logf, preferred_element_type=F32, precision=HIGHEST) + carry_ref[...]
    carry_ref[...] = cs[tm - 1:tm, :]

    qkv = _dot(h, wqkv_ref[...])
    lane = lax.broadcasted_iota(jnp.int32, (tm, Dh), 1)
    zpad = jnp.zeros((tm, Dh), F32)
    for hd in range(FOX_HEADS):
        lo = hd * Dh
        fhi, fmid, flo = _split3(cs[:, hd:hd + 1] * LOG2E)
        pieces = lambda o: jnp.where(lane == o, fhi, jnp.where(lane == o + 1, fmid, jnp.where(lane == o + 2, flo, 0.0)))
        q_aux = jnp.where(lane < 3, -1.0, pieces(3))
        k_aux = jnp.where((lane >= 3) & (lane < 6), 1.0, pieces(0))
        qa = jnp.concatenate([qkv[:, lo:lo + Dh] * (Dh ** -0.5 * LOG2E), q_aux], axis=1)
        ka = jnp.concatenate([qkv[:, FOX_WIDTH + lo:FOX_WIDTH + lo + Dh], k_aux], axis=1)
        va = jnp.concatenate([qkv[:, 2 * FOX_WIDTH + lo:2 * FOX_WIDTH + lo + Dh], zpad], axis=1)
        qt_ref[0, hd] = qa.T.astype(BF16)
        ka_ref[0, hd] = ka.astype(BF16)
        vt_ref[0, hd] = va.T[0:Dh, :].astype(BF16)

    hh = _dot(h, wh_ref[...])
    hq_ref[...] = hh[:, 0:HGRN_WIDTH]
    hf_ref[...] = hh[:, HGRN_WIDTH:2 * HGRN_WIDTH]
    hi_ref[...] = hh[:, 2 * HGRN_WIDTH:3 * HGRN_WIDTH]
    hg_ref[...] = hh[:, 3 * HGRN_WIDTH:4 * HGRN_WIDTH]


def _proj(x2d, mod3, g, wqkv, wf, bf, wh, B, S):
    T, D = x2d.shape
    tm = 256
    tpb = S // tm
    tri = (jnp.arange(tm)[:, None] >= jnp.arange(tm)[None, :]).astype(F32)
    H, Dh = FOX_HEADS, FOX_HEAD_DIM
    fox_shapes = [jax.ShapeDtypeStruct((B, H, LANES, S), BF16), jax.ShapeDtypeStruct((B, H, S, LANES), BF16),
                  jax.ShapeDtypeStruct((B, H, Dh, S), BF16)]
    fox_specs = [pl.BlockSpec((1, H, LANES, tm), lambda i: (i // tpb, 0, 0, i % tpb)),
                 pl.BlockSpec((1, H, tm, LANES), lambda i: (i // tpb, 0, i % tpb, 0)),
                 pl.BlockSpec((1, H, Dh, tm), lambda i: (i // tpb, 0, 0, i % tpb))]
    wide = jax.ShapeDtypeStruct((T, HGRN_WIDTH), F32)
    wspec = pl.BlockSpec((tm, HGRN_WIDTH), lambda i: (i, 0))
    const = lambda shape: pl.BlockSpec(shape, lambda i: (0,) * len(shape))
    return pl.pallas_call(
        functools.partial(_proj_kernel, tiles_per_batch=tpb),
        grid=(T // tm,),
        in_specs=[pl.BlockSpec((tm, D), lambda i: (i, 0)),
                  pl.BlockSpec((1, 1, N_MOD * D), lambda i: (i // tpb, 0, 0)),
                  const((1, D)), const(wqkv.shape), const(wf.shape), const(bf.shape), const(wh.shape),
                  const((tm, tm))],
        out_specs=fox_specs + [wspec, wspec, wspec, wspec],
        out_shape=fox_shapes + [wide, wide, wide, wide],
        scratch_shapes=[pltpu.VMEM((1, LANES), F32)],
        compiler_params=_cparams(1),
        name="proj",
    )(x2d, mod3, g, wqkv, wf, bf, wh, tri)


FOX_HB = 8


def _fox_kernel(qt_ref, ka_ref, vt_ref, o_ref, m_ref, l_ref, acc_ref, *, tq):
    qi = pl.program_id(2)
    m_ref[...] = jnp.full_like(m_ref, -jnp.inf)
    l_ref[...] = jnp.zeros_like(l_ref)
    acc_ref[...] = jnp.zeros_like(acc_ref)

    def step(j, masked):
        off = pl.multiple_of(j * tq, tq)
        hs = range(FOX_HB)
        s = [_dot(ka_ref[0, hh, pl.ds(off, tq), :], qt_ref[0, hh]) for hh in hs]
        if masked:
            key = lax.broadcasted_iota(jnp.int32, (tq, tq), 0)
            qry = lax.broadcasted_iota(jnp.int32, (tq, tq), 1)
            s = [jnp.where(key <= qry, sh, -jnp.inf) for sh in s]
        m_old = [m_ref[hh] for hh in hs]
        m_new = [jnp.maximum(m_old[hh], jnp.max(s[hh], axis=0, keepdims=True)) for hh in hs]
        p = [jnp.exp2(s[hh] - m_new[hh]) for hh in hs]
        alpha = [jnp.exp2(m_old[hh] - m_new[hh]) for hh in hs]
        pv = [_dot(vt_ref[0, hh, :, pl.ds(off, tq)], p[hh].astype(BF16)) for hh in hs]
        for hh in hs:
            l_ref[hh] = alpha[hh] * l_ref[hh] + jnp.sum(p[hh], axis=0, keepdims=True)
            acc_ref[hh] = alpha[hh] * acc_ref[hh] + pv[hh]
            m_ref[hh] = m_new[hh]

    step(qi, True)

    def body(j, _):
        step(j, False)
        return 0

    lax.fori_loop(0, qi, body, 0)
    for hh in range(FOX_HB):
        o_ref[0, hh] = (acc_ref[hh] / l_ref[hh]).T.astype(o_ref.dtype)


def _fox(qt, ka, vt):
    B, H, S, _ = ka.shape
    Dh = vt.shape[2]
    tq = 512
    hb = FOX_HB
    return pl.pallas_call(
        functools.partial(_fox_kernel, tq=tq),
        grid=(B, H // hb, S // tq),
        in_specs=[pl.BlockSpec((1, hb, LANES, tq), lambda b, h, i: (b, h, 0, i)),
                  pl.BlockSpec((1, hb, S, LANES), lambda b, h, i: (b, h, 0, 0)),
                  pl.BlockSpec((1, hb, Dh, S), lambda b, h, i: (b, h, 0, 0))],
        out_specs=pl.BlockSpec((1, hb, tq, Dh), lambda b, h, i: (b, h, i, 0)),
        out_shape=jax.ShapeDtypeStruct((B, H, S, Dh), BF16),
        scratch_shapes=[pltpu.VMEM((hb, 1, tq), F32), pltpu.VMEM((hb, 1, tq), F32), pltpu.VMEM((hb, Dh, tq), F32)],
        compiler_params=_cparams(3),
        name="fox",
    )(qt, ka, vt)


def _hgrn_kernel(hq_ref, hf_ref, hi_ref, hg_ref, gam_ref, ng_ref, bd_ref, bl_ref, o_ref, st_ref, os_ref):
    j = pl.program_id(1)
    tb = hq_ref.shape[0]
    C = HGRN_CHUNK
    Dk = HGRN_HEAD_DIM

    @pl.when(j == 0)
    def _():
        st_ref[...] = jnp.zeros_like(st_ref)

    gam = gam_ref[...]
    e = jnp.exp(gam - jnp.max(gam, axis=0, keepdims=True))
    lb = e[0:1, :] / jnp.sum(e, axis=0, keepdims=True)
    f = lb + (1.0 - lb) * jax.nn.sigmoid(hf_ref[...])
    lf = jnp.log(f)
    A = jnp.dot(bd_ref[...], lf, preferred_element_type=F32, precision=HIGHEST)
    AL = jnp.dot(bl_ref[...], lf, preferred_element_type=F32, precision=HIGHEST)
    q = hq_ref[...] * (Dk ** -0.5)
    kk = 1.0 - f
    qa = q * jnp.exp(A)
    ka = kk * jnp.exp(AL - A)
    eal = jnp.exp(AL)
    inp = hi_ref[...]

    tri3 = (lax.broadcasted_iota(jnp.int32, (C, C, Dk), 0) >= lax.broadcasted_iota(jnp.int32, (C, C, Dk), 1))
    for hd in range(HGRN_HEADS):
        lo = hd * Dk
        st = st_ref[hd]
        for c in range(tb // C):
            r0 = c * C
            Ac = A[r0:r0 + C, lo:lo + Dk]
            qc = q[r0:r0 + C, lo:lo + Dk]
            kc = kk[r0:r0 + C, lo:lo + Dk]
            ic = inp[r0:r0 + C, lo:lo + Dk]
            diff = Ac[:, None, :] - Ac[None, :, :]
            dec = jnp.exp(jnp.where(tri3, diff, -jnp.inf))
            sc = jnp.sum(qc[:, None, :] * kc[None, :, :] * dec, axis=-1)
            o = _dot_nt(qa[r0:r0 + C, lo:lo + Dk], st) + _dot(sc, ic)
            st = st * eal[r0:r0 + 1, lo:lo + Dk] + _dot_tn(ic, ka[r0:r0 + C, lo:lo + Dk])
            os_ref[r0:r0 + C, lo:lo + Dk] = o
        st_ref[hd] = st

    o_all = os_ref[...]
    hg = hg_ref[...]
    ng = ng_ref[...]
    for hd in range(HGRN_HEADS):
        lo = hd * Dk
        oh = _rms(o_all[:, lo:lo + Dk], ng[:, lo:lo + Dk])
        g = hg[:, lo:lo + Dk]
        o_ref[:, lo:lo + Dk] = (oh * (g * jax.nn.sigmoid(g))).astype(o_ref.dtype)


def _hgrn(hq, hf, hi, hg, gamma, norm_g, B, S):
    T, W = hq.shape
    tb = 128
    nb = S // tb
    t = jnp.arange(tb)
    same = (t[:, None] // HGRN_CHUNK) == (t[None, :] // HGRN_CHUNK)
    bd = (same & (t[:, None] >= t[None, :])).astype(F32)
    bl = same.astype(F32)
    wspec = pl.BlockSpec((tb, W), lambda b, j: (b * nb + j, 0))
    const = lambda shape: pl.BlockSpec(shape, lambda b, j: (0,) * len(shape))
    return pl.pallas_call(
        _hgrn_kernel,
        grid=(B, nb),
        in_specs=[wspec, wspec, wspec, wspec, const(gamma.shape), const((1, W)), const((tb, tb)), const((tb, tb))],
        out_specs=wspec,
        out_shape=jax.ShapeDtypeStruct((T, W), BF16),
        scratch_shapes=[pltpu.VMEM((HGRN_HEADS, HGRN_HEAD_DIM, HGRN_HEAD_DIM), F32), pltpu.VMEM((tb, W), F32)],
        compiler_params=_cparams(2),
        name="hgrn",
    )(hq, hf, hi, hg, gamma, norm_g, bd, bl)


def _outproj_kernel(x_ref, of_ref, oh_ref, mod_ref, wfo_ref, who_ref, pmg_ref, pfg_ref, wq_ref, keys_ref,
                    x1_ref, h2_ref, sc_ref):
    D = x_ref.shape[1]
    mix = _dot(oh_ref[...], who_ref[...])
    for hd in range(FOX_HEADS):
        mix = mix + _dot(of_ref[0, hd], wfo_ref[hd])
    gate_a = mod_ref[0, :, 2 * D:3 * D]
    shift_f = mod_ref[0, :, 3 * D:4 * D]
    scale_f = mod_ref[0, :, 4 * D:5 * D]
    x1 = x_ref[...] + gate_a * _rms(mix, pmg_ref[...])
    x1_ref[...] = x1
    h2 = _rms(x1, pfg_ref[...]) * (1.0 + scale_f) + shift_f
    h2_ref[...] = h2
    qp = _dot(h2.astype(BF16), wq_ref[...])
    for g in range(2 * PEER_HEADS):
        qg = qp[:, g * PEER_HALF:(g + 1) * PEER_HALF]
        qn = qg * lax.rsqrt(jnp.mean(qg * qg, axis=-1, keepdims=True) + EPS)
        sc_ref[g] = _dot_nt(keys_ref[g], qn.astype(BF16))


def _outproj(x2d, ofox, oh, mod3, wfo, who, pmg, pfg, wq, keys, B, S):
    T, D = x2d.shape
    tm = 256
    tpb = S // tm
    G = 2 * PEER_HEADS
    const = lambda shape: pl.BlockSpec(shape, lambda i: (0,) * len(shape))
    row = pl.BlockSpec((tm, D), lambda i: (i, 0))
    return pl.pallas_call(
        _outproj_kernel,
        grid=(T // tm,),
        in_specs=[row,
                  pl.BlockSpec((1, FOX_HEADS, tm, FOX_HEAD_DIM), lambda i: (i // tpb, 0, i % tpb, 0)),
                  pl.BlockSpec((tm, HGRN_WIDTH), lambda i: (i, 0)),
                  pl.BlockSpec((1, 1, N_MOD * D), lambda i: (i // tpb, 0, 0)),
                  const(wfo.shape), const(who.shape), const((1, D)), const((1, D)), const(wq.shape), const(keys.shape)],
        out_specs=[row, row, pl.BlockSpec((G, N_KEYS, tm), lambda i: (0, 0, i))],
        out_shape=[jax.ShapeDtypeStruct((T, D), F32), jax.ShapeDtypeStruct((T, D), F32),
                   jax.ShapeDtypeStruct((G, N_KEYS, T), F32)],
        compiler_params=_cparams(1),
        name="outproj",
    )(x2d, ofox, oh, mod3, wfo, who, pmg, pfg, wq, keys)


TOPK_TM = SUBLANES * LANES


def _cx(a, b, desc):
    (va, ia), (vb, ib) = a, b
    swap = (vb > va) if desc else (vb < va)
    return ((jnp.where(swap, vb, va), jnp.where(swap, ib, ia)),
            (jnp.where(swap, va, vb), jnp.where(swap, ia, ib)))


def _bitonic_merge(x, desc):
    n = len(x)
    x = list(x)
    d = n // 2
    while d >= 1:
        for i in range(n):
            if (i // d) % 2 == 0:
                x[i], x[i + d] = _cx(x[i], x[i + d], desc)
        d //= 2
    return x


def _bitonic_sort(x, desc):
    n = len(x)
    if n == 1:
        return list(x)
    lo = _bitonic_sort(x[:n // 2], True)
    hi = _bitonic_sort(x[n // 2:], False)
    return _bitonic_merge(lo + hi, desc)


def _tile(ref, row):
    return ref[pl.ds(pl.multiple_of(row * SUBLANES, SUBLANES), SUBLANES), :]


def _put(ref, row, x):
    ref[pl.ds(pl.multiple_of(row * SUBLANES, SUBLANES), SUBLANES), :] = x


def _top16_network(s_ref, n, p_ref, va_ref, ia_ref, vb_ref, ib_ref):
    K = PEER_TOPK
    shape = (SUBLANES, LANES)

    def store_block(v_ref, i_ref, blk, pairs):
        rev = blk % 2
        for k, (v, i) in enumerate(pairs):
            pos = blk * K + k + rev * (K - 1 - 2 * k)
            _put(v_ref, pos, v)
            _put(i_ref, pos, i)

    def sort_block(blk, c):
        tag = (lambda r: jnp.full(shape, r, jnp.int32)) if p_ref is None else (lambda r: _tile(p_ref, r))
        pairs = [(_tile(s_ref, blk * K + k), tag(blk * K + k)) for k in range(K)]
        store_block(va_ref, ia_ref, blk, _bitonic_sort(pairs, True))
        return c

    lax.fori_loop(0, n // K, sort_block, 0)

    def merge_round(src_v, src_i, dst_v, dst_i, n_pairs):
        def merge(p, c):
            top = []
            for k in range(K):
                va, ia = _tile(src_v, 2 * p * K + k), _tile(src_i, 2 * p * K + k)
                vb, ib = _tile(src_v, (2 * p + 1) * K + k), _tile(src_i, (2 * p + 1) * K + k)
                take_b = vb > va
                top.append((jnp.where(take_b, vb, va), jnp.where(take_b, ib, ia)))
            store_block(dst_v, dst_i, p, _bitonic_merge(top, True))
            return c
        lax.fori_loop(0, n_pairs, merge, 0)

    src, dst = (va_ref, ia_ref), (vb_ref, ib_ref)
    blocks = n // K
    while blocks > 1:
        merge_round(*src, *dst, blocks // 2)
        src, dst = dst, src
        blocks //= 2
    return ([src[0][k * SUBLANES:(k + 1) * SUBLANES, :] for k in range(K)],
            [src[1][k * SUBLANES:(k + 1) * SUBLANES, :] for k in range(K)])


def _untied(s_ref, n, vals):
    ok = vals[0] > vals[1]
    for a in range(1, PEER_TOPK - 1):
        ok = ok & (vals[a] > vals[a + 1])
    last = vals[-1]

    def count(c, cnt):
        for k in range(SUBLANES):
            cnt = cnt + jnp.where(_tile(s_ref, c * SUBLANES + k) >= last, 1, 0)
        return cnt
    cnt = lax.fori_loop(0, n // SUBLANES, count, jnp.zeros(last.shape, jnp.int32))
    return ok & (cnt == PEER_TOPK)


def _top16_exact_loop(s_ref, n, p_ref):
    shape = (SUBLANES, LANES)
    vals, picks = [], []
    for _ in range(PEER_TOPK):
        m = lax.fori_loop(0, n, lambda k, m: jnp.maximum(m, _tile(s_ref, k)), jnp.full(shape, -jnp.inf, F32))
        ix = lax.fori_loop(0, n, lambda k, ix: jnp.minimum(ix, jnp.where(_tile(s_ref, k) == m, k, n)),
                           jnp.full(shape, n, jnp.int32))

        def remove(k, pick):
            hit = ix == k
            _put(s_ref, k, jnp.where(hit, -jnp.inf, _tile(s_ref, k)))
            return pick if p_ref is None else pick + jnp.where(hit, _tile(p_ref, k), 0)
        pick = lax.fori_loop(0, n, remove, jnp.zeros(shape, jnp.int32))
        vals.append(m)
        picks.append(ix if p_ref is None else pick)
    return vals, picks


def _top16(s_ref, n, p_ref, bufs):
    vals, tags = _top16_network(s_ref, n, p_ref, *bufs)
    ok = jnp.min(jnp.where(_untied(s_ref, n, vals), 1, 0)) > 0
    pack = lambda v, i: (jnp.stack(v), jnp.stack(i))
    v, i = lax.cond(ok, lambda: pack(vals, tags), lambda: pack(*_top16_exact_loop(s_ref, n, p_ref)))
    return [v[a] for a in range(PEER_TOPK)], [i[a] for a in range(PEER_TOPK)]


N_CAND = 64


def _topk_kernel(sc_ref, off_ref, par_ref, gate_ref, s_ref, p_ref, va_ref, ia_ref, vb_ref, ib_ref, e_ref, offs_ref):
    K = PEER_TOPK
    tm = TOPK_TM
    bufs = (va_ref, ia_ref, vb_ref, ib_ref)
    shape = (SUBLANES, LANES)

    def half_topk(g):
        for s in range(SUBLANES):
            s_ref[pl.ds(s, N_KEYS, stride=SUBLANES), :] = sc_ref[g, :, s * LANES:(s + 1) * LANES]
        return _top16(s_ref, N_KEYS, None, bufs)

    for hd in range(PEER_HEADS):
        s1, i1 = half_topk(2 * hd)
        s2, i2 = half_topk(2 * hd + 1)
        r = 0
        for a in range(K):
            for b in range(K // (a + 1)):
                _put(s_ref, r, s1[a] + s2[b])
                _put(p_ref, r, i1[a] * N_KEYS + i2[b])
                r += 1
        for r in range(r, N_CAND):
            _put(s_ref, r, jnp.full(shape, -jnp.inf, F32))
            _put(p_ref, r, jnp.zeros(shape, jnp.int32))
        best, eidx = _top16(s_ref, N_CAND, p_ref, bufs)
        p = [jnp.exp(bv - best[0]) for bv in best]
        tot = functools.reduce(jnp.add, p)
        for a in range(K):
            e = hd * K + a
            e_ref[0, e * SUBLANES:(e + 1) * SUBLANES, :] = eidx[a].astype(F32)
            e_ref[1, e * SUBLANES:(e + 1) * SUBLANES, :] = p[a] / tot

    n_half = N_KEYS * N_KEYS // 2
    for s in range(SUBLANES):
        idx = e_ref[0, pl.ds(s, PEER_E, stride=SUBLANES), :].T.astype(jnp.int32)
        gate_ref[s * LANES:(s + 1) * LANES, :] = e_ref[1, pl.ds(s, PEER_E, stride=SUBLANES), :].T
        par_ref[s * LANES:(s + 1) * LANES, :] = (idx >= n_half).astype(F32)
        offs_ref[s * LANES:(s + 1) * LANES, :] = (idx & (n_half - 1)) * SUBLANES
    for k in range(PEER_GROUP):
        off_ref[k] = offs_ref[pl.ds(k, tm // PEER_GROUP, stride=PEER_GROUP), :]


def _topk(scores):
    G, NK, T = scores.shape
    tm = TOPK_TM
    out = pl.BlockSpec((tm, PEER_E), lambda i: (i, 0))
    return pl.pallas_call(
        _topk_kernel,
        grid=(T // tm,),
        in_specs=[pl.BlockSpec((G, NK, tm), lambda i: (0, 0, i))],
        out_specs=[pl.BlockSpec((PEER_GROUP, tm // PEER_GROUP, PEER_E), lambda i: (0, i, 0)), out, out],
        out_shape=[jax.ShapeDtypeStruct((PEER_GROUP, T // PEER_GROUP, PEER_E), jnp.int32),
                   jax.ShapeDtypeStruct((T, PEER_E), F32), jax.ShapeDtypeStruct((T, PEER_E), F32)],
        scratch_shapes=[pltpu.VMEM((N_KEYS * SUBLANES, LANES), F32), pltpu.VMEM((N_CAND * SUBLANES, LANES), jnp.int32),
                        pltpu.VMEM((N_KEYS * SUBLANES, LANES), F32), pltpu.VMEM((N_KEYS * SUBLANES, LANES), jnp.int32),
                        pltpu.VMEM((N_KEYS * SUBLANES // 2, LANES), F32),
                        pltpu.VMEM((N_KEYS * SUBLANES // 2, LANES), jnp.int32),
                        pltpu.VMEM((2, PEER_E * SUBLANES, LANES), F32),
                        pltpu.VMEM((tm, PEER_E), jnp.int32)],
        compiler_params=_cparams(1),
        name="topk",
    )(scores)


PEER_TM = 8 * PEER_GROUP
TILE_ROWS = 2 * SUBLANES
STACK = PEER_E * TILE_ROWS


def _pack_kernel(lo_ref, hi_ref, o_ref):
    nb = lo_ref.shape[0]
    lo = lax.bitcast_convert_type(lo_ref[...].astype(BF16).astype(F32), jnp.uint32)
    hi = lax.bitcast_convert_type(hi_ref[...].astype(BF16).astype(F32), jnp.uint32)
    word = (hi & jnp.uint32(0xFFFF0000)) | lax.shift_right_logical(lo, jnp.uint32(16))
    for s in range(SUBLANES):
        o_ref[pl.ds(s, nb, stride=SUBLANES), :] = word[:, s * LANES:(s + 1) * LANES]


def _pack_table(w):
    n, d = w.shape
    nb = 256
    steps = n // 2 // nb
    return pl.pallas_call(
        _pack_kernel,
        grid=(steps,),
        in_specs=[pl.BlockSpec((nb, d), lambda i: (i, 0)), pl.BlockSpec((nb, d), lambda i: (i + steps, 0))],
        out_specs=pl.BlockSpec((nb * SUBLANES, LANES), lambda i: (i, 0)),
        out_shape=jax.ShapeDtypeStruct((n // 2 * SUBLANES, LANES), jnp.uint32),
        compiler_params=_cparams(1),
        name="pack",
    )(w, w)


def _peer_consts():
    c = jnp.arange(STACK)
    expand = (c[None, :] // TILE_ROWS == jnp.arange(PEER_E)[:, None])
    fold = ((c[None, :] % TILE_ROWS) // 2 == jnp.arange(SUBLANES)[:, None])
    half = (c % 2).reshape(1, STACK)
    return expand.astype(BF16), expand.T.astype(BF16), fold.astype(F32), half.astype(F32)


def _stack2(tab_ref, off_a, off_b, g):
    def tile(o):
        return pltpu.bitcast(tab_ref[pl.ds(pl.multiple_of(o, SUBLANES), SUBLANES), :], BF16)
    w0 = jnp.concatenate([tile(off_a[g, j]) for j in range(PEER_E)], axis=0)
    w1 = jnp.concatenate([tile(off_b[g, j]) for j in range(PEER_E)], axis=0)
    return jnp.concatenate([w0, w1], axis=1)


def _split2_dot(x, w):
    hi = x.astype(BF16)
    lo = (x - hi.astype(F32)).astype(BF16)
    return _dot(hi, w) + _dot(lo, w)


def _expand_sel(par_ref, expand_ref, half_ref):
    par = _dot(par_ref[...].astype(BF16), expand_ref[...])
    return jnp.where(par == half_ref[...], 1.0, 0.0)


def _fold_rows(dst_ref, x, tm):
    for c in range(SUBLANES):
        dst_ref[pl.ds(c, tm, stride=SUBLANES), :] = x[:, c * LANES:(c + 1) * LANES]


def _peer_u_kernel(*refs):
    offs = refs[:PEER_GROUP]
    (par_ref, x_ref, gate_ref, expand_ref, collapse_ref, fold_ref, half_ref, tab_ref,
     act_ref, sel_ref, g_ref, xf_ref) = refs[PEER_GROUP:]
    tm = x_ref.shape[0]
    sel_ref[...] = _expand_sel(par_ref, expand_ref, half_ref)
    _fold_rows(xf_ref, x_ref[...], tm)
    fold = fold_ref[...]
    zero = jnp.zeros((SUBLANES, LANES), BF16)

    def pair(g, p):
        t = PEER_GROUP * g + 2 * p
        w2 = _stack2(tab_ref, offs[2 * p], offs[2 * p + 1], g)
        x0 = xf_ref[pl.ds(pl.multiple_of(t * SUBLANES, SUBLANES), SUBLANES), :].astype(BF16)
        x1 = xf_ref[pl.ds(pl.multiple_of((t + 1) * SUBLANES, SUBLANES), SUBLANES), :].astype(BF16)
        lhs = jnp.concatenate([jnp.concatenate([x0, zero], axis=1),
                               jnp.concatenate([zero, x1], axis=1)], axis=0)
        gm = _dot_nt(lhs, w2)
        g_ref[pl.ds(t, 1), :] = jnp.sum(gm[0:SUBLANES] * fold, axis=0, keepdims=True) * sel_ref[pl.ds(t, 1), :]
        g_ref[pl.ds(t + 1, 1), :] = (jnp.sum(gm[SUBLANES:] * fold, axis=0, keepdims=True)
                                     * sel_ref[pl.ds(t + 1, 1), :])

    def group(g, _):
        for p in range(PEER_GROUP // 2):
            pair(g, p)
        return 0

    lax.fori_loop(0, tm // PEER_GROUP, group, 0)
    a = _split2_dot(g_ref[...], collapse_ref[...])
    gelu = 0.5 * a * (1.0 + lax.erf(a * (2.0 ** -0.5)))
    act_ref[...] = gate_ref[...] * gelu


def _peer_v_kernel(*refs):
    offs = refs[:PEER_GROUP]
    (par_ref, act_ref, expand_ref, fold_ref, half_ref, tab_ref, x1_ref, mod_ref, g_ref,
     o_ref, ce_ref, yf_ref) = refs[PEER_GROUP:]
    tm, D = x1_ref.shape
    ce_ref[...] = _dot(act_ref[...].astype(BF16), expand_ref[...]) * _expand_sel(par_ref, expand_ref, half_ref)
    fold = fold_ref[...]

    def pair(g, p):
        t = PEER_GROUP * g + 2 * p
        w2 = _stack2(tab_ref, offs[2 * p], offs[2 * p + 1], g)
        coef = jnp.concatenate([(ce_ref[pl.ds(t, 1), :] * fold).astype(BF16),
                                (ce_ref[pl.ds(t + 1, 1), :] * fold).astype(BF16)], axis=0)
        out = _dot(coef, w2)
        yf_ref[pl.ds(pl.multiple_of(t * SUBLANES, SUBLANES), SUBLANES), :] = out[0:SUBLANES, 0:LANES]
        yf_ref[pl.ds(pl.multiple_of((t + 1) * SUBLANES, SUBLANES), SUBLANES), :] = out[SUBLANES:, LANES:]

    def group(g, _):
        for p in range(PEER_GROUP // 2):
            pair(g, p)
        return 0

    lax.fori_loop(0, tm // PEER_GROUP, group, 0)
    y = jnp.concatenate([yf_ref[pl.ds(c, tm, stride=SUBLANES), :] for c in range(SUBLANES)], axis=1)
    gate_f = mod_ref[0, :, 5 * D:6 * D]
    o_ref[...] = x1_ref[...] + gate_f * _rms(y, g_ref[...])


def _peer_specs(tab):
    tm = PEER_TM
    smem = [pl.BlockSpec((None, tm // PEER_GROUP, PEER_E), functools.partial(lambda k, i: (k, i, 0), k),
                         memory_space=pltpu.SMEM) for k in range(PEER_GROUP)]
    row = pl.BlockSpec((tm, PEER_E), lambda i: (i, 0))
    const = lambda shape: pl.BlockSpec(shape, lambda i: (0,) * len(shape))
    table = pl.BlockSpec(tab.shape, lambda i: (0, 0), pipeline_mode=pl.Buffered(1))
    return tm, smem, row, const, table


def _peer_u(off, par, h2, gate, tab):
    T, D = h2.shape
    tm, smem, row, const, table = _peer_specs(tab)
    expand, collapse, fold, half = _peer_consts()
    return pl.pallas_call(
        _peer_u_kernel,
        grid=(T // tm,),
        in_specs=smem + [row, pl.BlockSpec((tm, D), lambda i: (i, 0)), row,
                  const(expand.shape), const(collapse.shape), const(fold.shape), const(half.shape), table],
        out_specs=row,
        out_shape=jax.ShapeDtypeStruct((T, PEER_E), F32),
        scratch_shapes=[pltpu.VMEM((tm, STACK), F32), pltpu.VMEM((tm, STACK), F32),
                        pltpu.VMEM((tm * SUBLANES, LANES), F32)],
        compiler_params=_cparams(1),
        name="peer_u",
    )(*([off] * PEER_GROUP), par, h2, gate, expand, collapse, fold, half, tab)


def _peer_v(off, par, act, tab, x1, mod3, g, S):
    T, D = x1.shape
    tm, smem, row, const, table = _peer_specs(tab)
    tpb = S // tm
    expand, _, fold, half = _peer_consts()
    wide = pl.BlockSpec((tm, D), lambda i: (i, 0))
    return pl.pallas_call(
        _peer_v_kernel,
        grid=(T // tm,),
        in_specs=smem + [row, row, const(expand.shape), const(fold.shape), const(half.shape), table,
                  wide, pl.BlockSpec((1, 1, N_MOD * D), lambda i: (i // tpb, 0, 0)), const((1, D))],
        out_specs=wide,
        out_shape=jax.ShapeDtypeStruct((T, D), F32),
        scratch_shapes=[pltpu.VMEM((tm, STACK), F32), pltpu.VMEM((tm * SUBLANES, LANES), F32)],
        compiler_params=_cparams(1),
        name="peer_v",
    )(*([off] * PEER_GROUP), par, act, expand, fold, half, tab, x1, mod3, g)


def kernel(x, c, w_ada, b_ada, pre_mix_g, post_mix_g, w_in, b_fox_f, hgrn_gamma, hgrn_norm_g, w_out, pre_ffn_g,
           post_ffn_g, peer_w_q, peer_sub_keys, peer_u, peer_v):
    B, S, D = x.shape
    T = B * S
    depth = w_in.shape[0]
    assert depth == 1, "single-layer block"
    l = 0
    x2d = x.reshape(T, D)

    mod3 = _ada(c, w_ada[l], b_ada[l]).reshape(B, 1, N_MOD * D)

    w = w_in[l]
    o1 = 3 * FOX_WIDTH
    wqkv = w[:, :o1].astype(BF16)
    wf = jnp.pad(w[:, o1:o1 + FOX_HEADS], ((0, 0), (0, LANES - FOX_HEADS))).astype(BF16)
    bf = jnp.pad(b_fox_f[l].astype(F32), (0, LANES - FOX_HEADS)).reshape(1, LANES)
    wh = w[:, o1 + FOX_HEADS:].astype(BF16)

    qt, ka, vt, hq, hf, hi, hg = _proj(x2d, mod3, pre_mix_g[l].reshape(1, D), wqkv, wf, bf, wh, B, S)

    o_fox = _fox(qt, ka, vt)

    o_h = _hgrn(hq, hf, hi, hg, hgrn_gamma[l:l + 2].astype(F32), hgrn_norm_g[l].reshape(1, HGRN_WIDTH), B, S)

    wo = w_out[l].astype(BF16)
    wfo = wo[:FOX_WIDTH].reshape(FOX_HEADS, FOX_HEAD_DIM, D)
    who = wo[FOX_WIDTH:]
    keys = peer_sub_keys[l].reshape(2 * PEER_HEADS, N_KEYS, PEER_HALF).astype(BF16)
    x1, h2, scores = _outproj(x2d, o_fox, o_h, mod3, wfo, who, post_mix_g[l].reshape(1, D),
                              pre_ffn_g[l].reshape(1, D), peer_w_q[l].astype(BF16), keys, B, S)

    off, par, gate = _topk(scores)

    act = _peer_u(off, par, h2, gate, _pack_table(peer_u[l]))
    out = _peer_v(off, par, act, _pack_table(peer_v[l]), x1, mod3, post_ffn_g[l].reshape(1, D), S)
    return out.reshape(B, S, D)
```

```python
import functools
import math

import jax
import jax.numpy as jnp
from jax import lax
from jax.experimental import pallas as pl
from jax.experimental.pallas import tpu as pltpu

F32 = jnp.float32
BF16 = jnp.bfloat16
EPS = 1e-6
HIGHEST = lax.Precision.HIGHEST
LOG2E = math.log2(math.e)

FOX_HEADS = 8
FOX_HEAD_DIM = 64
FOX_WIDTH = FOX_HEADS * FOX_HEAD_DIM
HGRN_HEADS = 4
HGRN_HEAD_DIM = 128
HGRN_WIDTH = HGRN_HEADS * HGRN_HEAD_DIM
HGRN_CHUNK = 16
PEER_HEADS = 8
PEER_QUERY_DIM = 256
PEER_HALF = PEER_QUERY_DIM // 2
N_KEYS = 128
PEER_TOPK = 16
PEER_E = PEER_HEADS * PEER_TOPK
PEER_GROUP = 32
N_MOD = 6

LANES = 128
SUBLANES = 8
VMEM_LIMIT = 56 * 1024 * 1024


def _cparams(n_axes, vmem=VMEM_LIMIT):
    return pltpu.CompilerParams(dimension_semantics=("arbitrary",) * n_axes, vmem_limit_bytes=vmem)


def _dot(a, b):
    return jnp.dot(a, b, preferred_element_type=F32)


def _dot_nt(a, b):
    return lax.dot_general(a, b, (((1,), (1,)), ((), ())), preferred_element_type=F32)


def _dot_tn(a, b):
    return lax.dot_general(a, b, (((0,), (0,)), ((), ())), preferred_element_type=F32)


def _rms(x, g):
    return x * lax.rsqrt(jnp.mean(x * x, axis=-1, keepdims=True) + EPS) * g


def _ada_kernel(c_ref, w_ref, b_ref, o_ref):
    c = c_ref[...]
    cond = c * jax.nn.sigmoid(c)
    o_ref[...] = jnp.dot(cond, w_ref[...], preferred_element_type=F32, precision=HIGHEST) + b_ref[...]


def _ada(c, w, b):
    B, D = c.shape
    N = w.shape[1]
    tn = 1024
    return pl.pallas_call(
        _ada_kernel,
        grid=(N // tn,),
        in_specs=[pl.BlockSpec((B, D), lambda j: (0, 0)),
                  pl.BlockSpec((D, tn), lambda j: (0, j)),
                  pl.BlockSpec((1, tn), lambda j: (0, j))],
        out_specs=pl.BlockSpec((B, tn), lambda j: (0, j)),
        out_shape=jax.ShapeDtypeStruct((B, N), F32),
        compiler_params=_cparams(1),
        name="ada",
    )(c, w, b.reshape(1, N))


def _split3(f):
    hi = f.astype(BF16).astype(F32)
    r = f - hi
    mid = r.astype(BF16).astype(F32)
    return hi, mid, r - mid


def _proj_kernel(x_ref, mod_ref, g_ref, wqkv_ref, wf_ref, bf_ref, wh_ref, tri_ref,
                 qt_ref, ka_ref, vt_ref, hq_ref, hf_ref, hi_ref, hg_ref, carry_ref,
                 *, tiles_per_batch):
    i = pl.program_id(0)
    D = x_ref.shape[1]
    tm = x_ref.shape[0]
    Dh = FOX_HEAD_DIM
    x = x_ref[...]
    shift = mod_ref[0, :, 0:D]
    scale = mod_ref[0, :, D:2 * D]
    h = (_rms(x, g_ref[...]) * (1.0 + scale) + shift).astype(BF16)

    ff = _dot(h, wf_ref[...]) + bf_ref[...]
    logf = jnp.minimum(ff, 0.0) - jnp.log(1.0 + jnp.exp(-jnp.abs(ff)))

    @pl.when(i % tiles_per_batch == 0)
    def _():
        carry_ref[...] = jnp.zeros_like(carry_ref)

    cs = jnp.dot(tri_ref[...], logf, preferred_element_type=F32, precision=HIGHEST) + carry_ref[...]
    carry_ref[...] = cs[tm - 1:tm, :]

    qkv = _dot(h, wqkv_ref[...])
    lane = lax.broadcasted_iota(jnp.int32, (tm, Dh), 1)
    zpad = jnp.zeros((tm, Dh), F32)
    for hd in range(FOX_HEADS):
        lo = hd * Dh
        fhi, fmid, flo = _split3(cs[:, hd:hd + 1] * LOG2E)
        pieces = lambda o: jnp.where(lane == o, fhi, jnp.where(lane == o + 1, fmid, jnp.where(lane == o + 2, flo, 0.0)))
        q_aux = jnp.where(lane < 3, -1.0, pieces(3))
        k_aux = jnp.where((lane >= 3) & (lane < 6), 1.0, pieces(0))
        qa = jnp.concatenate([qkv[:, lo:lo + Dh] * (Dh ** -0.5 * LOG2E), q_aux], axis=1)
        ka = jnp.concatenate([qkv[:, FOX_WIDTH + lo:FOX_WIDTH + lo + Dh], k_aux], axis=1)
        va = jnp.concatenate([qkv[:, 2 * FOX_WIDTH + lo:2 * FOX_WIDTH + lo + Dh], zpad], axis=1)
        qt_ref[0, hd] = qa.T.astype(BF16)
        ka_ref[0, hd] = ka.astype(BF16)
        vt_ref[0, hd] = va.T[0:Dh, :].astype(BF16)

    hh = _dot(h, wh_ref[...])
    hq_ref[...] = hh[:, 0:HGRN_WIDTH]
    hf_ref[...] = hh[:, HGRN_WIDTH:2 * HGRN_WIDTH]
    hi_ref[...] = hh[:, 2 * HGRN_WIDTH:3 * HGRN_WIDTH]
    hg_ref[...] = hh[:, 3 * HGRN_WIDTH:4 * HGRN_WIDTH]


def _proj(x2d, mod3, g, wqkv, wf, bf, wh, B, S):
    T, D = x2d.shape
    tm = 256
    tpb = S // tm
    tri = (jnp.arange(tm)[:, None] >= jnp.arange(tm)[None, :]).astype(F32)
    H, Dh = FOX_HEADS, FOX_HEAD_DIM
    fox_shapes = [jax.ShapeDtypeStruct((B, H, LANES, S), BF16), jax.ShapeDtypeStruct((B, H, S, LANES), BF16),
                  jax.ShapeDtypeStruct((B, H, Dh, S), BF16)]
    fox_specs = [pl.BlockSpec((1, H, LANES, tm), lambda i: (i // tpb, 0, 0, i % tpb)),
                 pl.BlockSpec((1, H, tm, LANES), lambda i: (i // tpb, 0, i % tpb, 0)),
                 pl.BlockSpec((1, H, Dh, tm), lambda i: (i // tpb, 0, 0, i % tpb))]
    wide = jax.ShapeDtypeStruct((T, HGRN_WIDTH), F32)
    wspec = pl.BlockSpec((tm, HGRN_WIDTH), lambda i: (i, 0))
    const = lambda shape: pl.BlockSpec(shape, lambda i: (0,) * len(shape))
    return pl.pallas_call(
        functools.partial(_proj_kernel, tiles_per_batch=tpb),
        grid=(T // tm,),
        in_specs=[pl.BlockSpec((tm, D), lambda i: (i, 0)),
                  pl.BlockSpec((1, 1, N_MOD * D), lambda i: (i // tpb, 0, 0)),
                  const((1, D)), const(wqkv.shape), const(wf.shape), const(bf.shape), const(wh.shape),
                  const((tm, tm))],
        out_specs=fox_specs + [wspec, wspec, wspec, wspec],
        out_shape=fox_shapes + [wide, wide, wide, wide],
        scratch_shapes=[pltpu.VMEM((1, LANES), F32)],
        compiler_params=_cparams(1),
        name="proj",
    )(x2d, mod3, g, wqkv, wf, bf, wh, tri)


FOX_HB = 8


def _fox_kernel(qt_ref, ka_ref, vt_ref, o_ref, m_ref, l_ref, acc_ref, *, tq):
    qi = pl.program_id(2)
    m_ref[...] = jnp.full_like(m_ref, -jnp.inf)
    l_ref[...] = jnp.zeros_like(l_ref)
    acc_ref[...] = jnp.zeros_like(acc_ref)

    def step(j, masked):
        off = pl.multiple_of(j * tq, tq)
        hs = range(FOX_HB)
        s = [_dot(ka_ref[0, hh, pl.ds(off, tq), :], qt_ref[0, hh]) for hh in hs]
        if masked:
            key = lax.broadcasted_iota(jnp.int32, (tq, tq), 0)
            qry = lax.broadcasted_iota(jnp.int32, (tq, tq), 1)
            s = [jnp.where(key <= qry, sh, -jnp.inf) for sh in s]
        m_old = [m_ref[hh] for hh in hs]
        m_new = [jnp.maximum(m_old[hh], jnp.max(s[hh], axis=0, keepdims=True)) for hh in hs]
        p = [jnp.exp2(s[hh] - m_new[hh]) for hh in hs]
        alpha = [jnp.exp2(m_old[hh] - m_new[hh]) for hh in hs]
        pv = [_dot(vt_ref[0, hh, :, pl.ds(off, tq)], p[hh].astype(BF16)) for hh in hs]
        for hh in hs:
            l_ref[hh] = alpha[hh] * l_ref[hh] + jnp.sum(p[hh], axis=0, keepdims=True)
            acc_ref[hh] = alpha[hh] * acc_ref[hh] + pv[hh]
            m_ref[hh] = m_new[hh]

    step(qi, True)

    def body(j, _):
        step(j, False)
        return 0

    lax.fori_loop(0, qi, body, 0)
    for hh in range(FOX_HB):
        o_ref[0, hh] = (acc_ref[hh] / l_ref[hh]).T.astype(o_ref.dtype)


def _fox(qt, ka, vt):
    B, H, S, _ = ka.shape
    Dh = vt.shape[2]
    tq = 512
    hb = FOX_HB
    return pl.pallas_call(
        functools.partial(_fox_kernel, tq=tq),
        grid=(B, H // hb, S // tq),
        in_specs=[pl.BlockSpec((1, hb, LANES, tq), lambda b, h, i: (b, h, 0, i)),
                  pl.BlockSpec((1, hb, S, LANES), lambda b, h, i: (b, h, 0, 0)),
                  pl.BlockSpec((1, hb, Dh, S), lambda b, h, i: (b, h, 0, 0))],
        out_specs=pl.BlockSpec((1, hb, tq, Dh), lambda b, h, i: (b, h, i, 0)),
        out_shape=jax.ShapeDtypeStruct((B, H, S, Dh), BF16),
        scratch_shapes=[pltpu.VMEM((hb, 1, tq), F32), pltpu.VMEM((hb, 1, tq), F32), pltpu.VMEM((hb, Dh, tq), F32)],
        compiler_params=_cparams(3),
        name="fox",
    )(qt, ka, vt)


def _hgrn_kernel(hq_ref, hf_ref, hi_ref, hg_ref, gam_ref, ng_ref, bd_ref, bl_ref, o_ref, st_ref, os_ref):
    j = pl.program_id(1)
    tb = hq_ref.shape[0]
    C = HGRN_CHUNK
    Dk = HGRN_HEAD_DIM

    @pl.when(j == 0)
    def _():
        st_ref[...] = jnp.zeros_like(st_ref)

    gam = gam_ref[...]
    e = jnp.exp(gam - jnp.max(gam, axis=0, keepdims=True))
    lb = e[0:1, :] / jnp.sum(e, axis=0, keepdims=True)
    f = lb + (1.0 - lb) * jax.nn.sigmoid(hf_ref[...])
    lf = jnp.log(f)
    A = jnp.dot(bd_ref[...], lf, preferred_element_type=F32, precision=HIGHEST)
    AL = jnp.dot(bl_ref[...], lf, preferred_element_type=F32, precision=HIGHEST)
    q = hq_ref[...] * (Dk ** -0.5)
    kk = 1.0 - f
    qa = q * jnp.exp(A)
    ka = kk * jnp.exp(AL - A)
    eal = jnp.exp(AL)
    inp = hi_ref[...]

    tri3 = (lax.broadcasted_iota(jnp.int32, (C, C, Dk), 0) >= lax.broadcasted_iota(jnp.int32, (C, C, Dk), 1))
    for hd in range(HGRN_HEADS):
        lo = hd * Dk
        st = st_ref[hd]
        for c in range(tb // C):
            r0 = c * C
            Ac = A[r0:r0 + C, lo:lo + Dk]
            qc = q[r0:r0 + C, lo:lo + Dk]
            kc = kk[r0:r0 + C, lo:lo + Dk]
            ic = inp[r0:r0 + C, lo:lo + Dk]
            diff = Ac[:, None, :] - Ac[None, :, :]
            dec = jnp.exp(jnp.where(tri3, diff, -jnp.inf))
            sc = jnp.sum(qc[:, None, :] * kc[None, :, :] * dec, axis=-1)
            o = _dot_nt(qa[r0:r0 + C, lo:lo + Dk], st) + _dot(sc, ic)
            st = st * eal[r0:r0 + 1, lo:lo + Dk] + _dot_tn(ic, ka[r0:r0 + C, lo:lo + Dk])
            os_ref[r0:r0 + C, lo:lo + Dk] = o
        st_ref[hd] = st

    o_all = os_ref[...]
    hg = hg_ref[...]
    ng = ng_ref[...]
    for hd in range(HGRN_HEADS):
        lo = hd * Dk
        oh = _rms(o_all[:, lo:lo + Dk], ng[:, lo:lo + Dk])
        g = hg[:, lo:lo + Dk]
        o_ref[:, lo:lo + Dk] = (oh * (g * jax.nn.sigmoid(g))).astype(o_ref.dtype)


def _hgrn(hq, hf, hi, hg, gamma, norm_g, B, S):
    T, W = hq.shape
    tb = 128
    nb = S // tb
    t = jnp.arange(tb)
    same = (t[:, None] // HGRN_CHUNK) == (t[None, :] // HGRN_CHUNK)
    bd = (same & (t[:, None] >= t[None, :])).astype(F32)
    bl = same.astype(F32)
    wspec = pl.BlockSpec((tb, W), lambda b, j: (b * nb + j, 0))
    const = lambda shape: pl.BlockSpec(shape, lambda b, j: (0,) * len(shape))
    return pl.pallas_call(
        _hgrn_kernel,
        grid=(B, nb),
        in_specs=[wspec, wspec, wspec, wspec, const(gamma.shape), const((1, W)), const((tb, tb)), const((tb, tb))],
        out_specs=wspec,
        out_shape=jax.ShapeDtypeStruct((T, W), BF16),
        scratch_shapes=[pltpu.VMEM((HGRN_HEADS, HGRN_HEAD_DIM, HGRN_HEAD_DIM), F32), pltpu.VMEM((tb, W), F32)],
        compiler_params=_cparams(2),
        name="hgrn",
    )(hq, hf, hi, hg, gamma, norm_g, bd, bl)


def _outproj_kernel(x_ref, of_ref, oh_ref, mod_ref, wfo_ref, who_ref, pmg_ref, pfg_ref, wq_ref, keys_ref,
                    x1_ref, h2_ref, sc_ref):
    D = x_ref.shape[1]
    mix = _dot(oh_ref[...], who_ref[...])
    for hd in range(FOX_HEADS):
        mix = mix + _dot(of_ref[0, hd], wfo_ref[hd])
    gate_a = mod_ref[0, :, 2 * D:3 * D]
    shift_f = mod_ref[0, :, 3 * D:4 * D]
    scale_f = mod_ref[0, :, 4 * D:5 * D]
    x1 = x_ref[...] + gate_a * _rms(mix, pmg_ref[...])
    x1_ref[...] = x1
    h2 = _rms(x1, pfg_ref[...]) * (1.0 + scale_f) + shift_f
    h2_ref[...] = h2
    qp = _dot(h2.astype(BF16), wq_ref[...])
    for g in range(2 * PEER_HEADS):
        qg = qp[:, g * PEER_HALF:(g + 1) * PEER_HALF]
        qn = qg * lax.rsqrt(jnp.mean(qg * qg, axis=-1, keepdims=True) + EPS)
        sc_ref[g] = _dot_nt(keys_ref[g], qn.astype(BF16))


def _outproj(x2d, ofox, oh, mod3, wfo, who, pmg, pfg, wq, keys, B, S):
    T, D = x2d.shape
    tm = 256
    tpb = S // tm
    G = 2 * PEER_HEADS
    const = lambda shape: pl.BlockSpec(shape, lambda i: (0,) * len(shape))
    row = pl.BlockSpec((tm, D), lambda i: (i, 0))
    return pl.pallas_call(
        _outproj_kernel,
        grid=(T // tm,),
        in_specs=[row,
                  pl.BlockSpec((1, FOX_HEADS, tm, FOX_HEAD_DIM), lambda i: (i // tpb, 0, i % tpb, 0)),
                  pl.BlockSpec((tm, HGRN_WIDTH), lambda i: (i, 0)),
                  pl.BlockSpec((1, 1, N_MOD * D), lambda i: (i // tpb, 0, 0)),
                  const(wfo.shape), const(who.shape), const((1, D)), const((1, D)), const(wq.shape), const(keys.shape)],
        out_specs=[row, row, pl.BlockSpec((G, N_KEYS, tm), lambda i: (0, 0, i))],
        out_shape=[jax.ShapeDtypeStruct((T, D), F32), jax.ShapeDtypeStruct((T, D), F32),
                   jax.ShapeDtypeStruct((G, N_KEYS, T), F32)],
        compiler_params=_cparams(1),
        name="outproj",
    )(x2d, ofox, oh, mod3, wfo, who, pmg, pfg, wq, keys)


TOPK_TM = SUBLANES * LANES


def _cx(a, b, desc):
    (va, ia), (vb, ib) = a, b
    swap = (vb > va) if desc else (vb < va)
    return ((jnp.where(swap, vb, va), jnp.where(swap, ib, ia)),
            (jnp.where(swap, va, vb), jnp.where(swap, ia, ib)))


def _bitonic_merge(x, desc):
    n = len(x)
    x = list(x)
    d = n // 2
    while d >= 1:
        for i in range(n):
            if (i // d) % 2 == 0:
                x[i], x[i + d] = _cx(x[i], x[i + d], desc)
        d //= 2
    return x


def _bitonic_sort(x, desc):
    n = len(x)
    if n == 1:
        return list(x)
    lo = _bitonic_sort(x[:n // 2], True)
    hi = _bitonic_sort(x[n // 2:], False)
    return _bitonic_merge(lo + hi, desc)


def _tile(ref, row):
    return ref[pl.ds(pl.multiple_of(row * SUBLANES, SUBLANES), SUBLANES), :]


def _put(ref, row, x):
    ref[pl.ds(pl.multiple_of(row * SUBLANES, SUBLANES), SUBLANES), :] = x


def _top16_network(s_ref, n, p_ref, va_ref, ia_ref, vb_ref, ib_ref):
    K = PEER_TOPK
    shape = (SUBLANES, LANES)

    def store_block(v_ref, i_ref, blk, pairs):
        rev = blk % 2
        for k, (v, i) in enumerate(pairs):
            pos = blk * K + k + rev * (K - 1 - 2 * k)
            _put(v_ref, pos, v)
            _put(i_ref, pos, i)

    def sort_block(blk, c):
        tag = (lambda r: jnp.full(shape, r, jnp.int32)) if p_ref is None else (lambda r: _tile(p_ref, r))
        pairs = [(_tile(s_ref, blk * K + k), tag(blk * K + k)) for k in range(K)]
        store_block(va_ref, ia_ref, blk, _bitonic_sort(pairs, True))
        return c

    lax.fori_loop(0, n // K, sort_block, 0)

    def merge_round(src_v, src_i, dst_v, dst_i, n_pairs):
        def merge(p, c):
            top = []
            for k in range(K):
                va, ia = _tile(src_v, 2 * p * K + k), _tile(src_i, 2 * p * K + k)
                vb, ib = _tile(src_v, (2 * p + 1) * K + k), _tile(src_i, (2 * p + 1) * K + k)
                take_b = vb > va
                top.append((jnp.where(take_b, vb, va), jnp.where(take_b, ib, ia)))
            store_block(dst_v, dst_i, p, _bitonic_merge(top, True))
            return c
        lax.fori_loop(0, n_pairs, merge, 0)

    src, dst = (va_ref, ia_ref), (vb_ref, ib_ref)
    blocks = n // K
    while blocks > 1:
        merge_round(*src, *dst, blocks // 2)
        src, dst = dst, src
        blocks //= 2
    return ([src[0][k * SUBLANES:(k + 1) * SUBLANES, :] for k in range(K)],
            [src[1][k * SUBLANES:(k + 1) * SUBLANES, :] for k in range(K)])


def _untied(s_ref, n, vals):
    ok = vals[0] > vals[1]
    for a in range(1, PEER_TOPK - 1):
        ok = ok & (vals[a] > vals[a + 1])
    last = vals[-1]

    def count(c, cnt):
        for k in range(SUBLANES):
            cnt = cnt + jnp.where(_tile(s_ref, c * SUBLANES + k) >= last, 1, 0)
        return cnt
    cnt = lax.fori_loop(0, n // SUBLANES, count, jnp.zeros(last.shape, jnp.int32))
    return ok & (cnt == PEER_TOPK)


def _top16_exact_loop(s_ref, n, p_ref):
    shape = (SUBLANES, LANES)
    vals, picks = [], []
    for _ in range(PEER_TOPK):
        m = lax.fori_loop(0, n, lambda k, m: jnp.maximum(m, _tile(s_ref, k)), jnp.full(shape, -jnp.inf, F32))
        ix = lax.fori_loop(0, n, lambda k, ix: jnp.minimum(ix, jnp.where(_tile(s_ref, k) == m, k, n)),
                           jnp.full(shape, n, jnp.int32))

        def remove(k, pick):
            hit = ix == k
            _put(s_ref, k, jnp.where(hit, -jnp.inf, _tile(s_ref, k)))
            return pick if p_ref is None else pick + jnp.where(hit, _tile(p_ref, k), 0)
        pick = lax.fori_loop(0, n, remove, jnp.zeros(shape, jnp.int32))
        vals.append(m)
        picks.append(ix if p_ref is None else pick)
    return vals, picks


def _top16(s_ref, n, p_ref, bufs):
    vals, tags = _top16_network(s_ref, n, p_ref, *bufs)
    ok = jnp.min(jnp.where(_untied(s_ref, n, vals), 1, 0)) > 0
    pack = lambda v, i: (jnp.stack(v), jnp.stack(i))
    v, i = lax.cond(ok, lambda: pack(vals, tags), lambda: pack(*_top16_exact_loop(s_ref, n, p_ref)))
    return [v[a] for a in range(PEER_TOPK)], [i[a] for a in range(PEER_TOPK)]


N_CAND = 64


def _topk_kernel(sc_ref, off_ref, par_ref, gate_ref, s_ref, p_ref, va_ref, ia_ref, vb_ref, ib_ref, e_ref):
    K = PEER_TOPK
    tm = TOPK_TM
    bufs = (va_ref, ia_ref, vb_ref, ib_ref)
    shape = (SUBLANES, LANES)

    def half_topk(g):
        x = sc_ref[g]
        blocks = [x[:, s * LANES:(s + 1) * LANES].reshape(N_KEYS // SUBLANES, SUBLANES, LANES)
                  for s in range(SUBLANES)]
        s_ref[...] = pltpu.einshape("gskl->gksl", jnp.stack(blocks, axis=1)).reshape(N_KEYS * SUBLANES, LANES)
        return _top16(s_ref, N_KEYS, None, bufs)

    for hd in range(PEER_HEADS):
        s1, i1 = half_topk(2 * hd)
        s2, i2 = half_topk(2 * hd + 1)
        r = 0
        for a in range(K):
            for b in range(K // (a + 1)):
                _put(s_ref, r, s1[a] + s2[b])
                _put(p_ref, r, i1[a] * N_KEYS + i2[b])
                r += 1
        for r in range(r, N_CAND):
            _put(s_ref, r, jnp.full(shape, -jnp.inf, F32))
            _put(p_ref, r, jnp.zeros(shape, jnp.int32))
        best, eidx = _top16(s_ref, N_CAND, p_ref, bufs)
        p = [jnp.exp(bv - best[0]) for bv in best]
        tot = functools.reduce(jnp.add, p)
        for a in range(K):
            e = hd * K + a
            e_ref[0, e * SUBLANES:(e + 1) * SUBLANES, :] = eidx[a].astype(F32)
            e_ref[1, e * SUBLANES:(e + 1) * SUBLANES, :] = p[a] / tot

    def token_major(x):
        t = pltpu.einshape("gesl->gsel", x.reshape(PEER_E // SUBLANES, SUBLANES, SUBLANES, LANES))
        return jnp.concatenate([t[:, s].reshape(PEER_E, LANES).T for s in range(SUBLANES)], axis=0)

    n_half = N_KEYS * N_KEYS // 2
    idx = token_major(e_ref[0]).astype(jnp.int32)
    gate_ref[...] = token_major(e_ref[1])
    par_ref[...] = (idx >= n_half).astype(F32)
    off = (idx & (n_half - 1)) * SUBLANES
    off_ref[...] = pltpu.einshape("abl->bal", off.reshape(tm // PEER_GROUP, PEER_GROUP, PEER_E))


def _topk(scores):
    G, NK, T = scores.shape
    tm = TOPK_TM
    out = pl.BlockSpec((tm, PEER_E), lambda i: (i, 0))
    return pl.pallas_call(
        _topk_kernel,
        grid=(T // tm,),
        in_specs=[pl.BlockSpec((G, NK, tm), lambda i: (0, 0, i))],
        out_specs=[pl.BlockSpec((PEER_GROUP, tm // PEER_GROUP, PEER_E), lambda i: (0, i, 0)), out, out],
        out_shape=[jax.ShapeDtypeStruct((PEER_GROUP, T // PEER_GROUP, PEER_E), jnp.int32),
                   jax.ShapeDtypeStruct((T, PEER_E), F32), jax.ShapeDtypeStruct((T, PEER_E), F32)],
        scratch_shapes=[pltpu.VMEM((N_KEYS * SUBLANES, LANES), F32), pltpu.VMEM((N_CAND * SUBLANES, LANES), jnp.int32),
                        pltpu.VMEM((N_KEYS * SUBLANES, LANES), F32), pltpu.VMEM((N_KEYS * SUBLANES, LANES), jnp.int32),
                        pltpu.VMEM((N_KEYS * SUBLANES // 2, LANES), F32),
                        pltpu.VMEM((N_KEYS * SUBLANES // 2, LANES), jnp.int32),
                        pltpu.VMEM((2, PEER_E * SUBLANES, LANES), F32)],
        compiler_params=_cparams(1),
        name="topk",
    )(scores)


PEER_TM = 8 * PEER_GROUP
TILE_ROWS = 2 * SUBLANES
STACK = PEER_E * TILE_ROWS


def _pack_kernel(lo_ref, hi_ref, o_ref):
    nb = lo_ref.shape[0]
    lo = lax.bitcast_convert_type(lo_ref[...].astype(BF16).astype(F32), jnp.uint32)
    hi = lax.bitcast_convert_type(hi_ref[...].astype(BF16).astype(F32), jnp.uint32)
    word = (hi & jnp.uint32(0xFFFF0000)) | lax.shift_right_logical(lo, jnp.uint32(16))
    for s in range(SUBLANES):
        o_ref[pl.ds(s, nb, stride=SUBLANES), :] = word[:, s * LANES:(s + 1) * LANES]


def _pack_table(w):
    n, d = w.shape
    nb = 256
    steps = n // 2 // nb
    return pl.pallas_call(
        _pack_kernel,
        grid=(steps,),
        in_specs=[pl.BlockSpec((nb, d), lambda i: (i, 0)), pl.BlockSpec((nb, d), lambda i: (i + steps, 0))],
        out_specs=pl.BlockSpec((nb * SUBLANES, LANES), lambda i: (i, 0)),
        out_shape=jax.ShapeDtypeStruct((n // 2 * SUBLANES, LANES), jnp.uint32),
        compiler_params=_cparams(1),
        name="pack",
    )(w, w)


def _peer_consts():
    c = jnp.arange(STACK)
    expand = (c[None, :] // TILE_ROWS == jnp.arange(PEER_E)[:, None])
    fold = ((c[None, :] % TILE_ROWS) // 2 == jnp.arange(SUBLANES)[:, None])
    half = (c % 2).reshape(1, STACK)
    return expand.astype(BF16), expand.T.astype(BF16), fold.astype(F32), half.astype(F32)


def _stack2(tab_ref, off_a, off_b, g):
    def tile(o):
        return pltpu.bitcast(tab_ref[pl.ds(pl.multiple_of(o, SUBLANES), SUBLANES), :], BF16)
    w0 = jnp.concatenate([tile(off_a[g, j]) for j in range(PEER_E)], axis=0)
    w1 = jnp.concatenate([tile(off_b[g, j]) for j in range(PEER_E)], axis=0)
    return jnp.concatenate([w0, w1], axis=1)


def _split2_dot(x, w):
    hi = x.astype(BF16)
    lo = (x - hi.astype(F32)).astype(BF16)
    return _dot(hi, w) + _dot(lo, w)


def _expand_sel(par_ref, expand_ref, half_ref):
    par = _dot(par_ref[...].astype(BF16), expand_ref[...])
    return jnp.where(par == half_ref[...], 1.0, 0.0)


def _fold_rows(x):
    tm = x.shape[0]
    chunks = [x[:, c * LANES:(c + 1) * LANES].reshape(tm // SUBLANES, SUBLANES, LANES) for c in range(SUBLANES)]
    return pltpu.einshape("gctl->gtcl", jnp.stack(chunks, axis=1)).reshape(tm * SUBLANES, LANES)


def _unfold_rows(xf):
    tm = xf.shape[0] // SUBLANES
    t = pltpu.einshape("gtcl->gctl", xf.reshape(tm // SUBLANES, SUBLANES, SUBLANES, LANES))
    return jnp.concatenate([t[:, c].reshape(tm, LANES) for c in range(SUBLANES)], axis=1)


def _peer_u_kernel(*refs):
    offs = refs[:PEER_GROUP]
    (par_ref, x_ref, gate_ref, expand_ref, collapse_ref, fold_ref, half_ref, tab_ref,
     act_ref, sel_ref, g_ref, xf_ref) = refs[PEER_GROUP:]
    tm = x_ref.shape[0]
    sel_ref[...] = _expand_sel(par_ref, expand_ref, half_ref)
    xf_ref[...] = _fold_rows(x_ref[...])
    fold = fold_ref[...]
    zero = jnp.zeros((SUBLANES, LANES), BF16)

    def pair(g, p):
        t = PEER_GROUP * g + 2 * p
        w2 = _stack2(tab_ref, offs[2 * p], offs[2 * p + 1], g)
        x0 = xf_ref[pl.ds(pl.multiple_of(t * SUBLANES, SUBLANES), SUBLANES), :].astype(BF16)
        x1 = xf_ref[pl.ds(pl.multiple_of((t + 1) * SUBLANES, SUBLANES), SUBLANES), :].astype(BF16)
        lhs = jnp.concatenate([jnp.concatenate([x0, zero], axis=1),
                               jnp.concatenate([zero, x1], axis=1)], axis=0)
        gm = _dot_nt(lhs, w2)
        g_ref[pl.ds(t, 1), :] = jnp.sum(gm[0:SUBLANES] * fold, axis=0, keepdims=True) * sel_ref[pl.ds(t, 1), :]
        g_ref[pl.ds(t + 1, 1), :] = (jnp.sum(gm[SUBLANES:] * fold, axis=0, keepdims=True)
                                     * sel_ref[pl.ds(t + 1, 1), :])

    def group(g, _):
        for p in range(PEER_GROUP // 2):
            pair(g, p)
        return 0

    lax.fori_loop(0, tm // PEER_GROUP, group, 0)
    a = _split2_dot(g_ref[...], collapse_ref[...])
    gelu = 0.5 * a * (1.0 + lax.erf(a * (2.0 ** -0.5)))
    act_ref[...] = gate_ref[...] * gelu


def _peer_v_kernel(*refs):
    offs = refs[:PEER_GROUP]
    (par_ref, act_ref, expand_ref, fold_ref, half_ref, tab_ref, x1_ref, mod_ref, g_ref,
     o_ref, ce_ref, yf_ref) = refs[PEER_GROUP:]
    tm, D = x1_ref.shape
    ce_ref[...] = _dot(act_ref[...].astype(BF16), expand_ref[...]) * _expand_sel(par_ref, expand_ref, half_ref)
    fold = fold_ref[...]

    def pair(g, p):
        t = PEER_GROUP * g + 2 * p
        w2 = _stack2(tab_ref, offs[2 * p], offs[2 * p + 1], g)
        coef = jnp.concatenate([(ce_ref[pl.ds(t, 1), :] * fold).astype(BF16),
                                (ce_ref[pl.ds(t + 1, 1), :] * fold).astype(BF16)], axis=0)
        out = _dot(coef, w2)
        yf_ref[pl.ds(pl.multiple_of(t * SUBLANES, SUBLANES), SUBLANES), :] = out[0:SUBLANES, 0:LANES]
        yf_ref[pl.ds(pl.multiple_of((t + 1) * SUBLANES, SUBLANES), SUBLANES), :] = out[SUBLANES:, LANES:]

    def group(g, _):
        for p in range(PEER_GROUP // 2):
            pair(g, p)
        return 0

    lax.fori_loop(0, tm // PEER_GROUP, group, 0)
    y = _unfold_rows(yf_ref[...])
    gate_f = mod_ref[0, :, 5 * D:6 * D]
    o_ref[...] = x1_ref[...] + gate_f * _rms(y, g_ref[...])


def _peer_specs(tab):
    tm = PEER_TM
    smem = [pl.BlockSpec((None, tm // PEER_GROUP, PEER_E), functools.partial(lambda k, i: (k, i, 0), k),
                         memory_space=pltpu.SMEM) for k in range(PEER_GROUP)]
    row = pl.BlockSpec((tm, PEER_E), lambda i: (i, 0))
    const = lambda shape: pl.BlockSpec(shape, lambda i: (0,) * len(shape))
    table = pl.BlockSpec(tab.shape, lambda i: (0, 0), pipeline_mode=pl.Buffered(1))
    return tm, smem, row, const, table


def _peer_u(off, par, h2, gate, tab):
    T, D = h2.shape
    tm, smem, row, const, table = _peer_specs(tab)
    expand, collapse, fold, half = _peer_consts()
    return pl.pallas_call(
        _peer_u_kernel,
        grid=(T // tm,),
        in_specs=smem + [row, pl.BlockSpec((tm, D), lambda i: (i, 0)), row,
                  const(expand.shape), const(collapse.shape), const(fold.shape), const(half.shape), table],
        out_specs=row,
        out_shape=jax.ShapeDtypeStruct((T, PEER_E), F32),
        scratch_shapes=[pltpu.VMEM((tm, STACK), F32), pltpu.VMEM((tm, STACK), F32),
                        pltpu.VMEM((tm * SUBLANES, LANES), F32)],
        compiler_params=_cparams(1),
        name="peer_u",
    )(*([off] * PEER_GROUP), par, h2, gate, expand, collapse, fold, half, tab)


def _peer_v(off, par, act, tab, x1, mod3, g, S):
    T, D = x1.shape
    tm, smem, row, const, table = _peer_specs(tab)
    tpb = S // tm
    expand, _, fold, half = _peer_consts()
    wide = pl.BlockSpec((tm, D), lambda i: (i, 0))
    return pl.pallas_call(
        _peer_v_kernel,
        grid=(T // tm,),
        in_specs=smem + [row, row, const(expand.shape), const(fold.shape), const(half.shape), table,
                  wide, pl.BlockSpec((1, 1, N_MOD * D), lambda i: (i // tpb, 0, 0)), const((1, D))],
        out_specs=wide,
        out_shape=jax.ShapeDtypeStruct((T, D), F32),
        scratch_shapes=[pltpu.VMEM((tm, STACK), F32), pltpu.VMEM((tm * SUBLANES, LANES), F32)],
        compiler_params=_cparams(1),
        name="peer_v",
    )(*([off] * PEER_GROUP), par, act, expand, fold, half, tab, x1, mod3, g)


def kernel(x, c, w_ada, b_ada, pre_mix_g, post_mix_g, w_in, b_fox_f, hgrn_gamma, hgrn_norm_g, w_out, pre_ffn_g,
           post_ffn_g, peer_w_q, peer_sub_keys, peer_u, peer_v):
    B, S, D = x.shape
    T = B * S
    depth = w_in.shape[0]
    assert depth == 1, "single-layer block"
    l = 0
    x2d = x.reshape(T, D)

    mod3 = _ada(c, w_ada[l], b_ada[l]).reshape(B, 1, N_MOD * D)

    w = w_in[l]
    o1 = 3 * FOX_WIDTH
    wqkv = w[:, :o1].astype(BF16)
    wf = jnp.pad(w[:, o1:o1 + FOX_HEADS], ((0, 0), (0, LANES - FOX_HEADS))).astype(BF16)
    bf = jnp.pad(b_fox_f[l].astype(F32), (0, LANES - FOX_HEADS)).reshape(1, LANES)
    wh = w[:, o1 + FOX_HEADS:].astype(BF16)

    qt, ka, vt, hq, hf, hi, hg = _proj(x2d, mod3, pre_mix_g[l].reshape(1, D), wqkv, wf, bf, wh, B, S)

    o_fox = _fox(qt, ka, vt)

    o_h = _hgrn(hq, hf, hi, hg, hgrn_gamma[l:l + 2].astype(F32), hgrn_norm_g[l].reshape(1, HGRN_WIDTH), B, S)

    wo = w_out[l].astype(BF16)
    wfo = wo[:FOX_WIDTH].reshape(FOX_HEADS, FOX_HEAD_DIM, D)
    who = wo[FOX_WIDTH:]
    keys = peer_sub_keys[l].reshape(2 * PEER_HEADS, N_KEYS, PEER_HALF).astype(BF16)
    x1, h2, scores = _outproj(x2d, o_fox, o_h, mod3, wfo, who, post_mix_g[l].reshape(1, D),
                              pre_ffn_g[l].reshape(1, D), peer_w_q[l].astype(BF16), keys, B, S)

    off, par, gate = _topk(scores)

    act = _peer_u(off, par, h2, gate, _pack_table(peer_u[l]))
    out = _peer_v(off, par, act, _pack_table(peer_v[l]), x1, mod3, post_ffn_g[l].reshape(1, D), S)
    return out.reshape(B, S, D)
```

```python
import functools
import math

import jax
import jax.numpy as jnp
from jax import lax
from jax.experimental import pallas as pl
from jax.experimental.pallas import tpu as pltpu

F32 = jnp.float32
BF16 = jnp.bfloat16
EPS = 1e-6
HIGHEST = lax.Precision.HIGHEST
LOG2E = math.log2(math.e)

FOX_HEADS = 8
FOX_HEAD_DIM = 64
FOX_WIDTH = FOX_HEADS * FOX_HEAD_DIM
HGRN_HEADS = 4
HGRN_HEAD_DIM = 128
HGRN_WIDTH = HGRN_HEADS * HGRN_HEAD_DIM
HGRN_CHUNK = 16
PEER_HEADS = 8
PEER_QUERY_DIM = 256
PEER_HALF = PEER_QUERY_DIM // 2
N_KEYS = 128
PEER_TOPK = 16
PEER_E = PEER_HEADS * PEER_TOPK
PEER_GROUP = 32
N_MOD = 6

LANES = 128
SUBLANES = 8
VMEM_LIMIT = 56 * 1024 * 1024


def _cparams(n_axes, vmem=VMEM_LIMIT):
    return pltpu.CompilerParams(dimension_semantics=("arbitrary",) * n_axes, vmem_limit_bytes=vmem)


def _dot(a, b):
    return jnp.dot(a, b, preferred_element_type=F32)


def _dot_nt(a, b):
    return lax.dot_general(a, b, (((1,), (1,)), ((), ())), preferred_element_type=F32)


def _dot_tn(a, b):
    return lax.dot_general(a, b, (((0,), (0,)), ((), ())), preferred_element_type=F32)


def _rms(x, g):
    return x * lax.rsqrt(jnp.mean(x * x, axis=-1, keepdims=True) + EPS) * g


def _ada_kernel(c_ref, w_ref, b_ref, o_ref):
    c = c_ref[...]
    cond = c * jax.nn.sigmoid(c)
    o_ref[...] = jnp.dot(cond, w_ref[...], preferred_element_type=F32, precision=HIGHEST) + b_ref[...]


def _ada(c, w, b):
    B, D = c.shape
    N = w.shape[1]
    tn = 1024
    return pl.pallas_call(
        _ada_kernel,
        grid=(N // tn,),
        in_specs=[pl.BlockSpec((B, D), lambda j: (0, 0)),
                  pl.BlockSpec((D, tn), lambda j: (0, j)),
                  pl.BlockSpec((1, tn), lambda j: (0, j))],
        out_specs=pl.BlockSpec((B, tn), lambda j: (0, j)),
        out_shape=jax.ShapeDtypeStruct((B, N), F32),
        compiler_params=_cparams(1),
        name="ada",
    )(c, w, b.reshape(1, N))


def _split3(f):
    hi = f.astype(BF16).astype(F32)
    r = f - hi
    mid = r.astype(BF16).astype(F32)
    return hi, mid, r - mid


def _proj_kernel(x_ref, mod_ref, g_ref, wqkv_ref, wf_ref, bf_ref, wh_ref, tri_ref,
                 qt_ref, ka_ref, vt_ref, hq_ref, hf_ref, hi_ref, hg_ref, carry_ref,
                 *, tiles_per_batch):
    i = pl.program_id(0)
    D = x_ref.shape[1]
    tm = x_ref.shape[0]
    Dh = FOX_HEAD_DIM
    x = x_ref[...]
    shift = mod_ref[0, :, 0:D]
    scale = mod_ref[0, :, D:2 * D]
    h = (_rms(x, g_ref[...]) * (1.0 + scale) + shift).astype(BF16)

    ff = _dot(h, wf_ref[...]) + bf_ref[...]
    logf = jnp.minimum(ff, 0.0) - jnp.log(1.0 + jnp.exp(-jnp.abs(ff)))

    @pl.when(i % tiles_per_batch == 0)
    def _():
        carry_ref[...] = jnp.zeros_like(carry_ref)

    cs = jnp.dot(tri_ref[...], logf, preferred_element_type=F32, precision=HIGHEST) + carry_ref[...]
    carry_ref[...] = cs[tm - 1:tm, :]

    qkv = _dot(h, wqkv_ref[...])
    lane = lax.broadcasted_iota(jnp.int32, (tm, Dh), 1)
    zpad = jnp.zeros((tm, Dh), F32)
    for hd in range(FOX_HEADS):
        lo = hd * Dh
        fhi, fmid, flo = _split3(cs[:, hd:hd + 1] * LOG2E)
        pieces = lambda o: jnp.where(lane == o, fhi, jnp.where(lane == o + 1, fmid, jnp.where(lane == o + 2, flo, 0.0)))
        q_aux = jnp.where(lane < 3, -1.0, pieces(3))
        k_aux = jnp.where((lane >= 3) & (lane < 6), 1.0, pieces(0))
        qa = jnp.concatenate([qkv[:, lo:lo + Dh] * (Dh ** -0.5 * LOG2E), q_aux], axis=1)
        ka = jnp.concatenate([qkv[:, FOX_WIDTH + lo:FOX_WIDTH + lo + Dh], k_aux], axis=1)
        va = jnp.concatenate([qkv[:, 2 * FOX_WIDTH + lo:2 * FOX_WIDTH + lo + Dh], zpad], axis=1)
        qt_ref[0, hd] = qa.T.astype(BF16)
        ka_ref[0, hd] = ka.astype(BF16)
        vt_ref[0, hd] = va.T[0:Dh, :].astype(BF16)

    hh = _dot(h, wh_ref[...])
    hq_ref[...] = hh[:, 0:HGRN_WIDTH]
    hf_ref[...] = hh[:, HGRN_WIDTH:2 * HGRN_WIDTH]
    hi_ref[...] = hh[:, 2 * HGRN_WIDTH:3 * HGRN_WIDTH]
    hg_ref[...] = hh[:, 3 * HGRN_WIDTH:4 * HGRN_WIDTH]


def _proj(x2d, mod3, g, wqkv, wf, bf, wh, B, S):
    T, D = x2d.shape
    tm = 256
    tpb = S // tm
    tri = (jnp.arange(tm)[:, None] >= jnp.arange(tm)[None, :]).astype(F32)
    H, Dh = FOX_HEADS, FOX_HEAD_DIM
    fox_shapes = [jax.ShapeDtypeStruct((B, H, LANES, S), BF16), jax.ShapeDtypeStruct((B, H, S, LANES), BF16),
                  jax.ShapeDtypeStruct((B, H, Dh, S), BF16)]
    fox_specs = [pl.BlockSpec((1, H, LANES, tm), lambda i: (i // tpb, 0, 0, i % tpb)),
                 pl.BlockSpec((1, H, tm, LANES), lambda i: (i // tpb, 0, i % tpb, 0)),
                 pl.BlockSpec((1, H, Dh, tm), lambda i: (i // tpb, 0, 0, i % tpb))]
    wide = jax.ShapeDtypeStruct((T, HGRN_WIDTH), F32)
    wspec = pl.BlockSpec((tm, HGRN_WIDTH), lambda i: (i, 0))
    const = lambda shape: pl.BlockSpec(shape, lambda i: (0,) * len(shape))
    return pl.pallas_call(
        functools.partial(_proj_kernel, tiles_per_batch=tpb),
        grid=(T // tm,),
        in_specs=[pl.BlockSpec((tm, D), lambda i: (i, 0)),
                  pl.BlockSpec((1, 1, N_MOD * D), lambda i: (i // tpb, 0, 0)),
                  const((1, D)), const(wqkv.shape), const(wf.shape), const(bf.shape), const(wh.shape),
                  const((tm, tm))],
        out_specs=fox_specs + [wspec, wspec, wspec, wspec],
        out_shape=fox_shapes + [wide, wide, wide, wide],
        scratch_shapes=[pltpu.VMEM((1, LANES), F32)],
        compiler_params=_cparams(1),
        name="proj",
    )(x2d, mod3, g, wqkv, wf, bf, wh, tri)


FOX_HB = 8


def _fox_kernel(qt_ref, ka_ref, vt_ref, o_ref, m_ref, l_ref, acc_ref, *, tq):
    qi = pl.program_id(2)
    m_ref[...] = jnp.full_like(m_ref, -jnp.inf)
    l_ref[...] = jnp.zeros_like(l_ref)
    acc_ref[...] = jnp.zeros_like(acc_ref)

    def step(j, masked):
        off = pl.multiple_of(j * tq, tq)
        hs = range(FOX_HB)
        s = [_dot(ka_ref[0, hh, pl.ds(off, tq), :], qt_ref[0, hh]) for hh in hs]
        if masked:
            key = lax.broadcasted_iota(jnp.int32, (tq, tq), 0)
            qry = lax.broadcasted_iota(jnp.int32, (tq, tq), 1)
            s = [jnp.where(key <= qry, sh, -jnp.inf) for sh in s]
        m_old = [m_ref[hh] for hh in hs]
        m_new = [jnp.maximum(m_old[hh], jnp.max(s[hh], axis=0, keepdims=True)) for hh in hs]
        p = [jnp.exp2(s[hh] - m_new[hh]) for hh in hs]
        alpha = [jnp.exp2(m_old[hh] - m_new[hh]) for hh in hs]
        pv = [_dot(vt_ref[0, hh, :, pl.ds(off, tq)], p[hh].astype(BF16)) for hh in hs]
        for hh in hs:
            l_ref[hh] = alpha[hh] * l_ref[hh] + jnp.sum(p[hh], axis=0, keepdims=True)
            acc_ref[hh] = alpha[hh] * acc_ref[hh] + pv[hh]
            m_ref[hh] = m_new[hh]

    step(qi, True)

    def body(j, _):
        step(j, False)
        return 0

    lax.fori_loop(0, qi, body, 0)
    for hh in range(FOX_HB):
        o_ref[0, hh] = (acc_ref[hh] / l_ref[hh]).T.astype(o_ref.dtype)


def _fox(qt, ka, vt):
    B, H, S, _ = ka.shape
    Dh = vt.shape[2]
    tq = 512
    hb = FOX_HB
    return pl.pallas_call(
        functools.partial(_fox_kernel, tq=tq),
        grid=(B, H // hb, S // tq),
        in_specs=[pl.BlockSpec((1, hb, LANES, tq), lambda b, h, i: (b, h, 0, i)),
                  pl.BlockSpec((1, hb, S, LANES), lambda b, h, i: (b, h, 0, 0)),
                  pl.BlockSpec((1, hb, Dh, S), lambda b, h, i: (b, h, 0, 0))],
        out_specs=pl.BlockSpec((1, hb, tq, Dh), lambda b, h, i: (b, h, i, 0)),
        out_shape=jax.ShapeDtypeStruct((B, H, S, Dh), BF16),
        scratch_shapes=[pltpu.VMEM((hb, 1, tq), F32), pltpu.VMEM((hb, 1, tq), F32), pltpu.VMEM((hb, Dh, tq), F32)],
        compiler_params=_cparams(3),
        name="fox",
    )(qt, ka, vt)


def _hgrn_kernel(hq_ref, hf_ref, hi_ref, hg_ref, gam_ref, ng_ref, bd_ref, bl_ref, o_ref, st_ref, os_ref):
    j = pl.program_id(1)
    tb = hq_ref.shape[0]
    C = HGRN_CHUNK
    Dk = HGRN_HEAD_DIM

    @pl.when(j == 0)
    def _():
        st_ref[...] = jnp.zeros_like(st_ref)

    gam = gam_ref[...]
    e = jnp.exp(gam - jnp.max(gam, axis=0, keepdims=True))
    lb = e[0:1, :] / jnp.sum(e, axis=0, keepdims=True)
    f = lb + (1.0 - lb) * jax.nn.sigmoid(hf_ref[...])
    lf = jnp.log(f)
    A = jnp.dot(bd_ref[...], lf, preferred_element_type=F32, precision=HIGHEST)
    AL = jnp.dot(bl_ref[...], lf, preferred_element_type=F32, precision=HIGHEST)
    q = hq_ref[...] * (Dk ** -0.5)
    kk = 1.0 - f
    qa = q * jnp.exp(A)
    ka = kk * jnp.exp(AL - A)
    eal = jnp.exp(AL)
    inp = hi_ref[...]

    tri3 = (lax.broadcasted_iota(jnp.int32, (C, C, Dk), 0) >= lax.broadcasted_iota(jnp.int32, (C, C, Dk), 1))
    for hd in range(HGRN_HEADS):
        lo = hd * Dk
        st = st_ref[hd]
        for c in range(tb // C):
            r0 = c * C
            Ac = A[r0:r0 + C, lo:lo + Dk]
            qc = q[r0:r0 + C, lo:lo + Dk]
            kc = kk[r0:r0 + C, lo:lo + Dk]
            ic = inp[r0:r0 + C, lo:lo + Dk]
            diff = Ac[:, None, :] - Ac[None, :, :]
            dec = jnp.exp(jnp.where(tri3, diff, -jnp.inf))
            sc = jnp.sum(qc[:, None, :] * kc[None, :, :] * dec, axis=-1)
            o = _dot_nt(qa[r0:r0 + C, lo:lo + Dk], st) + _dot(sc, ic)
            st = st * eal[r0:r0 + 1, lo:lo + Dk] + _dot_tn(ic, ka[r0:r0 + C, lo:lo + Dk])
            os_ref[r0:r0 + C, lo:lo + Dk] = o
        st_ref[hd] = st

    o_all = os_ref[...]
    hg = hg_ref[...]
    ng = ng_ref[...]
    for hd in range(HGRN_HEADS):
        lo = hd * Dk
        oh = _rms(o_all[:, lo:lo + Dk], ng[:, lo:lo + Dk])
        g = hg[:, lo:lo + Dk]
        o_ref[:, lo:lo + Dk] = (oh * (g * jax.nn.sigmoid(g))).astype(o_ref.dtype)


def _hgrn(hq, hf, hi, hg, gamma, norm_g, B, S):
    T, W = hq.shape
    tb = 128
    nb = S // tb
    t = jnp.arange(tb)
    same = (t[:, None] // HGRN_CHUNK) == (t[None, :] // HGRN_CHUNK)
    bd = (same & (t[:, None] >= t[None, :])).astype(F32)
    bl = same.astype(F32)
    wspec = pl.BlockSpec((tb, W), lambda b, j: (b * nb + j, 0))
    const = lambda shape: pl.BlockSpec(shape, lambda b, j: (0,) * len(shape))
    return pl.pallas_call(
        _hgrn_kernel,
        grid=(B, nb),
        in_specs=[wspec, wspec, wspec, wspec, const(gamma.shape), const((1, W)), const((tb, tb)), const((tb, tb))],
        out_specs=wspec,
        out_shape=jax.ShapeDtypeStruct((T, W), BF16),
        scratch_shapes=[pltpu.VMEM((HGRN_HEADS, HGRN_HEAD_DIM, HGRN_HEAD_DIM), F32), pltpu.VMEM((tb, W), F32)],
        compiler_params=_cparams(2),
        name="hgrn",
    )(hq, hf, hi, hg, gamma, norm_g, bd, bl)


def _outproj_kernel(x_ref, of_ref, oh_ref, mod_ref, wfo_ref, who_ref, pmg_ref, pfg_ref, wq_ref, keys_ref,
                    x1_ref, h2_ref, sc_ref):
    D = x_ref.shape[1]
    mix = _dot(oh_ref[...], who_ref[...])
    for hd in range(FOX_HEADS):
        mix = mix + _dot(of_ref[0, hd], wfo_ref[hd])
    gate_a = mod_ref[0, :, 2 * D:3 * D]
    shift_f = mod_ref[0, :, 3 * D:4 * D]
    scale_f = mod_ref[0, :, 4 * D:5 * D]
    x1 = x_ref[...] + gate_a * _rms(mix, pmg_ref[...])
    x1_ref[...] = x1
    h2 = _rms(x1, pfg_ref[...]) * (1.0 + scale_f) + shift_f
    h2_ref[...] = h2
    qp = _dot(h2.astype(BF16), wq_ref[...])
    for g in range(2 * PEER_HEADS):
        qg = qp[:, g * PEER_HALF:(g + 1) * PEER_HALF]
        qn = qg * lax.rsqrt(jnp.mean(qg * qg, axis=-1, keepdims=True) + EPS)
        sc_ref[g] = _dot_nt(keys_ref[g], qn.astype(BF16))


def _outproj(x2d, ofox, oh, mod3, wfo, who, pmg, pfg, wq, keys, B, S):
    T, D = x2d.shape
    tm = 256
    tpb = S // tm
    G = 2 * PEER_HEADS
    const = lambda shape: pl.BlockSpec(shape, lambda i: (0,) * len(shape))
    row = pl.BlockSpec((tm, D), lambda i: (i, 0))
    return pl.pallas_call(
        _outproj_kernel,
        grid=(T // tm,),
        in_specs=[row,
                  pl.BlockSpec((1, FOX_HEADS, tm, FOX_HEAD_DIM), lambda i: (i // tpb, 0, i % tpb, 0)),
                  pl.BlockSpec((tm, HGRN_WIDTH), lambda i: (i, 0)),
                  pl.BlockSpec((1, 1, N_MOD * D), lambda i: (i // tpb, 0, 0)),
                  const(wfo.shape), const(who.shape), const((1, D)), const((1, D)), const(wq.shape), const(keys.shape)],
        out_specs=[row, row, pl.BlockSpec((G, N_KEYS, tm), lambda i: (0, 0, i))],
        out_shape=[jax.ShapeDtypeStruct((T, D), F32), jax.ShapeDtypeStruct((T, D), F32),
                   jax.ShapeDtypeStruct((G, N_KEYS, T), F32)],
        compiler_params=_cparams(1),
        name="outproj",
    )(x2d, ofox, oh, mod3, wfo, who, pmg, pfg, wq, keys)


TOPK_TM = SUBLANES * LANES


def _cx(a, b, desc):
    (va, ia), (vb, ib) = a, b
    swap = (vb > va) if desc else (vb < va)
    return ((jnp.where(swap, vb, va), jnp.where(swap, ib, ia)),
            (jnp.where(swap, va, vb), jnp.where(swap, ia, ib)))


def _bitonic_merge(x, desc):
    n = len(x)
    x = list(x)
    d = n // 2
    while d >= 1:
        for i in range(n):
            if (i // d) % 2 == 0:
                x[i], x[i + d] = _cx(x[i], x[i + d], desc)
        d //= 2
    return x


def _bitonic_sort(x, desc):
    n = len(x)
    if n == 1:
        return list(x)
    lo = _bitonic_sort(x[:n // 2], True)
    hi = _bitonic_sort(x[n // 2:], False)
    return _bitonic_merge(lo + hi, desc)


def _tile(ref, row):
    return ref[pl.ds(pl.multiple_of(row * SUBLANES, SUBLANES), SUBLANES), :]


def _put(ref, row, x):
    ref[pl.ds(pl.multiple_of(row * SUBLANES, SUBLANES), SUBLANES), :] = x


def _top16_network(s_ref, n, p_ref, va_ref, ia_ref, vb_ref, ib_ref):
    K = PEER_TOPK
    shape = (SUBLANES, LANES)

    def store_block(v_ref, i_ref, blk, pairs):
        rev = blk % 2
        for k, (v, i) in enumerate(pairs):
            pos = blk * K + k + rev * (K - 1 - 2 * k)
            _put(v_ref, pos, v)
            _put(i_ref, pos, i)

    def sort_block(blk, c):
        tag = (lambda r: jnp.full(shape, r, jnp.int32)) if p_ref is None else (lambda r: _tile(p_ref, r))
        pairs = [(_tile(s_ref, blk * K + k), tag(blk * K + k)) for k in range(K)]
        store_block(va_ref, ia_ref, blk, _bitonic_sort(pairs, True))
        return c

    lax.fori_loop(0, n // K, sort_block, 0)

    def merge_round(src_v, src_i, dst_v, dst_i, n_pairs):
        def merge(p, c):
            top = []
            for k in range(K):
                va, ia = _tile(src_v, 2 * p * K + k), _tile(src_i, 2 * p * K + k)
                vb, ib = _tile(src_v, (2 * p + 1) * K + k), _tile(src_i, (2 * p + 1) * K + k)
                take_b = vb > va
                top.append((jnp.where(take_b, vb, va), jnp.where(take_b, ib, ia)))
            store_block(dst_v, dst_i, p, _bitonic_merge(top, True))
            return c
        lax.fori_loop(0, n_pairs, merge, 0)

    src, dst = (va_ref, ia_ref), (vb_ref, ib_ref)
    blocks = n // K
    while blocks > 1:
        merge_round(*src, *dst, blocks // 2)
        src, dst = dst, src
        blocks //= 2
    return ([src[0][k * SUBLANES:(k + 1) * SUBLANES, :] for k in range(K)],
            [src[1][k * SUBLANES:(k + 1) * SUBLANES, :] for k in range(K)])


def _untied(s_ref, n, vals):
    ok = vals[0] > vals[1]
    for a in range(1, PEER_TOPK - 1):
        ok = ok & (vals[a] > vals[a + 1])
    last = vals[-1]

    def count(c, cnt):
        for k in range(SUBLANES):
            cnt = cnt + jnp.where(_tile(s_ref, c * SUBLANES + k) >= last, 1, 0)
        return cnt
    cnt = lax.fori_loop(0, n // SUBLANES, count, jnp.zeros(last.shape, jnp.int32))
    return ok & (cnt == PEER_TOPK)


def _top16_exact_loop(s_ref, n, p_ref):
    shape = (SUBLANES, LANES)
    U = SUBLANES

    def over_rows(fn, init):
        def trip(c, acc):
            for u in range(U):
                acc = fn(c * U + u, acc)
            return acc
        return lax.fori_loop(0, n // U, trip, init)

    vals, picks = [], []
    for _ in range(PEER_TOPK):
        m = over_rows(lambda k, m: jnp.maximum(m, _tile(s_ref, k)), jnp.full(shape, -jnp.inf, F32))
        ix = over_rows(lambda k, ix: jnp.minimum(ix, jnp.where(_tile(s_ref, k) == m, k, n)),
                       jnp.full(shape, n, jnp.int32))

        def remove(k, pick):
            hit = ix == k
            _put(s_ref, k, jnp.where(hit, -jnp.inf, _tile(s_ref, k)))
            return pick if p_ref is None else pick + jnp.where(hit, _tile(p_ref, k), 0)
        pick = over_rows(remove, jnp.zeros(shape, jnp.int32))
        vals.append(m)
        picks.append(ix if p_ref is None else pick)
    return vals, picks


def _top16(s_ref, n, p_ref, bufs):
    vals, tags = _top16_network(s_ref, n, p_ref, *bufs)
    ok = jnp.min(jnp.where(_untied(s_ref, n, vals), 1, 0)) > 0
    pack = lambda v, i: (jnp.stack(v), jnp.stack(i))
    v, i = lax.cond(ok, lambda: pack(vals, tags), lambda: pack(*_top16_exact_loop(s_ref, n, p_ref)))
    return [v[a] for a in range(PEER_TOPK)], [i[a] for a in range(PEER_TOPK)]


N_CAND = 64


def _topk_kernel(sc_ref, off_ref, par_ref, gate_ref, s_ref, p_ref, va_ref, ia_ref, vb_ref, ib_ref, e_ref):
    K = PEER_TOPK
    tm = TOPK_TM
    bufs = (va_ref, ia_ref, vb_ref, ib_ref)
    shape = (SUBLANES, LANES)

    def half_topk(g):
        x = sc_ref[g]
        blocks = [x[:, s * LANES:(s + 1) * LANES].reshape(N_KEYS // SUBLANES, SUBLANES, LANES)
                  for s in range(SUBLANES)]
        s_ref[...] = pltpu.einshape("gskl->gksl", jnp.stack(blocks, axis=1)).reshape(N_KEYS * SUBLANES, LANES)
        return _top16(s_ref, N_KEYS, None, bufs)

    for hd in range(PEER_HEADS):
        s1, i1 = half_topk(2 * hd)
        s2, i2 = half_topk(2 * hd + 1)
        r = 0
        for a in range(K):
            for b in range(K // (a + 1)):
                _put(s_ref, r, s1[a] + s2[b])
                _put(p_ref, r, i1[a] * N_KEYS + i2[b])
                r += 1
        for r in range(r, N_CAND):
            _put(s_ref, r, jnp.full(shape, -jnp.inf, F32))
            _put(p_ref, r, jnp.zeros(shape, jnp.int32))
        best, eidx = _top16(s_ref, N_CAND, p_ref, bufs)
        p = [jnp.exp(bv - best[0]) for bv in best]
        tot = functools.reduce(jnp.add, p)
        for a in range(K):
            e = hd * K + a
            e_ref[0, e * SUBLANES:(e + 1) * SUBLANES, :] = eidx[a].astype(F32)
            e_ref[1, e * SUBLANES:(e + 1) * SUBLANES, :] = p[a] / tot

    def token_major(x):
        t = pltpu.einshape("gesl->gsel", x.reshape(PEER_E // SUBLANES, SUBLANES, SUBLANES, LANES))
        return jnp.concatenate([t[:, s].reshape(PEER_E, LANES).T for s in range(SUBLANES)], axis=0)

    n_half = N_KEYS * N_KEYS // 2
    idx = token_major(e_ref[0]).astype(jnp.int32)
    gate_ref[...] = token_major(e_ref[1])
    par_ref[...] = (idx >= n_half).astype(F32)
    off = (idx & (n_half - 1)) * SUBLANES
    off_ref[...] = pltpu.einshape("abl->bal", off.reshape(tm // PEER_GROUP, PEER_GROUP, PEER_E))


def _topk(scores):
    G, NK, T = scores.shape
    tm = TOPK_TM
    out = pl.BlockSpec((tm, PEER_E), lambda i: (i, 0))
    return pl.pallas_call(
        _topk_kernel,
        grid=(T // tm,),
        in_specs=[pl.BlockSpec((G, NK, tm), lambda i: (0, 0, i))],
        out_specs=[pl.BlockSpec((PEER_GROUP, tm // PEER_GROUP, PEER_E), lambda i: (0, i, 0)), out, out],
        out_shape=[jax.ShapeDtypeStruct((PEER_GROUP, T // PEER_GROUP, PEER_E), jnp.int32),
                   jax.ShapeDtypeStruct((T, PEER_E), F32), jax.ShapeDtypeStruct((T, PEER_E), F32)],
        scratch_shapes=[pltpu.VMEM((N_KEYS * SUBLANES, LANES), F32), pltpu.VMEM((N_CAND * SUBLANES, LANES), jnp.int32),
                        pltpu.VMEM((N_KEYS * SUBLANES, LANES), F32), pltpu.VMEM((N_KEYS * SUBLANES, LANES), jnp.int32),
                        pltpu.VMEM((N_KEYS * SUBLANES // 2, LANES), F32),
                        pltpu.VMEM((N_KEYS * SUBLANES // 2, LANES), jnp.int32),
                        pltpu.VMEM((2, PEER_E * SUBLANES, LANES), F32)],
        compiler_params=_cparams(1),
        name="topk",
    )(scores)


PEER_TM = 8 * PEER_GROUP
TILE_ROWS = 2 * SUBLANES
STACK = PEER_E * TILE_ROWS


def _pack_kernel(lo_ref, hi_ref, o_ref):
    nb = lo_ref.shape[0]
    lo = lax.bitcast_convert_type(lo_ref[...].astype(BF16).astype(F32), jnp.uint32)
    hi = lax.bitcast_convert_type(hi_ref[...].astype(BF16).astype(F32), jnp.uint32)
    word = (hi & jnp.uint32(0xFFFF0000)) | lax.shift_right_logical(lo, jnp.uint32(16))
    for s in range(SUBLANES):
        o_ref[pl.ds(s, nb, stride=SUBLANES), :] = word[:, s * LANES:(s + 1) * LANES]


def _pack_table(w):
    n, d = w.shape
    nb = 256
    steps = n // 2 // nb
    return pl.pallas_call(
        _pack_kernel,
        grid=(steps,),
        in_specs=[pl.BlockSpec((nb, d), lambda i: (i, 0)), pl.BlockSpec((nb, d), lambda i: (i + steps, 0))],
        out_specs=pl.BlockSpec((nb * SUBLANES, LANES), lambda i: (i, 0)),
        out_shape=jax.ShapeDtypeStruct((n // 2 * SUBLANES, LANES), jnp.uint32),
        compiler_params=_cparams(1),
        name="pack",
    )(w, w)


def _peer_consts():
    c = jnp.arange(STACK)
    expand = (c[None, :] // TILE_ROWS == jnp.arange(PEER_E)[:, None])
    fold = ((c[None, :] % TILE_ROWS) // 2 == jnp.arange(SUBLANES)[:, None])
    half = (c % 2).reshape(1, STACK)
    return expand.astype(BF16), expand.T.astype(BF16), fold.astype(F32), half.astype(F32)


def _stack2(tab_ref, off_a, off_b, g):
    def tile(o):
        return pltpu.bitcast(tab_ref[pl.ds(pl.multiple_of(o, SUBLANES), SUBLANES), :], BF16)
    w0 = jnp.concatenate([tile(off_a[g, j]) for j in range(PEER_E)], axis=0)
    w1 = jnp.concatenate([tile(off_b[g, j]) for j in range(PEER_E)], axis=0)
    return jnp.concatenate([w0, w1], axis=1)


def _split2_dot(x, w):
    hi = x.astype(BF16)
    lo = (x - hi.astype(F32)).astype(BF16)
    return _dot(hi, w) + _dot(lo, w)


def _expand_sel(par_ref, expand_ref, half_ref):
    par = _dot(par_ref[...].astype(BF16), expand_ref[...])
    return jnp.where(par == half_ref[...], 1.0, 0.0)


def _fold_rows(x):
    tm = x.shape[0]
    chunks = [x[:, c * LANES:(c + 1) * LANES].reshape(tm // SUBLANES, SUBLANES, LANES) for c in range(SUBLANES)]
    return pltpu.einshape("gctl->gtcl", jnp.stack(chunks, axis=1)).reshape(tm * SUBLANES, LANES)


def _unfold_rows(xf):
    tm = xf.shape[0] // SUBLANES
    t = pltpu.einshape("gtcl->gctl", xf.reshape(tm // SUBLANES, SUBLANES, SUBLANES, LANES))
    return jnp.concatenate([t[:, c].reshape(tm, LANES) for c in range(SUBLANES)], axis=1)


def _peer_u_kernel(*refs):
    offs = refs[:PEER_GROUP]
    (par_ref, x_ref, gate_ref, expand_ref, collapse_ref, fold_ref, half_ref, tab_ref,
     act_ref, sel_ref, g_ref, xf_ref) = refs[PEER_GROUP:]
    tm = x_ref.shape[0]
    sel_ref[...] = _expand_sel(par_ref, expand_ref, half_ref)
    xf_ref[...] = _fold_rows(x_ref[...])
    fold = fold_ref[...]
    zero = jnp.zeros((SUBLANES, LANES), BF16)

    def pair(g, p):
        t = PEER_GROUP * g + 2 * p
        w2 = _stack2(tab_ref, offs[2 * p], offs[2 * p + 1], g)
        x0 = xf_ref[pl.ds(pl.multiple_of(t * SUBLANES, SUBLANES), SUBLANES), :].astype(BF16)
        x1 = xf_ref[pl.ds(pl.multiple_of((t + 1) * SUBLANES, SUBLANES), SUBLANES), :].astype(BF16)
        lhs = jnp.concatenate([jnp.concatenate([x0, zero], axis=1),
                               jnp.concatenate([zero, x1], axis=1)], axis=0)
        gm = _dot_nt(lhs, w2)
        g_ref[pl.ds(t, 1), :] = jnp.sum(gm[0:SUBLANES] * fold, axis=0, keepdims=True) * sel_ref[pl.ds(t, 1), :]
        g_ref[pl.ds(t + 1, 1), :] = (jnp.sum(gm[SUBLANES:] * fold, axis=0, keepdims=True)
                                     * sel_ref[pl.ds(t + 1, 1), :])

    def group(g, _):
        for p in range(PEER_GROUP // 2):
            pair(g, p)
        return 0

    lax.fori_loop(0, tm // PEER_GROUP, group, 0)
    a = _split2_dot(g_ref[...], collapse_ref[...])
    gelu = 0.5 * a * (1.0 + lax.erf(a * (2.0 ** -0.5)))
    act_ref[...] = gate_ref[...] * gelu


def _peer_v_kernel(*refs):
    offs = refs[:PEER_GROUP]
    (par_ref, act_ref, expand_ref, fold_ref, half_ref, tab_ref, x1_ref, mod_ref, g_ref,
     o_ref, ce_ref, yf_ref) = refs[PEER_GROUP:]
    tm, D = x1_ref.shape
    ce_ref[...] = _dot(act_ref[...].astype(BF16), expand_ref[...]) * _expand_sel(par_ref, expand_ref, half_ref)
    fold = fold_ref[...]

    def pair(g, p):
        t = PEER_GROUP * g + 2 * p
        w2 = _stack2(tab_ref, offs[2 * p], offs[2 * p + 1], g)
        coef = jnp.concatenate([(ce_ref[pl.ds(t, 1), :] * fold).astype(BF16),
                                (ce_ref[pl.ds(t + 1, 1), :] * fold).astype(BF16)], axis=0)
        out = _dot(coef, w2)
        yf_ref[pl.ds(pl.multiple_of(t * SUBLANES, SUBLANES), SUBLANES), :] = out[0:SUBLANES, 0:LANES]
        yf_ref[pl.ds(pl.multiple_of((t + 1) * SUBLANES, SUBLANES), SUBLANES), :] = out[SUBLANES:, LANES:]

    def group(g, _):
        for p in range(PEER_GROUP // 2):
            pair(g, p)
        return 0

    lax.fori_loop(0, tm // PEER_GROUP, group, 0)
    y = _unfold_rows(yf_ref[...])
    gate_f = mod_ref[0, :, 5 * D:6 * D]
    o_ref[...] = x1_ref[...] + gate_f * _rms(y, g_ref[...])


def _peer_specs(tab):
    tm = PEER_TM
    smem = [pl.BlockSpec((None, tm // PEER_GROUP, PEER_E), functools.partial(lambda k, i: (k, i, 0), k),
                         memory_space=pltpu.SMEM) for k in range(PEER_GROUP)]
    row = pl.BlockSpec((tm, PEER_E), lambda i: (i, 0))
    const = lambda shape: pl.BlockSpec(shape, lambda i: (0,) * len(shape))
    table = pl.BlockSpec(tab.shape, lambda i: (0, 0), pipeline_mode=pl.Buffered(1))
    return tm, smem, row, const, table


def _peer_u(off, par, h2, gate, tab):
    T, D = h2.shape
    tm, smem, row, const, table = _peer_specs(tab)
    expand, collapse, fold, half = _peer_consts()
    return pl.pallas_call(
        _peer_u_kernel,
        grid=(T // tm,),
        in_specs=smem + [row, pl.BlockSpec((tm, D), lambda i: (i, 0)), row,
                  const(expand.shape), const(collapse.shape), const(fold.shape), const(half.shape), table],
        out_specs=row,
        out_shape=jax.ShapeDtypeStruct((T, PEER_E), F32),
        scratch_shapes=[pltpu.VMEM((tm, STACK), F32), pltpu.VMEM((tm, STACK), F32),
                        pltpu.VMEM((tm * SUBLANES, LANES), F32)],
        compiler_params=_cparams(1),
        name="peer_u",
    )(*([off] * PEER_GROUP), par, h2, gate, expand, collapse, fold, half, tab)


def _peer_v(off, par, act, tab, x1, mod3, g, S):
    T, D = x1.shape
    tm, smem, row, const, table = _peer_specs(tab)
    tpb = S // tm
    expand, _, fold, half = _peer_consts()
    wide = pl.BlockSpec((tm, D), lambda i: (i, 0))
    return pl.pallas_call(
        _peer_v_kernel,
        grid=(T // tm,),
        in_specs=smem + [row, row, const(expand.shape), const(fold.shape), const(half.shape), table,
                  wide, pl.BlockSpec((1, 1, N_MOD * D), lambda i: (i // tpb, 0, 0)), const((1, D))],
        out_specs=wide,
        out_shape=jax.ShapeDtypeStruct((T, D), F32),
        scratch_shapes=[pltpu.VMEM((tm, STACK), F32), pltpu.VMEM((tm * SUBLANES, LANES), F32)],
        compiler_params=_cparams(1),
        name="peer_v",
    )(*([off] * PEER_GROUP), par, act, expand, fold, half, tab, x1, mod3, g)


def kernel(x, c, w_ada, b_ada, pre_mix_g, post_mix_g, w_in, b_fox_f, hgrn_gamma, hgrn_norm_g, w_out, pre_ffn_g,
           post_ffn_g, peer_w_q, peer_sub_keys, peer_u, peer_v):
    B, S, D = x.shape
    T = B * S
    depth = w_in.shape[0]
    assert depth == 1, "single-layer block"
    l = 0
    x2d = x.reshape(T, D)

    mod3 = _ada(c, w_ada[l], b_ada[l]).reshape(B, 1, N_MOD * D)

    w = w_in[l]
    o1 = 3 * FOX_WIDTH
    wqkv = w[:, :o1].astype(BF16)
    wf = jnp.pad(w[:, o1:o1 + FOX_HEADS], ((0, 0), (0, LANES - FOX_HEADS))).astype(BF16)
    bf = jnp.pad(b_fox_f[l].astype(F32), (0, LANES - FOX_HEADS)).reshape(1, LANES)
    wh = w[:, o1 + FOX_HEADS:].astype(BF16)

    qt, ka, vt, hq, hf, hi, hg = _proj(x2d, mod3, pre_mix_g[l].reshape(1, D), wqkv, wf, bf, wh, B, S)

    o_fox = _fox(qt, ka, vt)

    o_h = _hgrn(hq, hf, hi, hg, hgrn_gamma[l:l + 2].astype(F32), hgrn_norm_g[l].reshape(1, HGRN_WIDTH), B, S)

    wo = w_out[l].astype(BF16)
    wfo = wo[:FOX_WIDTH].reshape(FOX_HEADS, FOX_HEAD_DIM, D)
    who = wo[FOX_WIDTH:]
    keys = peer_sub_keys[l].reshape(2 * PEER_HEADS, N_KEYS, PEER_HALF).astype(BF16)
    x1, h2, scores = _outproj(x2d, o_fox, o_h, mod3, wfo, who, post_mix_g[l].reshape(1, D),
                              pre_ffn_g[l].reshape(1, D), peer_w_q[l].astype(BF16), keys, B, S)

    off, par, gate = _topk(scores)

    act = _peer_u(off, par, h2, gate, _pack_table(peer_u[l]))
    out = _peer_v(off, par, act, _pack_table(peer_v[l]), x1, mod3, post_ffn_g[l].reshape(1, D), S)
    return out.reshape(B, S, D)
```

```python
import functools
import math

import jax
import jax.numpy as jnp
from jax import lax
from jax.experimental import pallas as pl
from jax.experimental.pallas import tpu as pltpu

F32 = jnp.float32
BF16 = jnp.bfloat16
EPS = 1e-6
HIGHEST = lax.Precision.HIGHEST
LOG2E = math.log2(math.e)

FOX_HEADS = 8
FOX_HEAD_DIM = 64
FOX_WIDTH = FOX_HEADS * FOX_HEAD_DIM
HGRN_HEADS = 4
HGRN_HEAD_DIM = 128
HGRN_WIDTH = HGRN_HEADS * HGRN_HEAD_DIM
HGRN_CHUNK = 16
PEER_HEADS = 8
PEER_QUERY_DIM = 256
PEER_HALF = PEER_QUERY_DIM // 2
N_KEYS = 128
PEER_TOPK = 16
PEER_E = PEER_HEADS * PEER_TOPK
PEER_GROUP = 32
N_MOD = 6

LANES = 128
SUBLANES = 8
VMEM_LIMIT = 56 * 1024 * 1024


def _cparams(n_axes, vmem=VMEM_LIMIT):
    return pltpu.CompilerParams(dimension_semantics=("arbitrary",) * n_axes, vmem_limit_bytes=vmem)


def _dot(a, b):
    return jnp.dot(a, b, preferred_element_type=F32)


def _dot_nt(a, b):
    return lax.dot_general(a, b, (((1,), (1,)), ((), ())), preferred_element_type=F32)


def _dot_tn(a, b):
    return lax.dot_general(a, b, (((0,), (0,)), ((), ())), preferred_element_type=F32)


def _rms(x, g):
    return x * lax.rsqrt(jnp.mean(x * x, axis=-1, keepdims=True) + EPS) * g


def _ada_kernel(c_ref, w_ref, b_ref, o_ref):
    c = c_ref[...]
    cond = c * jax.nn.sigmoid(c)
    o_ref[...] = jnp.dot(cond, w_ref[...], preferred_element_type=F32, precision=HIGHEST) + b_ref[...]


def _ada(c, w, b):
    B, D = c.shape
    N = w.shape[1]
    tn = 1024
    return pl.pallas_call(
        _ada_kernel,
        grid=(N // tn,),
        in_specs=[pl.BlockSpec((B, D), lambda j: (0, 0)),
                  pl.BlockSpec((D, tn), lambda j: (0, j)),
                  pl.BlockSpec((1, tn), lambda j: (0, j))],
        out_specs=pl.BlockSpec((B, tn), lambda j: (0, j)),
        out_shape=jax.ShapeDtypeStruct((B, N), F32),
        compiler_params=_cparams(1),
        name="ada",
    )(c, w, b.reshape(1, N))


def _split3(f):
    hi = f.astype(BF16).astype(F32)
    r = f - hi
    mid = r.astype(BF16).astype(F32)
    return hi, mid, r - mid


def _proj_kernel(x_ref, mod_ref, g_ref, wqkv_ref, wf_ref, bf_ref, wh_ref, tri_ref,
                 qt_ref, ka_ref, vt_ref, hq_ref, hf_ref, hi_ref, hg_ref, carry_ref,
                 *, tiles_per_batch):
    i = pl.program_id(0)
    D = x_ref.shape[1]
    tm = x_ref.shape[0]
    Dh = FOX_HEAD_DIM
    x = x_ref[...]
    shift = mod_ref[0, :, 0:D]
    scale = mod_ref[0, :, D:2 * D]
    h = (_rms(x, g_ref[...]) * (1.0 + scale) + shift).astype(BF16)

    ff = _dot(h, wf_ref[...]) + bf_ref[...]
    logf = jnp.minimum(ff, 0.0) - jnp.log(1.0 + jnp.exp(-jnp.abs(ff)))

    @pl.when(i % tiles_per_batch == 0)
    def _():
        carry_ref[...] = jnp.zeros_like(carry_ref)

    cs = jnp.dot(tri_ref[...], logf, preferred_element_type=F32, precision=HIGHEST) + carry_ref[...]
    carry_ref[...] = cs[tm - 1:tm, :]

    qkv = _dot(h, wqkv_ref[...])
    lane = lax.broadcasted_iota(jnp.int32, (tm, Dh), 1)
    zpad = jnp.zeros((tm, Dh), F32)
    for hd in range(FOX_HEADS):
        lo = hd * Dh
        fhi, fmid, flo = _split3(cs[:, hd:hd + 1] * LOG2E)
        pieces = lambda o: jnp.where(lane == o, fhi, jnp.where(lane == o + 1, fmid, jnp.where(lane == o + 2, flo, 0.0)))
        q_aux = jnp.where(lane < 3, -1.0, pieces(3))
        k_aux = jnp.where((lane >= 3) & (lane < 6), 1.0, pieces(0))
        qa = jnp.concatenate([qkv[:, lo:lo + Dh] * (Dh ** -0.5 * LOG2E), q_aux], axis=1)
        ka = jnp.concatenate([qkv[:, FOX_WIDTH + lo:FOX_WIDTH + lo + Dh], k_aux], axis=1)
        va = jnp.concatenate([qkv[:, 2 * FOX_WIDTH + lo:2 * FOX_WIDTH + lo + Dh], zpad], axis=1)
        qt_ref[0, hd] = qa.T.astype(BF16)
        ka_ref[0, hd] = ka.astype(BF16)
        vt_ref[0, hd] = va.T[0:Dh, :].astype(BF16)

    hh = _dot(h, wh_ref[...])
    hq_ref[...] = hh[:, 0:HGRN_WIDTH]
    hf_ref[...] = hh[:, HGRN_WIDTH:2 * HGRN_WIDTH]
    hi_ref[...] = hh[:, 2 * HGRN_WIDTH:3 * HGRN_WIDTH]
    hg_ref[...] = hh[:, 3 * HGRN_WIDTH:4 * HGRN_WIDTH]


def _proj(x2d, mod3, g, wqkv, wf, bf, wh, B, S):
    T, D = x2d.shape
    tm = 256
    tpb = S // tm
    tri = (jnp.arange(tm)[:, None] >= jnp.arange(tm)[None, :]).astype(F32)
    H, Dh = FOX_HEADS, FOX_HEAD_DIM
    fox_shapes = [jax.ShapeDtypeStruct((B, H, LANES, S), BF16), jax.ShapeDtypeStruct((B, H, S, LANES), BF16),
                  jax.ShapeDtypeStruct((B, H, Dh, S), BF16)]
    fox_specs = [pl.BlockSpec((1, H, LANES, tm), lambda i: (i // tpb, 0, 0, i % tpb)),
                 pl.BlockSpec((1, H, tm, LANES), lambda i: (i // tpb, 0, i % tpb, 0)),
                 pl.BlockSpec((1, H, Dh, tm), lambda i: (i // tpb, 0, 0, i % tpb))]
    wide = jax.ShapeDtypeStruct((T, HGRN_WIDTH), F32)
    wspec = pl.BlockSpec((tm, HGRN_WIDTH), lambda i: (i, 0))
    const = lambda shape: pl.BlockSpec(shape, lambda i: (0,) * len(shape))
    return pl.pallas_call(
        functools.partial(_proj_kernel, tiles_per_batch=tpb),
        grid=(T // tm,),
        in_specs=[pl.BlockSpec((tm, D), lambda i: (i, 0)),
                  pl.BlockSpec((1, 1, N_MOD * D), lambda i: (i // tpb, 0, 0)),
                  const((1, D)), const(wqkv.shape), const(wf.shape), const(bf.shape), const(wh.shape),
                  const((tm, tm))],
        out_specs=fox_specs + [wspec, wspec, wspec, wspec],
        out_shape=fox_shapes + [wide, wide, wide, wide],
        scratch_shapes=[pltpu.VMEM((1, LANES), F32)],
        compiler_params=_cparams(1),
        name="proj",
    )(x2d, mod3, g, wqkv, wf, bf, wh, tri)


FOX_HB = 8


def _fox_kernel(qt_ref, ka_ref, vt_ref, o_ref, m_ref, l_ref, acc_ref, *, tq):
    qi = pl.program_id(2)
    m_ref[...] = jnp.full_like(m_ref, -jnp.inf)
    l_ref[...] = jnp.zeros_like(l_ref)
    acc_ref[...] = jnp.zeros_like(acc_ref)

    def step(j, masked):
        off = pl.multiple_of(j * tq, tq)
        hs = range(FOX_HB)
        s = [_dot(ka_ref[0, hh, pl.ds(off, tq), :], qt_ref[0, hh]) for hh in hs]
        if masked:
            key = lax.broadcasted_iota(jnp.int32, (tq, tq), 0)
            qry = lax.broadcasted_iota(jnp.int32, (tq, tq), 1)
            s = [jnp.where(key <= qry, sh, -jnp.inf) for sh in s]
        m_old = [m_ref[hh] for hh in hs]
        m_new = [jnp.maximum(m_old[hh], jnp.max(s[hh], axis=0, keepdims=True)) for hh in hs]
        p = [jnp.exp2(s[hh] - m_new[hh]) for hh in hs]
        alpha = [jnp.exp2(m_old[hh] - m_new[hh]) for hh in hs]
        pv = [_dot(vt_ref[0, hh, :, pl.ds(off, tq)], p[hh].astype(BF16)) for hh in hs]
        for hh in hs:
            l_ref[hh] = alpha[hh] * l_ref[hh] + jnp.sum(p[hh], axis=0, keepdims=True)
            acc_ref[hh] = alpha[hh] * acc_ref[hh] + pv[hh]
            m_ref[hh] = m_new[hh]

    step(qi, True)

    def body(j, _):
        step(j, False)
        return 0

    lax.fori_loop(0, qi, body, 0)
    o = [(acc_ref[hh] / l_ref[hh]).T for hh in range(FOX_HB)]
    o_ref[0] = jnp.concatenate(o, axis=1).astype(o_ref.dtype)


def _fox(qt, ka, vt):
    B, H, S, _ = ka.shape
    Dh = vt.shape[2]
    tq = 512
    hb = FOX_HB
    return pl.pallas_call(
        functools.partial(_fox_kernel, tq=tq),
        grid=(B, H // hb, S // tq),
        in_specs=[pl.BlockSpec((1, hb, LANES, tq), lambda b, h, i: (b, h, 0, i)),
                  pl.BlockSpec((1, hb, S, LANES), lambda b, h, i: (b, h, 0, 0)),
                  pl.BlockSpec((1, hb, Dh, S), lambda b, h, i: (b, h, 0, 0))],
        out_specs=pl.BlockSpec((1, tq, hb * Dh), lambda b, h, i: (b, i, h)),
        out_shape=jax.ShapeDtypeStruct((B, S, H * Dh), BF16),
        scratch_shapes=[pltpu.VMEM((hb, 1, tq), F32), pltpu.VMEM((hb, 1, tq), F32), pltpu.VMEM((hb, Dh, tq), F32)],
        compiler_params=_cparams(3),
        name="fox",
    )(qt, ka, vt)


def _hgrn_kernel(hq_ref, hf_ref, hi_ref, hg_ref, gam_ref, ng_ref, bd_ref, bl_ref, o_ref, st_ref, os_ref):
    j = pl.program_id(1)
    tb = hq_ref.shape[0]
    C = HGRN_CHUNK
    Dk = HGRN_HEAD_DIM

    @pl.when(j == 0)
    def _():
        st_ref[...] = jnp.zeros_like(st_ref)

    gam = gam_ref[...]
    e = jnp.exp(gam - jnp.max(gam, axis=0, keepdims=True))
    lb = e[0:1, :] / jnp.sum(e, axis=0, keepdims=True)
    f = lb + (1.0 - lb) * jax.nn.sigmoid(hf_ref[...])
    lf = jnp.log(f)
    A = jnp.dot(bd_ref[...], lf, preferred_element_type=F32, precision=HIGHEST)
    AL = jnp.dot(bl_ref[...], lf, preferred_element_type=F32, precision=HIGHEST)
    q = hq_ref[...] * (Dk ** -0.5)
    kk = 1.0 - f
    qa = q * jnp.exp(A)
    ka = kk * jnp.exp(AL - A)
    eal = jnp.exp(AL)
    inp = hi_ref[...]

    tri3 = (lax.broadcasted_iota(jnp.int32, (C, C, Dk), 0) >= lax.broadcasted_iota(jnp.int32, (C, C, Dk), 1))
    heads = range(HGRN_HEADS)
    chunks = range(tb // C)
    blk = lambda a, c, hd: a[c * C:(c + 1) * C, hd * Dk:(hd + 1) * Dk]
    intra = {}
    for c in chunks:
        for hd in heads:
            Ac = blk(A, c, hd)
            diff = Ac[:, None, :] - Ac[None, :, :]
            dec = jnp.exp(jnp.where(tri3, diff, -jnp.inf))
            sc = jnp.sum(blk(q, c, hd)[:, None, :] * blk(kk, c, hd)[None, :, :] * dec, axis=-1)
            intra[c, hd] = _dot(sc, blk(inp, c, hd))
    st = [st_ref[hd] for hd in heads]
    for c in chunks:
        for hd in heads:
            os_ref[c * C:(c + 1) * C, hd * Dk:(hd + 1) * Dk] = _dot_nt(blk(qa, c, hd), st[hd]) + intra[c, hd]
            st[hd] = st[hd] * blk(eal, c, hd)[0:1, :] + _dot_tn(blk(inp, c, hd), blk(ka, c, hd))
    for hd in heads:
        st_ref[hd] = st[hd]

    o_all = os_ref[...]
    hg = hg_ref[...]
    ng = ng_ref[...]
    for hd in range(HGRN_HEADS):
        lo = hd * Dk
        oh = _rms(o_all[:, lo:lo + Dk], ng[:, lo:lo + Dk])
        g = hg[:, lo:lo + Dk]
        o_ref[:, lo:lo + Dk] = (oh * (g * jax.nn.sigmoid(g))).astype(o_ref.dtype)


def _hgrn(hq, hf, hi, hg, gamma, norm_g, B, S):
    T, W = hq.shape
    tb = 128
    nb = S // tb
    t = jnp.arange(tb)
    same = (t[:, None] // HGRN_CHUNK) == (t[None, :] // HGRN_CHUNK)
    bd = (same & (t[:, None] >= t[None, :])).astype(F32)
    bl = same.astype(F32)
    wspec = pl.BlockSpec((tb, W), lambda b, j: (b * nb + j, 0))
    const = lambda shape: pl.BlockSpec(shape, lambda b, j: (0,) * len(shape))
    return pl.pallas_call(
        _hgrn_kernel,
        grid=(B, nb),
        in_specs=[wspec, wspec, wspec, wspec, const(gamma.shape), const((1, W)), const((tb, tb)), const((tb, tb))],
        out_specs=wspec,
        out_shape=jax.ShapeDtypeStruct((T, W), BF16),
        scratch_shapes=[pltpu.VMEM((HGRN_HEADS, HGRN_HEAD_DIM, HGRN_HEAD_DIM), F32), pltpu.VMEM((tb, W), F32)],
        compiler_params=_cparams(2),
        name="hgrn",
    )(hq, hf, hi, hg, gamma, norm_g, bd, bl)


def _outproj_kernel(x_ref, of_ref, oh_ref, mod_ref, wo_ref, pmg_ref, pfg_ref, wq_ref, keys_ref,
                    x1_ref, h2_ref, sc_ref):
    D = x_ref.shape[1]
    mix = _dot(jnp.concatenate([of_ref[...], oh_ref[...]], axis=1), wo_ref[...])
    gate_a = mod_ref[0, :, 2 * D:3 * D]
    shift_f = mod_ref[0, :, 3 * D:4 * D]
    scale_f = mod_ref[0, :, 4 * D:5 * D]
    x1 = x_ref[...] + gate_a * _rms(mix, pmg_ref[...])
    x1_ref[...] = x1
    h2 = _rms(x1, pfg_ref[...]) * (1.0 + scale_f) + shift_f
    h2_ref[...] = h2
    qp = _dot(h2.astype(BF16), wq_ref[...])
    for g in range(2 * PEER_HEADS):
        qg = qp[:, g * PEER_HALF:(g + 1) * PEER_HALF]
        qn = qg * lax.rsqrt(jnp.mean(qg * qg, axis=-1, keepdims=True) + EPS)
        sc_ref[g] = _dot_nt(keys_ref[g], qn.astype(BF16))


def _outproj(x2d, ofox, oh, mod3, wo, pmg, pfg, wq, keys, B, S):
    T, D = x2d.shape
    tm = 256
    tpb = S // tm
    G = 2 * PEER_HEADS
    const = lambda shape: pl.BlockSpec(shape, lambda i: (0,) * len(shape))
    row = pl.BlockSpec((tm, D), lambda i: (i, 0))
    return pl.pallas_call(
        _outproj_kernel,
        grid=(T // tm,),
        in_specs=[row,
                  pl.BlockSpec((tm, FOX_WIDTH), lambda i: (i, 0)),
                  pl.BlockSpec((tm, HGRN_WIDTH), lambda i: (i, 0)),
                  pl.BlockSpec((1, 1, N_MOD * D), lambda i: (i // tpb, 0, 0)),
                  const(wo.shape), const((1, D)), const((1, D)), const(wq.shape), const(keys.shape)],
        out_specs=[row, row, pl.BlockSpec((G, N_KEYS, tm), lambda i: (0, 0, i))],
        out_shape=[jax.ShapeDtypeStruct((T, D), F32), jax.ShapeDtypeStruct((T, D), F32),
                   jax.ShapeDtypeStruct((G, N_KEYS, T), F32)],
        compiler_params=_cparams(1),
        name="outproj",
    )(x2d, ofox, oh, mod3, wo, pmg, pfg, wq, keys)


TOPK_TM = SUBLANES * LANES


def _cx(a, b, desc):
    (va, ia), (vb, ib) = a, b
    swap = (vb > va) if desc else (vb < va)
    return ((jnp.where(swap, vb, va), jnp.where(swap, ib, ia)),
            (jnp.where(swap, va, vb), jnp.where(swap, ia, ib)))


def _bitonic_merge(x, desc):
    n = len(x)
    x = list(x)
    d = n // 2
    while d >= 1:
        for i in range(n):
            if (i // d) % 2 == 0:
                x[i], x[i + d] = _cx(x[i], x[i + d], desc)
        d //= 2
    return x


def _bitonic_sort(x, desc):
    n = len(x)
    if n == 1:
        return list(x)
    lo = _bitonic_sort(x[:n // 2], True)
    hi = _bitonic_sort(x[n // 2:], False)
    return _bitonic_merge(lo + hi, desc)


def _tile(ref, row):
    return ref[pl.ds(pl.multiple_of(row * SUBLANES, SUBLANES), SUBLANES), :]


def _put(ref, row, x):
    ref[pl.ds(pl.multiple_of(row * SUBLANES, SUBLANES), SUBLANES), :] = x


def _top16_network(s_ref, n, p_ref, va_ref, ia_ref, vb_ref, ib_ref):
    K = PEER_TOPK
    shape = (SUBLANES, LANES)

    def store_block(v_ref, i_ref, blk, pairs):
        rev = blk % 2
        for k, (v, i) in enumerate(pairs):
            pos = blk * K + k + rev * (K - 1 - 2 * k)
            _put(v_ref, pos, v)
            _put(i_ref, pos, i)

    def sort_block(blk, c):
        tag = (lambda r: jnp.full(shape, r, jnp.int32)) if p_ref is None else (lambda r: _tile(p_ref, r))
        pairs = [(_tile(s_ref, blk * K + k), tag(blk * K + k)) for k in range(K)]
        store_block(va_ref, ia_ref, blk, _bitonic_sort(pairs, True))
        return c

    lax.fori_loop(0, n // K, sort_block, 0)

    def merge_round(src_v, src_i, dst_v, dst_i, n_pairs):
        def merge(p, c):
            top = []
            for k in range(K):
                va, ia = _tile(src_v, 2 * p * K + k), _tile(src_i, 2 * p * K + k)
                vb, ib = _tile(src_v, (2 * p + 1) * K + k), _tile(src_i, (2 * p + 1) * K + k)
                take_b = vb > va
                top.append((jnp.where(take_b, vb, va), jnp.where(take_b, ib, ia)))
            store_block(dst_v, dst_i, p, _bitonic_merge(top, True))
            return c
        lax.fori_loop(0, n_pairs, merge, 0)

    src, dst = (va_ref, ia_ref), (vb_ref, ib_ref)
    blocks = n // K
    while blocks > 1:
        merge_round(*src, *dst, blocks // 2)
        src, dst = dst, src
        blocks //= 2
    return ([src[0][k * SUBLANES:(k + 1) * SUBLANES, :] for k in range(K)],
            [src[1][k * SUBLANES:(k + 1) * SUBLANES, :] for k in range(K)])


def _untied(s_ref, n, vals):
    ok = vals[0] > vals[1]
    for a in range(1, PEER_TOPK - 1):
        ok = ok & (vals[a] > vals[a + 1])
    last = vals[-1]

    def count(c, cnt):
        for k in range(SUBLANES):
            cnt = cnt + jnp.where(_tile(s_ref, c * SUBLANES + k) >= last, 1, 0)
        return cnt
    cnt = lax.fori_loop(0, n // SUBLANES, count, jnp.zeros(last.shape, jnp.int32))
    return ok & (cnt == PEER_TOPK)


def _top16_exact_loop(s_ref, n, p_ref):
    shape = (SUBLANES, LANES)
    U = SUBLANES

    def over_rows(fn, init):
        def trip(c, acc):
            for u in range(U):
                acc = fn(c * U + u, acc)
            return acc
        return lax.fori_loop(0, n // U, trip, init)

    vals, picks = [], []
    for _ in range(PEER_TOPK):
        m = over_rows(lambda k, m: jnp.maximum(m, _tile(s_ref, k)), jnp.full(shape, -jnp.inf, F32))
        ix = over_rows(lambda k, ix: jnp.minimum(ix, jnp.where(_tile(s_ref, k) == m, k, n)),
                       jnp.full(shape, n, jnp.int32))

        def remove(k, pick):
            hit = ix == k
            _put(s_ref, k, jnp.where(hit, -jnp.inf, _tile(s_ref, k)))
            return pick if p_ref is None else pick + jnp.where(hit, _tile(p_ref, k), 0)
        pick = over_rows(remove, jnp.zeros(shape, jnp.int32))
        vals.append(m)
        picks.append(ix if p_ref is None else pick)
    return vals, picks


def _top16(s_ref, n, p_ref, bufs):
    vals, tags = _top16_network(s_ref, n, p_ref, *bufs)
    ok = jnp.min(jnp.where(_untied(s_ref, n, vals), 1, 0)) > 0
    pack = lambda v, i: (jnp.stack(v), jnp.stack(i))
    v, i = lax.cond(ok, lambda: pack(vals, tags), lambda: pack(*_top16_exact_loop(s_ref, n, p_ref)))
    return [v[a] for a in range(PEER_TOPK)], [i[a] for a in range(PEER_TOPK)]


N_CAND = 64


def _topk_kernel(sc_ref, off_ref, par_ref, gate_ref, s_ref, p_ref, va_ref, ia_ref, vb_ref, ib_ref, e_ref):
    K = PEER_TOPK
    tm = TOPK_TM
    bufs = (va_ref, ia_ref, vb_ref, ib_ref)
    shape = (SUBLANES, LANES)

    def half_topk(g):
        x = sc_ref[g]
        blocks = [x[:, s * LANES:(s + 1) * LANES].reshape(N_KEYS // SUBLANES, SUBLANES, LANES)
                  for s in range(SUBLANES)]
        s_ref[...] = pltpu.einshape("gskl->gksl", jnp.stack(blocks, axis=1)).reshape(N_KEYS * SUBLANES, LANES)
        return _top16(s_ref, N_KEYS, None, bufs)

    for hd in range(PEER_HEADS):
        s1, i1 = half_topk(2 * hd)
        s2, i2 = half_topk(2 * hd + 1)
        r = 0
        for a in range(K):
            for b in range(K // (a + 1)):
                _put(s_ref, r, s1[a] + s2[b])
                _put(p_ref, r, i1[a] * N_KEYS + i2[b])
                r += 1
        for r in range(r, N_CAND):
            _put(s_ref, r, jnp.full(shape, -jnp.inf, F32))
            _put(p_ref, r, jnp.zeros(shape, jnp.int32))
        best, eidx = _top16(s_ref, N_CAND, p_ref, bufs)
        p = [jnp.exp(bv - best[0]) for bv in best]
        tot = functools.reduce(jnp.add, p)
        for a in range(K):
            e = hd * K + a
            e_ref[0, e * SUBLANES:(e + 1) * SUBLANES, :] = eidx[a].astype(F32)
            e_ref[1, e * SUBLANES:(e + 1) * SUBLANES, :] = p[a] / tot

    def token_major(x):
        t = pltpu.einshape("gesl->gsel", x.reshape(PEER_E // SUBLANES, SUBLANES, SUBLANES, LANES))
        return jnp.concatenate([t[:, s].reshape(PEER_E, LANES).T for s in range(SUBLANES)], axis=0)

    n_half = N_KEYS * N_KEYS // 2
    idx = token_major(e_ref[0]).astype(jnp.int32)
    gate_ref[...] = token_major(e_ref[1])
    par_ref[...] = (idx >= n_half).astype(F32)
    off = (idx & (n_half - 1)) * SUBLANES
    off_ref[...] = pltpu.einshape("abl->bal", off.reshape(tm // PEER_GROUP, PEER_GROUP, PEER_E))


def _topk(scores):
    G, NK, T = scores.shape
    tm = TOPK_TM
    out = pl.BlockSpec((tm, PEER_E), lambda i: (i, 0))
    return pl.pallas_call(
        _topk_kernel,
        grid=(T // tm,),
        in_specs=[pl.BlockSpec((G, NK, tm), lambda i: (0, 0, i))],
        out_specs=[pl.BlockSpec((PEER_GROUP, tm // PEER_GROUP, PEER_E), lambda i: (0, i, 0)), out, out],
        out_shape=[jax.ShapeDtypeStruct((PEER_GROUP, T // PEER_GROUP, PEER_E), jnp.int32),
                   jax.ShapeDtypeStruct((T, PEER_E), F32), jax.ShapeDtypeStruct((T, PEER_E), F32)],
        scratch_shapes=[pltpu.VMEM((N_KEYS * SUBLANES, LANES), F32), pltpu.VMEM((N_CAND * SUBLANES, LANES), jnp.int32),
                        pltpu.VMEM((N_KEYS * SUBLANES, LANES), F32), pltpu.VMEM((N_KEYS * SUBLANES, LANES), jnp.int32),
                        pltpu.VMEM((N_KEYS * SUBLANES // 2, LANES), F32),
                        pltpu.VMEM((N_KEYS * SUBLANES // 2, LANES), jnp.int32),
                        pltpu.VMEM((2, PEER_E * SUBLANES, LANES), F32)],
        compiler_params=_cparams(1),
        name="topk",
    )(scores)


PEER_TM = 8 * PEER_GROUP
TILE_ROWS = 2 * SUBLANES
STACK = PEER_E * TILE_ROWS


def _pack_kernel(lo_ref, hi_ref, o_ref):
    nb = lo_ref.shape[0]
    lo = lax.bitcast_convert_type(lo_ref[...].astype(BF16).astype(F32), jnp.uint32)
    hi = lax.bitcast_convert_type(hi_ref[...].astype(BF16).astype(F32), jnp.uint32)
    word = (hi & jnp.uint32(0xFFFF0000)) | lax.shift_right_logical(lo, jnp.uint32(16))
    for s in range(SUBLANES):
        o_ref[pl.ds(s, nb, stride=SUBLANES), :] = word[:, s * LANES:(s + 1) * LANES]


def _pack_table(w):
    n, d = w.shape
    nb = 256
    steps = n // 2 // nb
    return pl.pallas_call(
        _pack_kernel,
        grid=(steps,),
        in_specs=[pl.BlockSpec((nb, d), lambda i: (i, 0)), pl.BlockSpec((nb, d), lambda i: (i + steps, 0))],
        out_specs=pl.BlockSpec((nb * SUBLANES, LANES), lambda i: (i, 0)),
        out_shape=jax.ShapeDtypeStruct((n // 2 * SUBLANES, LANES), jnp.uint32),
        compiler_params=_cparams(1),
        name="pack",
    )(w, w)


def _peer_consts():
    c = jnp.arange(STACK)
    expand = (c[None, :] // TILE_ROWS == jnp.arange(PEER_E)[:, None])
    fold = ((c[None, :] % TILE_ROWS) // 2 == jnp.arange(SUBLANES)[:, None])
    half = (c % 2).reshape(1, STACK)
    return expand.astype(BF16), expand.T.astype(BF16), fold.astype(F32), half.astype(F32)


def _stack2(tab_ref, off_a, off_b, g):
    def tile(o):
        return pltpu.bitcast(tab_ref[pl.ds(pl.multiple_of(o, SUBLANES), SUBLANES), :], BF16)
    w0 = jnp.concatenate([tile(off_a[g, j]) for j in range(PEER_E)], axis=0)
    w1 = jnp.concatenate([tile(off_b[g, j]) for j in range(PEER_E)], axis=0)
    return jnp.concatenate([w0, w1], axis=1)


def _split2_dot(x, w):
    hi = x.astype(BF16)
    lo = (x - hi.astype(F32)).astype(BF16)
    return _dot(hi, w) + _dot(lo, w)


def _expand_sel(par_ref, expand_ref, half_ref):
    par = _dot(par_ref[...].astype(BF16), expand_ref[...])
    return jnp.where(par == half_ref[...], 1.0, 0.0)


def _fold_rows(x):
    tm = x.shape[0]
    chunks = [x[:, c * LANES:(c + 1) * LANES].reshape(tm // SUBLANES, SUBLANES, LANES) for c in range(SUBLANES)]
    return pltpu.einshape("gctl->gtcl", jnp.stack(chunks, axis=1)).reshape(tm * SUBLANES, LANES)


def _unfold_rows(xf):
    tm = xf.shape[0] // SUBLANES
    t = pltpu.einshape("gtcl->gctl", xf.reshape(tm // SUBLANES, SUBLANES, SUBLANES, LANES))
    return jnp.concatenate([t[:, c].reshape(tm, LANES) for c in range(SUBLANES)], axis=1)


def _peer_u_kernel(*refs):
    offs = refs[:PEER_GROUP]
    (par_ref, x_ref, gate_ref, expand_ref, collapse_ref, fold_ref, half_ref, tab_ref,
     act_ref, sel_ref, g_ref, xf_ref) = refs[PEER_GROUP:]
    tm = x_ref.shape[0]
    sel_ref[...] = _expand_sel(par_ref, expand_ref, half_ref)
    xf_ref[...] = _fold_rows(x_ref[...])
    fold = fold_ref[...]
    zero = jnp.zeros((SUBLANES, LANES), BF16)

    def pair(g, p):
        t = PEER_GROUP * g + 2 * p
        w2 = _stack2(tab_ref, offs[2 * p], offs[2 * p + 1], g)
        x0 = xf_ref[pl.ds(pl.multiple_of(t * SUBLANES, SUBLANES), SUBLANES), :].astype(BF16)
        x1 = xf_ref[pl.ds(pl.multiple_of((t + 1) * SUBLANES, SUBLANES), SUBLANES), :].astype(BF16)
        lhs = jnp.concatenate([jnp.concatenate([x0, zero], axis=1),
                               jnp.concatenate([zero, x1], axis=1)], axis=0)
        gm = _dot_nt(lhs, w2)
        g_ref[pl.ds(t, 1), :] = jnp.sum(gm[0:SUBLANES] * fold, axis=0, keepdims=True) * sel_ref[pl.ds(t, 1), :]
        g_ref[pl.ds(t + 1, 1), :] = (jnp.sum(gm[SUBLANES:] * fold, axis=0, keepdims=True)
                                     * sel_ref[pl.ds(t + 1, 1), :])

    def group(g, _):
        for p in range(PEER_GROUP // 2):
            pair(g, p)
        return 0

    lax.fori_loop(0, tm // PEER_GROUP, group, 0)
    a = _split2_dot(g_ref[...], collapse_ref[...])
    gelu = 0.5 * a * (1.0 + lax.erf(a * (2.0 ** -0.5)))
    act_ref[...] = gate_ref[...] * gelu


def _peer_v_kernel(*refs):
    offs = refs[:PEER_GROUP]
    (par_ref, act_ref, expand_ref, fold_ref, half_ref, tab_ref, x1_ref, mod_ref, g_ref,
     o_ref, ce_ref, yf_ref) = refs[PEER_GROUP:]
    tm, D = x1_ref.shape
    ce_ref[...] = _dot(act_ref[...].astype(BF16), expand_ref[...]) * _expand_sel(par_ref, expand_ref, half_ref)
    fold = fold_ref[...]

    def pair(g, p):
        t = PEER_GROUP * g + 2 * p
        w2 = _stack2(tab_ref, offs[2 * p], offs[2 * p + 1], g)
        coef = jnp.concatenate([(ce_ref[pl.ds(t, 1), :] * fold).astype(BF16),
                                (ce_ref[pl.ds(t + 1, 1), :] * fold).astype(BF16)], axis=0)
        out = _dot(coef, w2)
        yf_ref[pl.ds(pl.multiple_of(t * SUBLANES, SUBLANES), SUBLANES), :] = out[0:SUBLANES, 0:LANES]
        yf_ref[pl.ds(pl.multiple_of((t + 1) * SUBLANES, SUBLANES), SUBLANES), :] = out[SUBLANES:, LANES:]

    def group(g, _):
        for p in range(PEER_GROUP // 2):
            pair(g, p)
        return 0

    lax.fori_loop(0, tm // PEER_GROUP, group, 0)
    y = _unfold_rows(yf_ref[...])
    gate_f = mod_ref[0, :, 5 * D:6 * D]
    o_ref[...] = x1_ref[...] + gate_f * _rms(y, g_ref[...])


def _peer_specs(tab):
    tm = PEER_TM
    smem = [pl.BlockSpec((None, tm // PEER_GROUP, PEER_E), functools.partial(lambda k, i: (k, i, 0), k),
                         memory_space=pltpu.SMEM) for k in range(PEER_GROUP)]
    row = pl.BlockSpec((tm, PEER_E), lambda i: (i, 0))
    const = lambda shape: pl.BlockSpec(shape, lambda i: (0,) * len(shape))
    table = pl.BlockSpec(tab.shape, lambda i: (0, 0), pipeline_mode=pl.Buffered(1))
    return tm, smem, row, const, table


def _peer_u(off, par, h2, gate, tab):
    T, D = h2.shape
    tm, smem, row, const, table = _peer_specs(tab)
    expand, collapse, fold, half = _peer_consts()
    return pl.pallas_call(
        _peer_u_kernel,
        grid=(T // tm,),
        in_specs=smem + [row, pl.BlockSpec((tm, D), lambda i: (i, 0)), row,
                  const(expand.shape), const(collapse.shape), const(fold.shape), const(half.shape), table],
        out_specs=row,
        out_shape=jax.ShapeDtypeStruct((T, PEER_E), F32),
        scratch_shapes=[pltpu.VMEM((tm, STACK), F32), pltpu.VMEM((tm, STACK), F32),
                        pltpu.VMEM((tm * SUBLANES, LANES), F32)],
        compiler_params=_cparams(1),
        name="peer_u",
    )(*([off] * PEER_GROUP), par, h2, gate, expand, collapse, fold, half, tab)


def _peer_v(off, par, act, tab, x1, mod3, g, S):
    T, D = x1.shape
    tm, smem, row, const, table = _peer_specs(tab)
    tpb = S // tm
    expand, _, fold, half = _peer_consts()
    wide = pl.BlockSpec((tm, D), lambda i: (i, 0))
    return pl.pallas_call(
        _peer_v_kernel,
        grid=(T // tm,),
        in_specs=smem + [row, row, const(expand.shape), const(fold.shape), const(half.shape), table,
                  wide, pl.BlockSpec((1, 1, N_MOD * D), lambda i: (i // tpb, 0, 0)), const((1, D))],
        out_specs=wide,
        out_shape=jax.ShapeDtypeStruct((T, D), F32),
        scratch_shapes=[pltpu.VMEM((tm, STACK), F32), pltpu.VMEM((tm * SUBLANES, LANES), F32)],
        compiler_params=_cparams(1),
        name="peer_v",
    )(*([off] * PEER_GROUP), par, act, expand, fold, half, tab, x1, mod3, g)


def kernel(x, c, w_ada, b_ada, pre_mix_g, post_mix_g, w_in, b_fox_f, hgrn_gamma, hgrn_norm_g, w_out, pre_ffn_g,
           post_ffn_g, peer_w_q, peer_sub_keys, peer_u, peer_v):
    B, S, D = x.shape
    T = B * S
    depth = w_in.shape[0]
    assert depth == 1, "single-layer block"
    l = 0
    x2d = x.reshape(T, D)

    mod3 = _ada(c, w_ada[l], b_ada[l]).reshape(B, 1, N_MOD * D)

    w = w_in[l]
    o1 = 3 * FOX_WIDTH
    wqkv = w[:, :o1].astype(BF16)
    wf = jnp.pad(w[:, o1:o1 + FOX_HEADS], ((0, 0), (0, LANES - FOX_HEADS))).astype(BF16)
    bf = jnp.pad(b_fox_f[l].astype(F32), (0, LANES - FOX_HEADS)).reshape(1, LANES)
    wh = w[:, o1 + FOX_HEADS:].astype(BF16)

    qt, ka, vt, hq, hf, hi, hg = _proj(x2d, mod3, pre_mix_g[l].reshape(1, D), wqkv, wf, bf, wh, B, S)

    o_fox = _fox(qt, ka, vt)

    o_h = _hgrn(hq, hf, hi, hg, hgrn_gamma[l:l + 2].astype(F32), hgrn_norm_g[l].reshape(1, HGRN_WIDTH), B, S)

    wo = w_out[l].astype(BF16)
    keys =peer_sub_keys[l].reshape(2 * PEER_HEADS, N_KEYS, PEER_HALF).astype(BF16)
    x1, h2, scores = _outproj(x2d, o_fox.reshape(T, FOX_WIDTH), o_h, mod3, wo, post_mix_g[l].reshape(1, D),
                              pre_ffn_g[l].reshape(1, D), peer_w_q[l].astype(BF16), keys, B, S)

    off, par, gate = _topk(scores)

    act = _peer_u(off, par, h2, gate, _pack_table(peer_u[l]))
    out = _peer_v(off, par, act, _pack_table(peer_v[l]), x1, mod3, post_ffn_g[l].reshape(1, D), S)
    return out.reshape(B, S, D)
```

```python
import functools
import math

import jax
import jax.numpy as jnp
from jax import lax
from jax.experimental import pallas as pl
from jax.experimental.pallas import tpu as pltpu

F32 = jnp.float32
BF16 = jnp.bfloat16
EPS = 1e-6
HIGHEST = lax.Precision.HIGHEST
LOG2E = math.log2(math.e)

FOX_HEADS = 8
FOX_HEAD_DIM = 64
FOX_WIDTH = FOX_HEADS * FOX_HEAD_DIM
HGRN_HEADS = 4
HGRN_HEAD_DIM = 128
HGRN_WIDTH = HGRN_HEADS * HGRN_HEAD_DIM
HGRN_CHUNK = 16
PEER_HEADS = 8
PEER_QUERY_DIM = 256
PEER_HALF = PEER_QUERY_DIM // 2
N_KEYS = 128
PEER_TOPK = 16
PEER_E = PEER_HEADS * PEER_TOPK
PEER_GROUP = 32
N_MOD = 6

LANES = 128
SUBLANES = 8
VMEM_LIMIT = 56 * 1024 * 1024


def _cparams(n_axes, vmem=VMEM_LIMIT):
    return pltpu.CompilerParams(dimension_semantics=("arbitrary",) * n_axes, vmem_limit_bytes=vmem)


def _dot(a, b):
    return jnp.dot(a, b, preferred_element_type=F32)


def _dot_nt(a, b):
    return lax.dot_general(a, b, (((1,), (1,)), ((), ())), preferred_element_type=F32)


def _dot_tn(a, b):
    return lax.dot_general(a, b, (((0,), (0,)), ((), ())), preferred_element_type=F32)


def _rms(x, g):
    return x * lax.rsqrt(jnp.mean(x * x, axis=-1, keepdims=True) + EPS) * g


def _ada_kernel(c_ref, w_ref, b_ref, o_ref):
    c = c_ref[...]
    cond = c * jax.nn.sigmoid(c)
    o_ref[...] = jnp.dot(cond, w_ref[...], preferred_element_type=F32, precision=HIGHEST) + b_ref[...]


def _ada(c, w, b):
    B, D = c.shape
    N = w.shape[1]
    tn = 1024
    return pl.pallas_call(
        _ada_kernel,
        grid=(N // tn,),
        in_specs=[pl.BlockSpec((B, D), lambda j: (0, 0)),
                  pl.BlockSpec((D, tn), lambda j: (0, j)),
                  pl.BlockSpec((1, tn), lambda j: (0, j))],
        out_specs=pl.BlockSpec((B, tn), lambda j: (0, j)),
        out_shape=jax.ShapeDtypeStruct((B, N), F32),
        compiler_params=_cparams(1),
        name="ada",
    )(c, w, b.reshape(1, N))


def _split3(f):
    hi = f.astype(BF16).astype(F32)
    r = f - hi
    mid = r.astype(BF16).astype(F32)
    return hi, mid, r - mid


def _proj_kernel(x_ref, mod_ref, g_ref, wqkv_ref, wf_ref, bf_ref, wh_ref, tri_ref,
                 qt_ref, ka_ref, vt_ref, hq_ref, hf_ref, hi_ref, hg_ref, carry_ref,
                 *, tiles_per_batch):
    i = pl.program_id(0)
    D = x_ref.shape[1]
    tm = x_ref.shape[0]
    Dh = FOX_HEAD_DIM
    x = x_ref[...]
    shift = mod_ref[0, :, 0:D]
    scale = mod_ref[0, :, D:2 * D]
    h = (_rms(x, g_ref[...]) * (1.0 + scale) + shift).astype(BF16)

    ff = _dot(h, wf_ref[...]) + bf_ref[...]
    logf = jnp.minimum(ff, 0.0) - jnp.log(1.0 + jnp.exp(-jnp.abs(ff)))

    @pl.when(i % tiles_per_batch == 0)
    def _():
        carry_ref[...] = jnp.zeros_like(carry_ref)

    cs = jnp.dot(tri_ref[...], logf, preferred_element_type=F32, precision=HIGHEST) + carry_ref[...]
    carry_ref[...] = cs[tm - 1:tm, :]

    qkv = _dot(h, wqkv_ref[...])
    lane = lax.broadcasted_iota(jnp.int32, (tm, Dh), 1)
    zpad = jnp.zeros((tm, Dh), F32)
    for hd in range(FOX_HEADS):
        lo = hd * Dh
        fhi, fmid, flo = _split3(cs[:, hd:hd + 1] * LOG2E)
        pieces = lambda o: jnp.where(lane == o, fhi, jnp.where(lane == o + 1, fmid, jnp.where(lane == o + 2, flo, 0.0)))
        q_aux = jnp.where(lane < 3, -1.0, pieces(3))
        k_aux = jnp.where((lane >= 3) & (lane < 6), 1.0, pieces(0))
        qa = jnp.concatenate([qkv[:, lo:lo + Dh] * (Dh ** -0.5 * LOG2E), q_aux], axis=1)
        ka = jnp.concatenate([qkv[:, FOX_WIDTH + lo:FOX_WIDTH + lo + Dh], k_aux], axis=1)
        va = jnp.concatenate([qkv[:, 2 * FOX_WIDTH + lo:2 * FOX_WIDTH + lo + Dh], zpad], axis=1)
        qt_ref[0, hd] = qa.T.astype(BF16)
        ka_ref[0, hd] = ka.astype(BF16)
        vt_ref[0, hd] = va.T[0:Dh, :].astype(BF16)

    hh = _dot(h, wh_ref[...])
    hq_ref[...] = hh[:, 0:HGRN_WIDTH]
    hf_ref[...] = hh[:, HGRN_WIDTH:2 * HGRN_WIDTH]
    hi_ref[...] = hh[:, 2 * HGRN_WIDTH:3 * HGRN_WIDTH]
    hg_ref[...] = hh[:, 3 * HGRN_WIDTH:4 * HGRN_WIDTH]


def _proj(x2d, mod3, g, wqkv, wf, bf, wh, B, S):
    T, D = x2d.shape
    tm = 256
    tpb = S // tm
    tri = (jnp.arange(tm)[:, None] >= jnp.arange(tm)[None, :]).astype(F32)
    H, Dh = FOX_HEADS, FOX_HEAD_DIM
    fox_shapes = [jax.ShapeDtypeStruct((B, H, LANES, S), BF16), jax.ShapeDtypeStruct((B, H, S, LANES), BF16),
                  jax.ShapeDtypeStruct((B, H, Dh, S), BF16)]
    fox_specs = [pl.BlockSpec((1, H, LANES, tm), lambda i: (i // tpb, 0, 0, i % tpb)),
                 pl.BlockSpec((1, H, tm, LANES), lambda i: (i // tpb, 0, i % tpb, 0)),
                 pl.BlockSpec((1, H, Dh, tm), lambda i: (i // tpb, 0, 0, i % tpb))]
    wide = jax.ShapeDtypeStruct((T, HGRN_WIDTH), F32)
    wspec = pl.BlockSpec((tm, HGRN_WIDTH), lambda i: (i, 0))
    const = lambda shape: pl.BlockSpec(shape, lambda i: (0,) * len(shape))
    return pl.pallas_call(
        functools.partial(_proj_kernel, tiles_per_batch=tpb),
        grid=(T // tm,),
        in_specs=[pl.BlockSpec((tm, D), lambda i: (i, 0)),
                  pl.BlockSpec((1, 1, N_MOD * D), lambda i: (i // tpb, 0, 0)),
                  const((1, D)), const(wqkv.shape), const(wf.shape), const(bf.shape), const(wh.shape),
                  const((tm, tm))],
        out_specs=fox_specs + [wspec, wspec, wspec, wspec],
        out_shape=fox_shapes + [wide, wide, wide, wide],
        scratch_shapes=[pltpu.VMEM((1, LANES), F32)],
        compiler_params=_cparams(1),
        name="proj",
    )(x2d, mod3, g, wqkv, wf, bf, wh, tri)


FOX_HB = 8


def _fox_kernel(qt_ref, ka_ref, vt_ref, o_ref, m_ref, l_ref, acc_ref, *, tq):
    qi = pl.program_id(2)
    m_ref[...] = jnp.full_like(m_ref, -jnp.inf)
    l_ref[...] = jnp.zeros_like(l_ref)
    acc_ref[...] = jnp.zeros_like(acc_ref)

    def step(j, masked):
        off = pl.multiple_of(j * tq, tq)
        hs = range(FOX_HB)
        s = [_dot(ka_ref[0, hh, pl.ds(off, tq), :], qt_ref[0, hh]) for hh in hs]
        if masked:
            key = lax.broadcasted_iota(jnp.int32, (tq, tq), 0)
            qry = lax.broadcasted_iota(jnp.int32, (tq, tq), 1)
            s = [jnp.where(key <= qry, sh, -jnp.inf) for sh in s]
        m_old = [m_ref[hh] for hh in hs]
        m_new = [jnp.maximum(m_old[hh], jnp.max(s[hh], axis=0, keepdims=True)) for hh in hs]
        p = [jnp.exp2(s[hh] - m_new[hh]) for hh in hs]
        alpha = [jnp.exp2(m_old[hh] - m_new[hh]) for hh in hs]
        pv = [_dot(vt_ref[0, hh, :, pl.ds(off, tq)], p[hh].astype(BF16)) for hh in hs]
        for hh in hs:
            l_ref[hh] = alpha[hh] * l_ref[hh] + jnp.sum(p[hh], axis=0, keepdims=True)
            acc_ref[hh] = alpha[hh] * acc_ref[hh] + pv[hh]
            m_ref[hh] = m_new[hh]

    step(qi, True)

    def body(j, _):
        step(j, False)
        return 0

    lax.fori_loop(0, qi, body, 0)
    o = [(acc_ref[hh] / l_ref[hh]).T for hh in range(FOX_HB)]
    o_ref[0] = jnp.concatenate(o, axis=1).astype(o_ref.dtype)


def _fox(qt, ka, vt):
    B, H, S, _ = ka.shape
    Dh = vt.shape[2]
    tq = 512
    hb = FOX_HB
    return pl.pallas_call(
        functools.partial(_fox_kernel, tq=tq),
        grid=(B, H // hb, S // tq),
        in_specs=[pl.BlockSpec((1, hb, LANES, tq), lambda b, h, i: (b, h, 0, i)),
                  pl.BlockSpec((1, hb, S, LANES), lambda b, h, i: (b, h, 0, 0)),
                  pl.BlockSpec((1, hb, Dh, S), lambda b, h, i: (b, h, 0, 0))],
        out_specs=pl.BlockSpec((1, tq, hb * Dh), lambda b, h, i: (b, i, h)),
        out_shape=jax.ShapeDtypeStruct((B, S, H * Dh), BF16),
        scratch_shapes=[pltpu.VMEM((hb, 1, tq), F32), pltpu.VMEM((hb, 1, tq), F32), pltpu.VMEM((hb, Dh, tq), F32)],
        compiler_params=_cparams(3),
        name="fox",
    )(qt, ka, vt)


def _hgrn_kernel(hq_ref, hf_ref, hi_ref, hg_ref, gam_ref, ng_ref, bd_ref, bl_ref, o_ref, st_ref, os_ref):
    j = pl.program_id(1)
    tb = hq_ref.shape[0]
    C = HGRN_CHUNK
    Dk = HGRN_HEAD_DIM

    @pl.when(j == 0)
    def _():
        st_ref[...] = jnp.zeros_like(st_ref)

    gam = gam_ref[...]
    e = jnp.exp(gam - jnp.max(gam, axis=0, keepdims=True))
    lb = e[0:1, :] / jnp.sum(e, axis=0, keepdims=True)
    f = lb + (1.0 - lb) * jax.nn.sigmoid(hf_ref[...])
    lf = jnp.log(f)
    A = jnp.dot(bd_ref[...], lf, preferred_element_type=F32, precision=HIGHEST)
    AL = jnp.dot(bl_ref[...], lf, preferred_element_type=F32, precision=HIGHEST)
    q = hq_ref[...] * (Dk ** -0.5)
    kk = 1.0 - f
    qa = q * jnp.exp(A)
    ka = kk * jnp.exp(AL - A)
    eal = jnp.exp(AL)
    inp = hi_ref[...]

    tri3 = (lax.broadcasted_iota(jnp.int32, (C, C, Dk), 0) >= lax.broadcasted_iota(jnp.int32, (C, C, Dk), 1))
    heads = range(HGRN_HEADS)
    chunks = range(tb // C)
    blk = lambda a, c, hd: a[c * C:(c + 1) * C, hd * Dk:(hd + 1) * Dk]
    intra = {}
    for c in chunks:
        for hd in heads:
            Ac = blk(A, c, hd)
            diff = Ac[:, None, :] - Ac[None, :, :]
            dec = jnp.exp(jnp.where(tri3, diff, -jnp.inf))
            sc = jnp.sum(blk(q, c, hd)[:, None, :] * blk(kk, c, hd)[None, :, :] * dec, axis=-1)
            intra[c, hd] = _dot(sc, blk(inp, c, hd))
    st = [st_ref[hd] for hd in heads]
    for c in chunks:
        for hd in heads:
            os_ref[c * C:(c + 1) * C, hd * Dk:(hd + 1) * Dk] = _dot_nt(blk(qa, c, hd), st[hd]) + intra[c, hd]
            st[hd] = st[hd] * blk(eal, c, hd)[0:1, :] + _dot_tn(blk(inp, c, hd), blk(ka, c, hd))
    for hd in heads:
        st_ref[hd] = st[hd]

    o_all = os_ref[...]
    hg = hg_ref[...]
    ng = ng_ref[...]
    for hd in range(HGRN_HEADS):
        lo = hd * Dk
        oh = _rms(o_all[:, lo:lo + Dk], ng[:, lo:lo + Dk])
        g = hg[:, lo:lo + Dk]
        o_ref[:, lo:lo + Dk] = (oh * (g * jax.nn.sigmoid(g))).astype(o_ref.dtype)


def _hgrn(hq, hf, hi, hg, gamma, norm_g, B, S):
    T, W = hq.shape
    tb = 128
    nb = S // tb
    t = jnp.arange(tb)
    same = (t[:, None] // HGRN_CHUNK) == (t[None, :] // HGRN_CHUNK)
    bd = (same & (t[:, None] >= t[None, :])).astype(F32)
    bl = same.astype(F32)
    wspec = pl.BlockSpec((tb, W), lambda b, j: (b * nb + j, 0))
    const = lambda shape: pl.BlockSpec(shape, lambda b, j: (0,) * len(shape))
    return pl.pallas_call(
        _hgrn_kernel,
        grid=(B, nb),
        in_specs=[wspec, wspec, wspec, wspec, const(gamma.shape), const((1, W)), const((tb, tb)), const((tb, tb))],
        out_specs=wspec,
        out_shape=jax.ShapeDtypeStruct((T, W), BF16),
        scratch_shapes=[pltpu.VMEM((HGRN_HEADS, HGRN_HEAD_DIM, HGRN_HEAD_DIM), F32), pltpu.VMEM((tb, W), F32)],
        compiler_params=_cparams(2),
        name="hgrn",
    )(hq, hf, hi, hg, gamma, norm_g, bd, bl)


def _outproj_kernel(x_ref, of_ref, oh_ref, mod_ref, wo_ref, pmg_ref, pfg_ref, wq_ref, keys_ref,
                    x1_ref, h2_ref, sc_ref):
    D = x_ref.shape[1]
    mix = _dot(jnp.concatenate([of_ref[...], oh_ref[...]], axis=1), wo_ref[...])
    gate_a = mod_ref[0, :, 2 * D:3 * D]
    shift_f = mod_ref[0, :, 3 * D:4 * D]
    scale_f = mod_ref[0, :, 4 * D:5 * D]
    x1 = x_ref[...] + gate_a * _rms(mix, pmg_ref[...])
    x1_ref[...] = x1
    h2 = _rms(x1, pfg_ref[...]) * (1.0 + scale_f) + shift_f
    h2_ref[...] = h2
    qp = _dot(h2.astype(BF16), wq_ref[...])
    for g in range(2 * PEER_HEADS):
        qg = qp[:, g * PEER_HALF:(g + 1) * PEER_HALF]
        qn = qg * lax.rsqrt(jnp.mean(qg * qg, axis=-1, keepdims=True) + EPS)
        sc_ref[g] = _dot_nt(keys_ref[g], qn.astype(BF16))


def _outproj(x2d, ofox, oh, mod3, wo, pmg, pfg, wq, keys, B, S):
    T, D = x2d.shape
    tm = 512
    tpb = S // tm
    G = 2 * PEER_HEADS
    const = lambda shape: pl.BlockSpec(shape, lambda i: (0,) * len(shape))
    row = pl.BlockSpec((tm, D), lambda i: (i, 0))
    return pl.pallas_call(
        _outproj_kernel,
        grid=(T // tm,),
        in_specs=[row,
                  pl.BlockSpec((tm, FOX_WIDTH), lambda i: (i, 0)),
                  pl.BlockSpec((tm, HGRN_WIDTH), lambda i: (i, 0)),
                  pl.BlockSpec((1, 1, N_MOD * D), lambda i: (i // tpb, 0, 0)),
                  const(wo.shape), const((1, D)), const((1, D)), const(wq.shape), const(keys.shape)],
        out_specs=[row, row, pl.BlockSpec((G, N_KEYS, tm), lambda i: (0, 0, i))],
        out_shape=[jax.ShapeDtypeStruct((T, D), F32), jax.ShapeDtypeStruct((T, D), F32),
                   jax.ShapeDtypeStruct((G, N_KEYS, T), F32)],
        compiler_params=_cparams(1),
        name="outproj",
    )(x2d, ofox, oh, mod3, wo, pmg, pfg, wq, keys)


TOPK_TM = SUBLANES * LANES


def _cx(a, b, desc):
    (va, ia), (vb, ib) = a, b
    swap = (vb > va) if desc else (vb < va)
    return ((jnp.where(swap, vb, va), jnp.where(swap, ib, ia)),
            (jnp.where(swap, va, vb), jnp.where(swap, ia, ib)))


def _bitonic_merge(x, desc):
    n = len(x)
    x = list(x)
    d = n // 2
    while d >= 1:
        for i in range(n):
            if (i // d) % 2 == 0:
                x[i], x[i + d] = _cx(x[i], x[i + d], desc)
        d //= 2
    return x


def _bitonic_sort(x, desc):
    n = len(x)
    if n == 1:
        return list(x)
    lo = _bitonic_sort(x[:n // 2], True)
    hi = _bitonic_sort(x[n // 2:], False)
    return _bitonic_merge(lo + hi, desc)


def _tile(ref, row):
    return ref[pl.ds(pl.multiple_of(row * SUBLANES, SUBLANES), SUBLANES), :]


def _put(ref, row, x):
    ref[pl.ds(pl.multiple_of(row * SUBLANES, SUBLANES), SUBLANES), :] = x


def _top16_network(s_ref, n, p_ref, va_ref, ia_ref, vb_ref, ib_ref):
    K = PEER_TOPK
    shape = (SUBLANES, LANES)

    def store_block(v_ref, i_ref, blk, pairs):
        rev = blk % 2
        for k, (v, i) in enumerate(pairs):
            pos = blk * K + k + rev * (K - 1 - 2 * k)
            _put(v_ref, pos, v)
            _put(i_ref, pos, i)

    def sort_block(blk, c):
        tag = (lambda r: jnp.full(shape, r, jnp.int32)) if p_ref is None else (lambda r: _tile(p_ref, r))
        pairs = [(_tile(s_ref, blk * K + k), tag(blk * K + k)) for k in range(K)]
        store_block(va_ref, ia_ref, blk, _bitonic_sort(pairs, True))
        return c

    lax.fori_loop(0, n // K, sort_block, 0)

    def merge_round(src_v, src_i, dst_v, dst_i, n_pairs):
        def merge(p, c):
            top = []
            for k in range(K):
                va, ia = _tile(src_v, 2 * p * K + k), _tile(src_i, 2 * p * K + k)
                vb, ib = _tile(src_v, (2 * p + 1) * K + k), _tile(src_i, (2 * p + 1) * K + k)
                take_b = vb > va
                top.append((jnp.where(take_b, vb, va), jnp.where(take_b, ib, ia)))
            store_block(dst_v, dst_i, p, _bitonic_merge(top, True))
            return c
        lax.fori_loop(0, n_pairs, merge, 0)

    src, dst = (va_ref, ia_ref), (vb_ref, ib_ref)
    blocks = n // K
    while blocks > 1:
        merge_round(*src, *dst, blocks // 2)
        src, dst = dst, src
        blocks //= 2
    return ([src[0][k * SUBLANES:(k + 1) * SUBLANES, :] for k in range(K)],
            [src[1][k * SUBLANES:(k + 1) * SUBLANES, :] for k in range(K)])


def _untied(s_ref, n, vals):
    ok = vals[0] > vals[1]
    for a in range(1, PEER_TOPK - 1):
        ok = ok & (vals[a] > vals[a + 1])
    last = vals[-1]

    def count(c, cnt):
        for k in range(SUBLANES):
            cnt = cnt + jnp.where(_tile(s_ref, c * SUBLANES + k) >= last, 1, 0)
        return cnt
    cnt = lax.fori_loop(0, n // SUBLANES, count, jnp.zeros(last.shape, jnp.int32))
    return ok & (cnt == PEER_TOPK)


def _top16_exact_loop(s_ref, n, p_ref):
    shape = (SUBLANES, LANES)
    U = SUBLANES

    def over_rows(fn, init):
        def trip(c, acc):
            for u in range(U):
                acc = fn(c * U + u, acc)
            return acc
        return lax.fori_loop(0, n // U, trip, init)

    vals, picks = [], []
    for _ in range(PEER_TOPK):
        m = over_rows(lambda k, m: jnp.maximum(m, _tile(s_ref, k)), jnp.full(shape, -jnp.inf, F32))
        ix = over_rows(lambda k, ix: jnp.minimum(ix, jnp.where(_tile(s_ref, k) == m, k, n)),
                       jnp.full(shape, n, jnp.int32))

        def remove(k, pick):
            hit = ix == k
            _put(s_ref, k, jnp.where(hit, -jnp.inf, _tile(s_ref, k)))
            return pick if p_ref is None else pick + jnp.where(hit, _tile(p_ref, k), 0)
        pick = over_rows(remove, jnp.zeros(shape, jnp.int32))
        vals.append(m)
        picks.append(ix if p_ref is None else pick)
    return vals, picks


def _top16(s_ref, n, p_ref, bufs):
    vals, tags = _top16_network(s_ref, n, p_ref, *bufs)
    ok = jnp.min(jnp.where(_untied(s_ref, n, vals), 1, 0)) > 0
    pack = lambda v, i: (jnp.stack(v), jnp.stack(i))
    v, i = lax.cond(ok, lambda: pack(vals, tags), lambda: pack(*_top16_exact_loop(s_ref, n, p_ref)))
    return [v[a] for a in range(PEER_TOPK)], [i[a] for a in range(PEER_TOPK)]


N_CAND = 64


def _topk_kernel(sc_ref, off_ref, par_ref, gate_ref, s_ref, p_ref, va_ref, ia_ref, vb_ref, ib_ref, e_ref):
    K = PEER_TOPK
    tm = TOPK_TM
    bufs = (va_ref, ia_ref, vb_ref, ib_ref)
    shape = (SUBLANES, LANES)

    def half_topk(g):
        x = sc_ref[g]
        blocks = [x[:, s * LANES:(s + 1) * LANES].reshape(N_KEYS // SUBLANES, SUBLANES, LANES)
                  for s in range(SUBLANES)]
        s_ref[...] = pltpu.einshape("gskl->gksl", jnp.stack(blocks, axis=1)).reshape(N_KEYS * SUBLANES, LANES)
        return _top16(s_ref, N_KEYS, None, bufs)

    for hd in range(PEER_HEADS):
        s1, i1 = half_topk(2 * hd)
        s2, i2 = half_topk(2 * hd + 1)
        r = 0
        for a in range(K):
            for b in range(K // (a + 1)):
                _put(s_ref, r, s1[a] + s2[b])
                _put(p_ref, r, i1[a] * N_KEYS + i2[b])
                r += 1
        for r in range(r, N_CAND):
            _put(s_ref, r, jnp.full(shape, -jnp.inf, F32))
            _put(p_ref, r, jnp.zeros(shape, jnp.int32))
        best, eidx = _top16(s_ref, N_CAND, p_ref, bufs)
        p = [jnp.exp(bv - best[0]) for bv in best]
        tot = functools.reduce(jnp.add, p)
        for a in range(K):
            e = hd * K + a
            e_ref[0, e * SUBLANES:(e + 1) * SUBLANES, :] = eidx[a].astype(F32)
            e_ref[1, e * SUBLANES:(e + 1) * SUBLANES, :] = p[a] / tot

    def token_major(x):
        t = pltpu.einshape("gesl->gsel", x.reshape(PEER_E // SUBLANES, SUBLANES, SUBLANES, LANES))
        return jnp.concatenate([t[:, s].reshape(PEER_E, LANES).T for s in range(SUBLANES)], axis=0)

    n_half = N_KEYS * N_KEYS // 2
    idx = token_major(e_ref[0]).astype(jnp.int32)
    gate_ref[...] = token_major(e_ref[1])
    par_ref[...] = (idx >= n_half).astype(F32)
    off = (idx & (n_half - 1)) * SUBLANES
    off_ref[...] = pltpu.einshape("abl->bal", off.reshape(tm // PEER_GROUP, PEER_GROUP, PEER_E))


def _topk(scores):
    G, NK, T = scores.shape
    tm = TOPK_TM
    out = pl.BlockSpec((tm, PEER_E), lambda i: (i, 0))
    return pl.pallas_call(
        _topk_kernel,
        grid=(T // tm,),
        in_specs=[pl.BlockSpec((G, NK, tm), lambda i: (0, 0, i))],
        out_specs=[pl.BlockSpec((PEER_GROUP, tm // PEER_GROUP, PEER_E), lambda i: (0, i, 0)), out, out],
        out_shape=[jax.ShapeDtypeStruct((PEER_GROUP, T // PEER_GROUP, PEER_E), jnp.int32),
                   jax.ShapeDtypeStruct((T, PEER_E), F32), jax.ShapeDtypeStruct((T, PEER_E), F32)],
        scratch_shapes=[pltpu.VMEM((N_KEYS * SUBLANES, LANES), F32), pltpu.VMEM((N_CAND * SUBLANES, LANES), jnp.int32),
                        pltpu.VMEM((N_KEYS * SUBLANES, LANES), F32), pltpu.VMEM((N_KEYS * SUBLANES, LANES), jnp.int32),
                        pltpu.VMEM((N_KEYS * SUBLANES // 2, LANES), F32),
                        pltpu.VMEM((N_KEYS * SUBLANES // 2, LANES), jnp.int32),
                        pltpu.VMEM((2, PEER_E * SUBLANES, LANES), F32)],
        compiler_params=_cparams(1),
        name="topk",
    )(scores)


PEER_TM = 8 * PEER_GROUP
TILE_ROWS = 2 * SUBLANES
STACK = PEER_E * TILE_ROWS


def _pack_kernel(lo_ref, hi_ref, o_ref):
    nb = lo_ref.shape[0]
    lo = lax.bitcast_convert_type(lo_ref[...].astype(BF16).astype(F32), jnp.uint32)
    hi = lax.bitcast_convert_type(hi_ref[...].astype(BF16).astype(F32), jnp.uint32)
    word = (hi & jnp.uint32(0xFFFF0000)) | lax.shift_right_logical(lo, jnp.uint32(16))
    for s in range(SUBLANES):
        o_ref[pl.ds(s, nb, stride=SUBLANES), :] = word[:, s * LANES:(s + 1) * LANES]


def _pack_table(w):
    n, d = w.shape
    nb = 256
    steps = n // 2 // nb
    return pl.pallas_call(
        _pack_kernel,
        grid=(steps,),
        in_specs=[pl.BlockSpec((nb, d), lambda i: (i, 0)), pl.BlockSpec((nb, d), lambda i: (i + steps, 0))],
        out_specs=pl.BlockSpec((nb * SUBLANES, LANES), lambda i: (i, 0)),
        out_shape=jax.ShapeDtypeStruct((n // 2 * SUBLANES, LANES), jnp.uint32),
        compiler_params=_cparams(1),
        name="pack",
    )(w, w)


def _peer_consts():
    c = jnp.arange(STACK)
    expand = (c[None, :] // TILE_ROWS == jnp.arange(PEER_E)[:, None])
    fold = ((c[None, :] % TILE_ROWS) // 2 == jnp.arange(SUBLANES)[:, None])
    half = (c % 2).reshape(1, STACK)
    return expand.astype(BF16), expand.T.astype(BF16), fold.astype(F32), half.astype(F32)


def _stack2(tab_ref, off_a, off_b, g):
    def tile(o):
        return pltpu.bitcast(tab_ref[pl.ds(pl.multiple_of(o, SUBLANES), SUBLANES), :], BF16)
    w0 = jnp.concatenate([tile(off_a[g, j]) for j in range(PEER_E)], axis=0)
    w1 = jnp.concatenate([tile(off_b[g, j]) for j in range(PEER_E)], axis=0)
    return jnp.concatenate([w0, w1], axis=1)


def _split2_dot(x, w):
    hi = x.astype(BF16)
    lo = (x - hi.astype(F32)).astype(BF16)
    return _dot(hi, w) + _dot(lo, w)


def _expand_sel(par_ref, expand_ref, half_ref):
    par = _dot(par_ref[...].astype(BF16), expand_ref[...])
    return jnp.where(par == half_ref[...], 1.0, 0.0)


def _fold_rows(x):
    tm = x.shape[0]
    chunks = [x[:, c * LANES:(c + 1) * LANES].reshape(tm // SUBLANES, SUBLANES, LANES) for c in range(SUBLANES)]
    return pltpu.einshape("gctl->gtcl", jnp.stack(chunks, axis=1)).reshape(tm * SUBLANES, LANES)


def _unfold_rows(xf):
    tm = xf.shape[0] // SUBLANES
    t = pltpu.einshape("gtcl->gctl", xf.reshape(tm // SUBLANES, SUBLANES, SUBLANES, LANES))
    return jnp.concatenate([t[:, c].reshape(tm, LANES) for c in range(SUBLANES)], axis=1)


def _peer_u_kernel(*refs):
    offs = refs[:PEER_GROUP]
    (par_ref, x_ref, gate_ref, expand_ref, collapse_ref, fold_ref, half_ref, tab_ref,
     act_ref, sel_ref, g_ref, xf_ref) = refs[PEER_GROUP:]
    tm = x_ref.shape[0]
    sel_ref[...] = _expand_sel(par_ref, expand_ref, half_ref)
    xf_ref[...] = _fold_rows(x_ref[...])
    fold = fold_ref[...]
    zero = jnp.zeros((SUBLANES, LANES), BF16)

    def pair(g, p):
        t = PEER_GROUP * g + 2 * p
        w2 = _stack2(tab_ref, offs[2 * p], offs[2 * p + 1], g)
        x0 = xf_ref[pl.ds(pl.multiple_of(t * SUBLANES, SUBLANES), SUBLANES), :].astype(BF16)
        x1 = xf_ref[pl.ds(pl.multiple_of((t + 1) * SUBLANES, SUBLANES), SUBLANES), :].astype(BF16)
        lhs = jnp.concatenate([jnp.concatenate([x0, zero], axis=1),
                               jnp.concatenate([zero, x1], axis=1)], axis=0)
        gm = _dot_nt(lhs, w2)
        g_ref[pl.ds(t, 1), :] = jnp.sum(gm[0:SUBLANES] * fold, axis=0, keepdims=True) * sel_ref[pl.ds(t, 1), :]
        g_ref[pl.ds(t + 1, 1), :] = (jnp.sum(gm[SUBLANES:] * fold, axis=0, keepdims=True)
                                     * sel_ref[pl.ds(t + 1, 1), :])

    def group(g, _):
        for p in range(PEER_GROUP // 2):
            pair(g, p)
        return 0

    lax.fori_loop(0, tm // PEER_GROUP, group, 0, unroll=2)
    a = _split2_dot(g_ref[...], collapse_ref[...])
    gelu = 0.5 * a * (1.0 + lax.erf(a * (2.0 ** -0.5)))
    act_ref[...] = gate_ref[...] * gelu


def _peer_v_kernel(*refs):
    offs = refs[:PEER_GROUP]
    (par_ref, act_ref, expand_ref, fold_ref, half_ref, tab_ref, x1_ref, mod_ref, g_ref,
     o_ref, ce_ref, yf_ref) = refs[PEER_GROUP:]
    tm, D = x1_ref.shape
    ce_ref[...] = _dot(act_ref[...].astype(BF16), expand_ref[...]) * _expand_sel(par_ref, expand_ref, half_ref)
    fold = fold_ref[...]

    def pair(g, p):
        t = PEER_GROUP * g + 2 * p
        w2 = _stack2(tab_ref, offs[2 * p], offs[2 * p + 1], g)
        coef = jnp.concatenate([(ce_ref[pl.ds(t, 1), :] * fold).astype(BF16),
                                (ce_ref[pl.ds(t + 1, 1), :] * fold).astype(BF16)], axis=0)
        out = _dot(coef, w2)
        yf_ref[pl.ds(pl.multiple_of(t * SUBLANES, SUBLANES), SUBLANES), :] = out[0:SUBLANES, 0:LANES]
        yf_ref[pl.ds(pl.multiple_of((t + 1) * SUBLANES, SUBLANES), SUBLANES), :] = out[SUBLANES:, LANES:]

    def group(g, _):
        for p in range(PEER_GROUP // 2):
            pair(g, p)
        return 0

    lax.fori_loop(0, tm // PEER_GROUP, group, 0)
    y = _unfold_rows(yf_ref[...])
    gate_f = mod_ref[0, :, 5 * D:6 * D]
    o_ref[...] = x1_ref[...] + gate_f * _rms(y, g_ref[...])


def _peer_specs(tab):
    tm = PEER_TM
    smem = [pl.BlockSpec((None, tm // PEER_GROUP, PEER_E), functools.partial(lambda k, i: (k, i, 0), k),
                         memory_space=pltpu.SMEM) for k in range(PEER_GROUP)]
    row = pl.BlockSpec((tm, PEER_E), lambda i: (i, 0))
    const = lambda shape: pl.BlockSpec(shape, lambda i: (0,) * len(shape))
    table = pl.BlockSpec(tab.shape, lambda i: (0, 0), pipeline_mode=pl.Buffered(1))
    return tm, smem, row, const, table


def _peer_u(off, par, h2, gate, tab):
    T, D = h2.shape
    tm, smem, row, const, table = _peer_specs(tab)
    expand, collapse, fold, half = _peer_consts()
    return pl.pallas_call(
        _peer_u_kernel,
        grid=(T // tm,),
        in_specs=smem + [row, pl.BlockSpec((tm, D), lambda i: (i, 0)), row,
                  const(expand.shape), const(collapse.shape), const(fold.shape), const(half.shape), table],
        out_specs=row,
        out_shape=jax.ShapeDtypeStruct((T, PEER_E), F32),
        scratch_shapes=[pltpu.VMEM((tm, STACK), F32), pltpu.VMEM((tm, STACK), F32),
                        pltpu.VMEM((tm * SUBLANES, LANES), F32)],
        compiler_params=_cparams(1),
        name="peer_u",
    )(*([off] * PEER_GROUP), par, h2, gate, expand, collapse, fold, half, tab)


def _peer_v(off, par, act, tab, x1, mod3, g, S):
    T, D = x1.shape
    tm, smem, row, const, table = _peer_specs(tab)
    tpb = S // tm
    expand, _, fold, half = _peer_consts()
    wide = pl.BlockSpec((tm, D), lambda i: (i, 0))
    return pl.pallas_call(
        _peer_v_kernel,
        grid=(T // tm,),
        in_specs=smem + [row, row, const(expand.shape), const(fold.shape), const(half.shape), table,
                  wide, pl.BlockSpec((1, 1, N_MOD * D), lambda i: (i // tpb, 0, 0)), const((1, D))],
        out_specs=wide,
        out_shape=jax.ShapeDtypeStruct((T, D), F32),
        scratch_shapes=[pltpu.VMEM((tm, STACK), F32), pltpu.VMEM((tm * SUBLANES, LANES), F32)],
        compiler_params=_cparams(1),
        name="peer_v",
    )(*([off] * PEER_GROUP), par, act, expand, fold, half, tab, x1, mod3, g)


def kernel(x, c, w_ada, b_ada, pre_mix_g, post_mix_g, w_in, b_fox_f, hgrn_gamma, hgrn_norm_g, w_out, pre_ffn_g,
           post_ffn_g, peer_w_q, peer_sub_keys, peer_u, peer_v):
    B, S, D = x.shape
    T = B * S
    depth = w_in.shape[0]
    assert depth == 1, "single-layer block"
    l = 0
    x2d = x.reshape(T, D)

    mod3 = _ada(c, w_ada[l], b_ada[l]).reshape(B, 1, N_MOD * D)

    w = w_in[l]
    o1 = 3 * FOX_WIDTH
    wqkv = w[:, :o1].astype(BF16)
    wf = jnp.pad(w[:, o1:o1 + FOX_HEADS], ((0, 0), (0, LANES - FOX_HEADS))).astype(BF16)
    bf = jnp.pad(b_fox_f[l].astype(F32), (0, LANES - FOX_HEADS)).reshape(1, LANES)
    wh = w[:, o1 + FOX_HEADS:].astype(BF16)

    qt, ka, vt, hq, hf, hi, hg = _proj(x2d, mod3, pre_mix_g[l].reshape(1, D), wqkv, wf, bf, wh, B, S)

    o_fox = _fox(qt, ka, vt)

    o_h = _hgrn(hq, hf, hi, hg, hgrn_gamma[l:l + 2].astype(F32), hgrn_norm_g[l].reshape(1, HGRN_WIDTH), B, S)

    wo = w_out[l].astype(BF16)
    keys =peer_sub_keys[l].reshape(2 * PEER_HEADS, N_KEYS, PEER_HALF).astype(BF16)
    x1, h2, scores = _outproj(x2d, o_fox.reshape(T, FOX_WIDTH), o_h, mod3, wo, post_mix_g[l].reshape(1, D),
                              pre_ffn_g[l].reshape(1, D), peer_w_q[l].astype(BF16), keys, B, S)

    off, par, gate = _topk(scores)

    act = _peer_u(off, par, h2, gate, _pack_table(peer_u[l]))
    out = _peer_v(off, par, act, _pack_table(peer_v[l]), x1, mod3, post_ffn_g[l].reshape(1, D), S)
    return out.reshape(B, S, D)
```

```python
import functools
import math

import jax
import jax.numpy as jnp
from jax import lax
from jax.experimental import pallas as pl
from jax.experimental.pallas import tpu as pltpu

F32 = jnp.float32
BF16 = jnp.bfloat16
EPS = 1e-6
HIGHEST = lax.Precision.HIGHEST
LOG2E = math.log2(math.e)

FOX_HEADS = 8
FOX_HEAD_DIM = 64
FOX_WIDTH = FOX_HEADS * FOX_HEAD_DIM
HGRN_HEADS = 4
HGRN_HEAD_DIM = 128
HGRN_WIDTH = HGRN_HEADS * HGRN_HEAD_DIM
HGRN_CHUNK = 16
PEER_HEADS = 8
PEER_QUERY_DIM = 256
PEER_HALF = PEER_QUERY_DIM // 2
N_KEYS = 128
PEER_TOPK = 16
PEER_E = PEER_HEADS * PEER_TOPK
PEER_GROUP = 32
N_MOD = 6

LANES = 128
SUBLANES = 8
VMEM_LIMIT = 56 * 1024 * 1024


def _cparams(n_axes, vmem=VMEM_LIMIT):
    return pltpu.CompilerParams(dimension_semantics=("arbitrary",) * n_axes, vmem_limit_bytes=vmem)


def _dot(a, b):
    return jnp.dot(a, b, preferred_element_type=F32)


def _dot_nt(a, b):
    return lax.dot_general(a, b, (((1,), (1,)), ((), ())), preferred_element_type=F32)


def _dot_tn(a, b):
    return lax.dot_general(a, b, (((0,), (0,)), ((), ())), preferred_element_type=F32)


def _rms(x, g):
    return x * lax.rsqrt(jnp.mean(x * x, axis=-1, keepdims=True) + EPS) * g


def _ada_kernel(c_ref, w_ref, b_ref, o_ref):
    c = c_ref[...]
    cond = c * jax.nn.sigmoid(c)
    o_ref[...] = jnp.dot(cond, w_ref[...], preferred_element_type=F32, precision=HIGHEST) + b_ref[...]


def _ada(c, w, b):
    B, D = c.shape
    N = w.shape[1]
    tn = 1024
    return pl.pallas_call(
        _ada_kernel,
        grid=(N // tn,),
        in_specs=[pl.BlockSpec((B, D), lambda j: (0, 0)),
                  pl.BlockSpec((D, tn), lambda j: (0, j)),
                  pl.BlockSpec((1, tn), lambda j: (0, j))],
        out_specs=pl.BlockSpec((B, tn), lambda j: (0, j)),
        out_shape=jax.ShapeDtypeStruct((B, N), F32),
        compiler_params=_cparams(1),
        name="ada",
    )(c, w, b.reshape(1, N))


def _split3(f):
    hi = f.astype(BF16).astype(F32)
    r = f - hi
    mid = r.astype(BF16).astype(F32)
    return hi, mid, r - mid


def _proj_kernel(x_ref, mod_ref, g_ref, wqkv_ref, wf_ref, bf_ref, wh_ref, tri_ref,
                 qt_ref, ka_ref, vt_ref, hq_ref, hf_ref, hi_ref, hg_ref, carry_ref,
                 *, tiles_per_batch):
    i = pl.program_id(0)
    D = x_ref.shape[1]
    tm = x_ref.shape[0]
    Dh = FOX_HEAD_DIM
    x = x_ref[...]
    shift = mod_ref[0, :, 0:D]
    scale = mod_ref[0, :, D:2 * D]
    h = (_rms(x, g_ref[...]) * (1.0 + scale) + shift).astype(BF16)

    ff = _dot(h, wf_ref[...]) + bf_ref[...]
    logf = jnp.minimum(ff, 0.0) - jnp.log(1.0 + jnp.exp(-jnp.abs(ff)))

    @pl.when(i % tiles_per_batch == 0)
    def _():
        carry_ref[...] = jnp.zeros_like(carry_ref)

    cs = jnp.dot(tri_ref[...], logf, preferred_element_type=F32, precision=HIGHEST) + carry_ref[...]
    carry_ref[...] = cs[tm - 1:tm, :]

    qkv = _dot(h, wqkv_ref[...])
    lane = lax.broadcasted_iota(jnp.int32, (tm, Dh), 1)
    zpad = jnp.zeros((tm, Dh), F32)
    for hd in range(FOX_HEADS):
        lo = hd * Dh
        fhi, fmid, flo = _split3(cs[:, hd:hd + 1] * LOG2E)
        pieces = lambda o: jnp.where(lane == o, fhi, jnp.where(lane == o + 1, fmid, jnp.where(lane == o + 2, flo, 0.0)))
        q_aux = jnp.where(lane < 3, -1.0, pieces(3))
        k_aux = jnp.where((lane >= 3) & (lane < 6), 1.0, pieces(0))
        qa = jnp.concatenate([qkv[:, lo:lo + Dh] * (Dh ** -0.5 * LOG2E), q_aux], axis=1)
        ka = jnp.concatenate([qkv[:, FOX_WIDTH + lo:FOX_WIDTH + lo + Dh], k_aux], axis=1)
        va = jnp.concatenate([qkv[:, 2 * FOX_WIDTH + lo:2 * FOX_WIDTH + lo + Dh], zpad], axis=1)
        qt_ref[0, hd] = qa.T.astype(BF16)
        ka_ref[0, hd] = ka.astype(BF16)
        vt_ref[0, hd] = va.T[0:Dh, :].astype(BF16)

    hh = _dot(h, wh_ref[...])
    hq_ref[...] = hh[:, 0:HGRN_WIDTH]
    hf_ref[...] = hh[:, HGRN_WIDTH:2 * HGRN_WIDTH]
    hi_ref[...] = hh[:, 2 * HGRN_WIDTH:3 * HGRN_WIDTH]
    hg_ref[...] = hh[:, 3 * HGRN_WIDTH:4 * HGRN_WIDTH]


def _proj(x2d, mod3, g, wqkv, wf, bf, wh, B, S):
    T, D = x2d.shape
    tm = 256
    tpb = S // tm
    tri = (jnp.arange(tm)[:, None] >= jnp.arange(tm)[None, :]).astype(F32)
    H, Dh = FOX_HEADS, FOX_HEAD_DIM
    fox_shapes = [jax.ShapeDtypeStruct((B, H, LANES, S), BF16), jax.ShapeDtypeStruct((B, H, S, LANES), BF16),
                  jax.ShapeDtypeStruct((B, H, Dh, S), BF16)]
    fox_specs = [pl.BlockSpec((1, H, LANES, tm), lambda i: (i // tpb, 0, 0, i % tpb)),
                 pl.BlockSpec((1, H, tm, LANES), lambda i: (i // tpb, 0, i % tpb, 0)),
                 pl.BlockSpec((1, H, Dh, tm), lambda i: (i // tpb, 0, 0, i % tpb))]
    wide = jax.ShapeDtypeStruct((T, HGRN_WIDTH), F32)
    wspec = pl.BlockSpec((tm, HGRN_WIDTH), lambda i: (i, 0))
    const = lambda shape: pl.BlockSpec(shape, lambda i: (0,) * len(shape))
    return pl.pallas_call(
        functools.partial(_proj_kernel, tiles_per_batch=tpb),
        grid=(T // tm,),
        in_specs=[pl.BlockSpec((tm, D), lambda i: (i, 0)),
                  pl.BlockSpec((1, 1, N_MOD * D), lambda i: (i // tpb, 0, 0)),
                  const((1, D)), const(wqkv.shape), const(wf.shape), const(bf.shape), const(wh.shape),
                  const((tm, tm))],
        out_specs=fox_specs + [wspec, wspec, wspec, wspec],
        out_shape=fox_shapes + [wide, wide, wide, wide],
        scratch_shapes=[pltpu.VMEM((1, LANES), F32)],
        compiler_params=_cparams(1),
        name="proj",
    )(x2d, mod3, g, wqkv, wf, bf, wh, tri)


FOX_HB = 8


def _fox_kernel(qt_ref, ka_ref, vt_ref, o_ref, m_ref, l_ref, acc_ref, *, tq):
    qi = pl.program_id(2)
    m_ref[...] = jnp.full_like(m_ref, -jnp.inf)
    l_ref[...] = jnp.zeros_like(l_ref)
    acc_ref[...] = jnp.zeros_like(acc_ref)

    def step(j, masked):
        off = pl.multiple_of(j * tq, tq)
        hs = range(FOX_HB)
        s = [_dot(ka_ref[0, hh, pl.ds(off, tq), :], qt_ref[0, hh]) for hh in hs]
        if masked:
            key = lax.broadcasted_iota(jnp.int32, (tq, tq), 0)
            qry = lax.broadcasted_iota(jnp.int32, (tq, tq), 1)
            s = [jnp.where(key <= qry, sh, -jnp.inf) for sh in s]
        m_old = [m_ref[hh] for hh in hs]
        m_new = [jnp.maximum(m_old[hh], jnp.max(s[hh], axis=0, keepdims=True)) for hh in hs]
        p = [jnp.exp2(s[hh] - m_new[hh]) for hh in hs]
        alpha = [jnp.exp2(m_old[hh] - m_new[hh]) for hh in hs]
        pv = [_dot(vt_ref[0, hh, :, pl.ds(off, tq)], p[hh].astype(BF16)) for hh in hs]
        for hh in hs:
            l_ref[hh] = alpha[hh] * l_ref[hh] + jnp.sum(p[hh], axis=0, keepdims=True)
            acc_ref[hh] = alpha[hh] * acc_ref[hh] + pv[hh]
            m_ref[hh] = m_new[hh]

    step(qi, True)

    def body(j, _):
        step(j, False)
        return 0

    lax.fori_loop(0, qi, body, 0)
    o = [(acc_ref[hh] / l_ref[hh]).T for hh in range(FOX_HB)]
    o_ref[0] = jnp.concatenate(o, axis=1).astype(o_ref.dtype)


def _fox(qt, ka, vt):
    B, H, S, _ = ka.shape
    Dh = vt.shape[2]
    tq = 512
    hb = FOX_HB
    return pl.pallas_call(
        functools.partial(_fox_kernel, tq=tq),
        grid=(B, H // hb, S // tq),
        in_specs=[pl.BlockSpec((1, hb, LANES, tq), lambda b, h, i: (b, h, 0, i)),
                  pl.BlockSpec((1, hb, S, LANES), lambda b, h, i: (b, h, 0, 0)),
                  pl.BlockSpec((1, hb, Dh, S), lambda b, h, i: (b, h, 0, 0))],
        out_specs=pl.BlockSpec((1, tq, hb * Dh), lambda b, h, i: (b, i, h)),
        out_shape=jax.ShapeDtypeStruct((B, S, H * Dh), BF16),
        scratch_shapes=[pltpu.VMEM((hb, 1, tq), F32), pltpu.VMEM((hb, 1, tq), F32), pltpu.VMEM((hb, Dh, tq), F32)],
        compiler_params=_cparams(3),
        name="fox",
    )(qt, ka, vt)


def _hgrn_kernel(hq_ref, hf_ref, hi_ref, hg_ref, gam_ref, ng_ref, bd_ref, bl_ref, o_ref, st_ref, os_ref):
    j = pl.program_id(1)
    tb = hq_ref.shape[0]
    C = HGRN_CHUNK
    Dk = HGRN_HEAD_DIM

    @pl.when(j == 0)
    def _():
        st_ref[...] = jnp.zeros_like(st_ref)

    gam = gam_ref[...]
    e = jnp.exp(gam - jnp.max(gam, axis=0, keepdims=True))
    lb = e[0:1, :] / jnp.sum(e, axis=0, keepdims=True)
    f = lb + (1.0 - lb) * jax.nn.sigmoid(hf_ref[...])
    lf = jnp.log(f)
    A = jnp.dot(bd_ref[...], lf, preferred_element_type=F32, precision=HIGHEST)
    AL = jnp.dot(bl_ref[...], lf, preferred_element_type=F32, precision=HIGHEST)
    q = hq_ref[...] * (Dk ** -0.5)
    kk = 1.0 - f
    qa = q * jnp.exp(A)
    ka = kk * jnp.exp(AL - A)
    eal = jnp.exp(AL)
    inp = hi_ref[...]

    tri3 = (lax.broadcasted_iota(jnp.int32, (C, C, Dk), 0) >= lax.broadcasted_iota(jnp.int32, (C, C, Dk), 1))
    heads = range(HGRN_HEADS)
    chunks = range(tb // C)
    blk = lambda a, c, hd: a[c * C:(c + 1) * C, hd * Dk:(hd + 1) * Dk]
    intra = {}
    for c in chunks:
        for hd in heads:
            Ac = blk(A, c, hd)
            diff = Ac[:, None, :] - Ac[None, :, :]
            dec = jnp.exp(jnp.where(tri3, diff, -jnp.inf))
            sc = jnp.sum(blk(q, c, hd)[:, None, :] * blk(kk, c, hd)[None, :, :] * dec, axis=-1)
            intra[c, hd] = _dot(sc, blk(inp, c, hd))
    st = [st_ref[hd] for hd in heads]
    for c in chunks:
        for hd in heads:
            os_ref[c * C:(c + 1) * C, hd * Dk:(hd + 1) * Dk] = _dot_nt(blk(qa, c, hd), st[hd]) + intra[c, hd]
            st[hd] = st[hd] * blk(eal, c, hd)[0:1, :] + _dot_tn(blk(inp, c, hd), blk(ka, c, hd))
    for hd in heads:
        st_ref[hd] = st[hd]

    o_all = os_ref[...]
    hg = hg_ref[...]
    ng = ng_ref[...]
    for hd in range(HGRN_HEADS):
        lo = hd * Dk
        oh = _rms(o_all[:, lo:lo + Dk], ng[:, lo:lo + Dk])
        g = hg[:, lo:lo + Dk]
        o_ref[:, lo:lo + Dk] = (oh * (g * jax.nn.sigmoid(g))).astype(o_ref.dtype)


def _hgrn(hq, hf, hi, hg, gamma, norm_g, B, S):
    T, W = hq.shape
    tb = 256
    nb = S // tb
    t = jnp.arange(tb)
    same = (t[:, None] // HGRN_CHUNK) == (t[None, :] // HGRN_CHUNK)
    bd = (same & (t[:, None] >= t[None, :])).astype(F32)
    bl = same.astype(F32)
    wspec = pl.BlockSpec((tb, W), lambda b, j: (b * nb + j, 0))
    const = lambda shape: pl.BlockSpec(shape, lambda b, j: (0,) * len(shape))
    return pl.pallas_call(
        _hgrn_kernel,
        grid=(B, nb),
        in_specs=[wspec, wspec, wspec, wspec, const(gamma.shape), const((1, W)), const((tb, tb)), const((tb, tb))],
        out_specs=wspec,
        out_shape=jax.ShapeDtypeStruct((T, W), BF16),
        scratch_shapes=[pltpu.VMEM((HGRN_HEADS, HGRN_HEAD_DIM, HGRN_HEAD_DIM), F32), pltpu.VMEM((tb, W), F32)],
        compiler_params=_cparams(2),
        name="hgrn",
    )(hq, hf, hi, hg, gamma, norm_g, bd, bl)


def _outproj_kernel(x_ref, of_ref, oh_ref, mod_ref, wo_ref, pmg_ref, pfg_ref, wq_ref, keys_ref,
                    x1_ref, h2_ref, sc_ref):
    D = x_ref.shape[1]
    mix = _dot(jnp.concatenate([of_ref[...], oh_ref[...]], axis=1), wo_ref[...])
    gate_a = mod_ref[0, :, 2 * D:3 * D]
    shift_f = mod_ref[0, :, 3 * D:4 * D]
    scale_f = mod_ref[0, :, 4 * D:5 * D]
    x1 = x_ref[...] + gate_a * _rms(mix, pmg_ref[...])
    x1_ref[...] = x1
    h2 = _rms(x1, pfg_ref[...]) * (1.0 + scale_f) + shift_f
    h2_ref[...] = h2
    qp = _dot(h2.astype(BF16), wq_ref[...])
    for g in range(2 * PEER_HEADS):
        qg = qp[:, g * PEER_HALF:(g + 1) * PEER_HALF]
        qn = qg * lax.rsqrt(jnp.mean(qg * qg, axis=-1, keepdims=True) + EPS)
        sc_ref[g] = _dot_nt(keys_ref[g], qn.astype(BF16))


def _outproj(x2d, ofox, oh, mod3, wo, pmg, pfg, wq, keys, B, S):
    T, D = x2d.shape
    tm = 512
    tpb = S // tm
    G = 2 * PEER_HEADS
    const = lambda shape: pl.BlockSpec(shape, lambda i: (0,) * len(shape))
    row = pl.BlockSpec((tm, D), lambda i: (i, 0))
    return pl.pallas_call(
        _outproj_kernel,
        grid=(T // tm,),
        in_specs=[row,
                  pl.BlockSpec((tm, FOX_WIDTH), lambda i: (i, 0)),
                  pl.BlockSpec((tm, HGRN_WIDTH), lambda i: (i, 0)),
                  pl.BlockSpec((1, 1, N_MOD * D), lambda i: (i // tpb, 0, 0)),
                  const(wo.shape), const((1, D)), const((1, D)), const(wq.shape), const(keys.shape)],
        out_specs=[row, row, pl.BlockSpec((G, N_KEYS, tm), lambda i: (0, 0, i))],
        out_shape=[jax.ShapeDtypeStruct((T, D), F32), jax.ShapeDtypeStruct((T, D), F32),
                   jax.ShapeDtypeStruct((G, N_KEYS, T), F32)],
        compiler_params=_cparams(1),
        name="outproj",
    )(x2d, ofox, oh, mod3, wo, pmg, pfg, wq, keys)


TOPK_TM = SUBLANES * LANES


def _cx(a, b, desc):
    (va, ia), (vb, ib) = a, b
    swap = (vb > va) if desc else (vb < va)
    return ((jnp.where(swap, vb, va), jnp.where(swap, ib, ia)),
            (jnp.where(swap, va, vb), jnp.where(swap, ia, ib)))


def _bitonic_merge(x, desc):
    n = len(x)
    x = list(x)
    d = n // 2
    while d >= 1:
        for i in range(n):
            if (i // d) % 2 == 0:
                x[i], x[i + d] = _cx(x[i], x[i + d], desc)
        d //= 2
    return x


def _bitonic_sort(x, desc):
    n = len(x)
    if n == 1:
        return list(x)
    lo = _bitonic_sort(x[:n // 2], True)
    hi = _bitonic_sort(x[n // 2:], False)
    return _bitonic_merge(lo + hi, desc)


def _tile(ref, row):
    return ref[pl.ds(pl.multiple_of(row * SUBLANES, SUBLANES), SUBLANES), :]


def _put(ref, row, x):
    ref[pl.ds(pl.multiple_of(row * SUBLANES, SUBLANES), SUBLANES), :] = x


def _top16_network(s_ref, n, p_ref, va_ref, ia_ref, vb_ref, ib_ref):
    K = PEER_TOPK
    shape = (SUBLANES, LANES)

    def store_block(v_ref, i_ref, blk, pairs):
        rev = blk % 2
        for k, (v, i) in enumerate(pairs):
            pos = blk * K + k + rev * (K - 1 - 2 * k)
            _put(v_ref, pos, v)
            _put(i_ref, pos, i)

    def sort_block(blk, c):
        tag = (lambda r: jnp.full(shape, r, jnp.int32)) if p_ref is None else (lambda r: _tile(p_ref, r))
        pairs = [(_tile(s_ref, blk * K + k), tag(blk * K + k)) for k in range(K)]
        store_block(va_ref, ia_ref, blk, _bitonic_sort(pairs, True))
        return c

    lax.fori_loop(0, n // K, sort_block, 0)

    def merge_round(src_v, src_i, dst_v, dst_i, n_pairs):
        def merge(p, c):
            top = []
            for k in range(K):
                va, ia = _tile(src_v, 2 * p * K + k), _tile(src_i, 2 * p * K + k)
                vb, ib = _tile(src_v, (2 * p + 1) * K + k), _tile(src_i, (2 * p + 1) * K + k)
                take_b = vb > va
                top.append((jnp.where(take_b, vb, va), jnp.where(take_b, ib, ia)))
            store_block(dst_v, dst_i, p, _bitonic_merge(top, True))
            return c
        lax.fori_loop(0, n_pairs, merge, 0)

    src, dst = (va_ref, ia_ref), (vb_ref, ib_ref)
    blocks = n // K
    while blocks > 1:
        merge_round(*src, *dst, blocks // 2)
        src, dst = dst, src
        blocks //= 2
    return ([src[0][k * SUBLANES:(k + 1) * SUBLANES, :] for k in range(K)],
            [src[1][k * SUBLANES:(k + 1) * SUBLANES, :] for k in range(K)])


def _untied(s_ref, n, vals):
    ok = vals[0] > vals[1]
    for a in range(1, PEER_TOPK - 1):
        ok = ok & (vals[a] > vals[a + 1])
    last = vals[-1]

    def count(c, cnt):
        for k in range(SUBLANES):
            cnt = cnt + jnp.where(_tile(s_ref, c * SUBLANES + k) >= last, 1, 0)
        return cnt
    cnt = lax.fori_loop(0, n // SUBLANES, count, jnp.zeros(last.shape, jnp.int32))
    return ok & (cnt == PEER_TOPK)


def _top16_exact_loop(s_ref, n, p_ref):
    shape = (SUBLANES, LANES)
    U = SUBLANES

    def over_rows(fn, init):
        def trip(c, acc):
            for u in range(U):
                acc = fn(c * U + u, acc)
            return acc
        return lax.fori_loop(0, n // U, trip, init)

    vals, picks = [], []
    for _ in range(PEER_TOPK):
        m = over_rows(lambda k, m: jnp.maximum(m, _tile(s_ref, k)), jnp.full(shape, -jnp.inf, F32))
        ix = over_rows(lambda k, ix: jnp.minimum(ix, jnp.where(_tile(s_ref, k) == m, k, n)),
                       jnp.full(shape, n, jnp.int32))

        def remove(k, pick):
            hit = ix == k
            _put(s_ref, k, jnp.where(hit, -jnp.inf, _tile(s_ref, k)))
            return pick if p_ref is None else pick + jnp.where(hit, _tile(p_ref, k), 0)
        pick = over_rows(remove, jnp.zeros(shape, jnp.int32))
        vals.append(m)
        picks.append(ix if p_ref is None else pick)
    return vals, picks


def _top16(s_ref, n, p_ref, bufs):
    vals, tags = _top16_network(s_ref, n, p_ref, *bufs)
    ok = jnp.min(jnp.where(_untied(s_ref, n, vals), 1, 0)) > 0
    pack = lambda v, i: (jnp.stack(v), jnp.stack(i))
    v, i = lax.cond(ok, lambda: pack(vals, tags), lambda: pack(*_top16_exact_loop(s_ref, n, p_ref)))
    return [v[a] for a in range(PEER_TOPK)], [i[a] for a in range(PEER_TOPK)]


N_CAND = 64


def _topk_kernel(sc_ref, off_ref, par_ref, gate_ref, s_ref, p_ref, va_ref, ia_ref, vb_ref, ib_ref, e_ref):
    K = PEER_TOPK
    tm = TOPK_TM
    bufs = (va_ref, ia_ref, vb_ref, ib_ref)
    shape = (SUBLANES, LANES)

    def half_topk(g):
        x = sc_ref[g]
        blocks = [x[:, s * LANES:(s + 1) * LANES].reshape(N_KEYS // SUBLANES, SUBLANES, LANES)
                  for s in range(SUBLANES)]
        s_ref[...] = pltpu.einshape("gskl->gksl", jnp.stack(blocks, axis=1)).reshape(N_KEYS * SUBLANES, LANES)
        return _top16(s_ref, N_KEYS, None, bufs)

    for hd in range(PEER_HEADS):
        s1, i1 = half_topk(2 * hd)
        s2, i2 = half_topk(2 * hd + 1)
        r = 0
        for a in range(K):
            for b in range(K // (a + 1)):
                _put(s_ref, r, s1[a] + s2[b])
                _put(p_ref, r, i1[a] * N_KEYS + i2[b])
                r += 1
        for r in range(r, N_CAND):
            _put(s_ref, r, jnp.full(shape, -jnp.inf, F32))
            _put(p_ref, r, jnp.zeros(shape, jnp.int32))
        best, eidx = _top16(s_ref, N_CAND, p_ref, bufs)
        p = [jnp.exp(bv - best[0]) for bv in best]
        tot = functools.reduce(jnp.add, p)
        for a in range(K):
            e = hd * K + a
            e_ref[0, e * SUBLANES:(e + 1) * SUBLANES, :] = eidx[a].astype(F32)
            e_ref[1, e * SUBLANES:(e + 1) * SUBLANES, :] = p[a] / tot

    def token_major(x):
        t = pltpu.einshape("gesl->gsel", x.reshape(PEER_E // SUBLANES, SUBLANES, SUBLANES, LANES))
        return jnp.concatenate([t[:, s].reshape(PEER_E, LANES).T for s in range(SUBLANES)], axis=0)

    n_half = N_KEYS * N_KEYS // 2
    idx = token_major(e_ref[0]).astype(jnp.int32)
    gate_ref[...] = token_major(e_ref[1])
    par_ref[...] = (idx >= n_half).astype(F32)
    off = (idx & (n_half - 1)) * SUBLANES
    off_ref[...] = pltpu.einshape("abl->bal", off.reshape(tm // PEER_GROUP, PEER_GROUP, PEER_E))


def _topk(scores):
    G, NK, T = scores.shape
    tm = TOPK_TM
    out = pl.BlockSpec((tm, PEER_E), lambda i: (i, 0))
    return pl.pallas_call(
        _topk_kernel,
        grid=(T // tm,),
        in_specs=[pl.BlockSpec((G, NK, tm), lambda i: (0, 0, i))],
        out_specs=[pl.BlockSpec((PEER_GROUP, tm // PEER_GROUP, PEER_E), lambda i: (0, i, 0)), out, out],
        out_shape=[jax.ShapeDtypeStruct((PEER_GROUP, T // PEER_GROUP, PEER_E), jnp.int32),
                   jax.ShapeDtypeStruct((T, PEER_E), F32), jax.ShapeDtypeStruct((T, PEER_E), F32)],
        scratch_shapes=[pltpu.VMEM((N_KEYS * SUBLANES, LANES), F32), pltpu.VMEM((N_CAND * SUBLANES, LANES), jnp.int32),
                        pltpu.VMEM((N_KEYS * SUBLANES, LANES), F32), pltpu.VMEM((N_KEYS * SUBLANES, LANES), jnp.int32),
                        pltpu.VMEM((N_KEYS * SUBLANES // 2, LANES), F32),
                        pltpu.VMEM((N_KEYS * SUBLANES // 2, LANES), jnp.int32),
                        pltpu.VMEM((2, PEER_E * SUBLANES, LANES), F32)],
        compiler_params=_cparams(1),
        name="topk",
    )(scores)


PEER_TM = 16 * PEER_GROUP
TILE_ROWS = 2 * SUBLANES
STACK = PEER_E * TILE_ROWS


def _pack_kernel(lo_ref, hi_ref, o_ref):
    nb = lo_ref.shape[0]
    lo = lax.bitcast_convert_type(lo_ref[...].astype(BF16).astype(F32), jnp.uint32)
    hi = lax.bitcast_convert_type(hi_ref[...].astype(BF16).astype(F32), jnp.uint32)
    word = (hi & jnp.uint32(0xFFFF0000)) | lax.shift_right_logical(lo, jnp.uint32(16))
    for s in range(SUBLANES):
        o_ref[pl.ds(s, nb, stride=SUBLANES), :] = word[:, s * LANES:(s + 1) * LANES]


def _pack_table(w):
    n, d = w.shape
    nb = 256
    steps = n // 2 // nb
    return pl.pallas_call(
        _pack_kernel,
        grid=(steps,),
        in_specs=[pl.BlockSpec((nb, d), lambda i: (i, 0)), pl.BlockSpec((nb, d), lambda i: (i + steps, 0))],
        out_specs=pl.BlockSpec((nb * SUBLANES, LANES), lambda i: (i, 0)),
        out_shape=jax.ShapeDtypeStruct((n // 2 * SUBLANES, LANES), jnp.uint32),
        compiler_params=_cparams(1),
        name="pack",
    )(w, w)


def _peer_consts():
    c = jnp.arange(STACK)
    expand = (c[None, :] // TILE_ROWS == jnp.arange(PEER_E)[:, None])
    fold = ((c[None, :] % TILE_ROWS) // 2 == jnp.arange(SUBLANES)[:, None])
    half = (c % 2).reshape(1, STACK)
    return expand.astype(BF16), expand.T.astype(BF16), fold.astype(F32), half.astype(F32)


def _stack2(tab_ref, off_a, off_b, g):
    def tile(o):
        return pltpu.bitcast(tab_ref[pl.ds(pl.multiple_of(o, SUBLANES), SUBLANES), :], BF16)
    w0 = jnp.concatenate([tile(off_a[g, j]) for j in range(PEER_E)], axis=0)
    w1 = jnp.concatenate([tile(off_b[g, j]) for j in range(PEER_E)], axis=0)
    return jnp.concatenate([w0, w1], axis=1)


def _split2_dot(x, w):
    hi = x.astype(BF16)
    lo = (x - hi.astype(F32)).astype(BF16)
    return _dot(hi, w) + _dot(lo, w)


def _expand_sel(par_ref, expand_ref, half_ref):
    par = _dot(par_ref[...].astype(BF16), expand_ref[...])
    return jnp.where(par == half_ref[...], 1.0, 0.0)


def _fold_rows(x):
    tm = x.shape[0]
    chunks = [x[:, c * LANES:(c + 1) * LANES].reshape(tm // SUBLANES, SUBLANES, LANES) for c in range(SUBLANES)]
    return pltpu.einshape("gctl->gtcl", jnp.stack(chunks, axis=1)).reshape(tm * SUBLANES, LANES)


def _unfold_rows(xf):
    tm = xf.shape[0] // SUBLANES
    t = pltpu.einshape("gtcl->gctl", xf.reshape(tm // SUBLANES, SUBLANES, SUBLANES, LANES))
    return jnp.concatenate([t[:, c].reshape(tm, LANES) for c in range(SUBLANES)], axis=1)


def _peer_u_kernel(*refs):
    offs = refs[:PEER_GROUP]
    (par_ref, x_ref, gate_ref, expand_ref, collapse_ref, fold_ref, half_ref, tab_ref,
     act_ref, sel_ref, g_ref, xf_ref) = refs[PEER_GROUP:]
    tm = x_ref.shape[0]
    sel_ref[...] = _expand_sel(par_ref, expand_ref, half_ref)
    xf_ref[...] = _fold_rows(x_ref[...])
    fold = fold_ref[...]
    zero = jnp.zeros((SUBLANES, LANES), BF16)

    def pair(g, p):
        t = PEER_GROUP * g + 2 * p
        w2 = _stack2(tab_ref, offs[2 * p], offs[2 * p + 1], g)
        x0 = xf_ref[pl.ds(pl.multiple_of(t * SUBLANES, SUBLANES), SUBLANES), :].astype(BF16)
        x1 = xf_ref[pl.ds(pl.multiple_of((t + 1) * SUBLANES, SUBLANES), SUBLANES), :].astype(BF16)
        lhs = jnp.concatenate([jnp.concatenate([x0, zero], axis=1),
                               jnp.concatenate([zero, x1], axis=1)], axis=0)
        gm = _dot_nt(lhs, w2)
        g_ref[pl.ds(t, 1), :] = jnp.sum(gm[0:SUBLANES] * fold, axis=0, keepdims=True) * sel_ref[pl.ds(t, 1), :]
        g_ref[pl.ds(t + 1, 1), :] = (jnp.sum(gm[SUBLANES:] * fold, axis=0, keepdims=True)
                                     * sel_ref[pl.ds(t + 1, 1), :])

    def group(g, _):
        for p in range(PEER_GROUP // 2):
            pair(g, p)
        return 0

    lax.fori_loop(0, tm // PEER_GROUP, group, 0, unroll=2)
    a = _split2_dot(g_ref[...], collapse_ref[...])
    gelu = 0.5 * a * (1.0 + lax.erf(a * (2.0 ** -0.5)))
    act_ref[...] = gate_ref[...] * gelu


def _peer_v_kernel(*refs):
    offs = refs[:PEER_GROUP]
    (par_ref, act_ref, expand_ref, fold_ref, half_ref, tab_ref, x1_ref, mod_ref, g_ref,
     o_ref, ce_ref, yf_ref) = refs[PEER_GROUP:]
    tm, D = x1_ref.shape
    ce_ref[...] = _dot(act_ref[...].astype(BF16), expand_ref[...]) * _expand_sel(par_ref, expand_ref, half_ref)
    fold = fold_ref[...]

    def pair(g, p):
        t = PEER_GROUP * g + 2 * p
        w2 = _stack2(tab_ref, offs[2 * p], offs[2 * p + 1], g)
        coef = jnp.concatenate([(ce_ref[pl.ds(t, 1), :] * fold).astype(BF16),
                                (ce_ref[pl.ds(t + 1, 1), :] * fold).astype(BF16)], axis=0)
        out = _dot(coef, w2)
        yf_ref[pl.ds(pl.multiple_of(t * SUBLANES, SUBLANES), SUBLANES), :] = out[0:SUBLANES, 0:LANES]
        yf_ref[pl.ds(pl.multiple_of((t + 1) * SUBLANES, SUBLANES), SUBLANES), :] = out[SUBLANES:, LANES:]

    def group(g, _):
        for p in range(PEER_GROUP // 2):
            pair(g, p)
        return 0

    lax.fori_loop(0, tm // PEER_GROUP, group, 0)
    y = _unfold_rows(yf_ref[...])
    gate_f = mod_ref[0, :, 5 * D:6 * D]
    o_ref[...] = x1_ref[...] + gate_f * _rms(y, g_ref[...])


def _peer_specs(tab):
    tm = PEER_TM
    smem = [pl.BlockSpec((None, tm // PEER_GROUP, PEER_E), functools.partial(lambda k, i: (k, i, 0), k),
                         memory_space=pltpu.SMEM) for k in range(PEER_GROUP)]
    row = pl.BlockSpec((tm, PEER_E), lambda i: (i, 0))
    const = lambda shape: pl.BlockSpec(shape, lambda i: (0,) * len(shape))
    table = pl.BlockSpec(tab.shape, lambda i: (0, 0), pipeline_mode=pl.Buffered(1))
    return tm, smem, row, const, table


def _peer_u(off, par, h2, gate, tab):
    T, D = h2.shape
    tm, smem, row, const, table = _peer_specs(tab)
    expand, collapse, fold, half = _peer_consts()
    return pl.pallas_call(
        _peer_u_kernel,
        grid=(T // tm,),
        in_specs=smem + [row, pl.BlockSpec((tm, D), lambda i: (i, 0)), row,
                  const(expand.shape), const(collapse.shape), const(fold.shape), const(half.shape), table],
        out_specs=row,
        out_shape=jax.ShapeDtypeStruct((T, PEER_E), F32),
        scratch_shapes=[pltpu.VMEM((tm, STACK), F32), pltpu.VMEM((tm, STACK), F32),
                        pltpu.VMEM((tm * SUBLANES, LANES), F32)],
        compiler_params=_cparams(1),
        name="peer_u",
    )(*([off] * PEER_GROUP), par, h2, gate, expand, collapse, fold, half, tab)


def _peer_v(off, par, act, tab, x1, mod3, g, S):
    T, D = x1.shape
    tm, smem, row, const, table = _peer_specs(tab)
    tpb = S // tm
    expand, _, fold, half = _peer_consts()
    wide = pl.BlockSpec((tm, D), lambda i: (i, 0))
    return pl.pallas_call(
        _peer_v_kernel,
        grid=(T // tm,),
        in_specs=smem + [row, row, const(expand.shape), const(fold.shape), const(half.shape), table,
                  wide, pl.BlockSpec((1, 1, N_MOD * D), lambda i: (i // tpb, 0, 0)), const((1, D))],
        out_specs=wide,
        out_shape=jax.ShapeDtypeStruct((T, D), F32),
        scratch_shapes=[pltpu.VMEM((tm, STACK), F32), pltpu.VMEM((tm * SUBLANES, LANES), F32)],
        compiler_params=_cparams(1),
        name="peer_v",
    )(*([off] * PEER_GROUP), par, act, expand, fold, half, tab, x1, mod3, g)


def kernel(x, c, w_ada, b_ada, pre_mix_g, post_mix_g, w_in, b_fox_f, hgrn_gamma, hgrn_norm_g, w_out, pre_ffn_g,
           post_ffn_g, peer_w_q, peer_sub_keys, peer_u, peer_v):
    B, S, D = x.shape
    T = B * S
    depth = w_in.shape[0]
    assert depth == 1, "single-layer block"
    l = 0
    x2d = x.reshape(T, D)

    mod3 = _ada(c, w_ada[l], b_ada[l]).reshape(B, 1, N_MOD * D)

    w = w_in[l]
    o1 = 3 * FOX_WIDTH
    wqkv = w[:, :o1].astype(BF16)
    wf = jnp.pad(w[:, o1:o1 + FOX_HEADS], ((0, 0), (0, LANES - FOX_HEADS))).astype(BF16)
    bf = jnp.pad(b_fox_f[l].astype(F32), (0, LANES - FOX_HEADS)).reshape(1, LANES)
    wh = w[:, o1 + FOX_HEADS:].astype(BF16)

    qt, ka, vt, hq, hf, hi, hg = _proj(x2d, mod3, pre_mix_g[l].reshape(1, D), wqkv, wf, bf, wh, B, S)

    o_fox = _fox(qt, ka, vt)

    o_h = _hgrn(hq, hf, hi, hg, hgrn_gamma[l:l + 2].astype(F32), hgrn_norm_g[l].reshape(1, HGRN_WIDTH), B, S)

    wo = w_out[l].astype(BF16)
    keys =peer_sub_keys[l].reshape(2 * PEER_HEADS, N_KEYS, PEER_HALF).astype(BF16)
    x1, h2, scores = _outproj(x2d, o_fox.reshape(T, FOX_WIDTH), o_h, mod3, wo, post_mix_g[l].reshape(1, D),
                              pre_ffn_g[l].reshape(1, D), peer_w_q[l].astype(BF16), keys, B, S)

    off, par, gate = _topk(scores)

    act = _peer_u(off, par, h2, gate, _pack_table(peer_u[l]))
    out = _peer_v(off, par, act, _pack_table(peer_v[l]), x1, mod3, post_ffn_g[l].reshape(1, D), S)
    return out.reshape(B, S, D)
```

```python
import functools
import math

import jax
import jax.numpy as jnp
from jax import lax
from jax.experimental import pallas as pl
from jax.experimental.pallas import tpu as pltpu

F32 = jnp.float32
BF16 = jnp.bfloat16
EPS = 1e-6
HIGHEST = lax.Precision.HIGHEST
LOG2E = math.log2(math.e)

FOX_HEADS = 8
FOX_HEAD_DIM = 64
FOX_WIDTH = FOX_HEADS * FOX_HEAD_DIM
HGRN_HEADS = 4
HGRN_HEAD_DIM = 128
HGRN_WIDTH = HGRN_HEADS * HGRN_HEAD_DIM
HGRN_CHUNK = 16
PEER_HEADS = 8
PEER_QUERY_DIM = 256
PEER_HALF = PEER_QUERY_DIM // 2
N_KEYS = 128
PEER_TOPK = 16
PEER_E = PEER_HEADS * PEER_TOPK
PEER_GROUP = 32
N_MOD = 6

LANES = 128
SUBLANES = 8
VMEM_LIMIT = 56 * 1024 * 1024


def _cparams(n_axes, vmem=VMEM_LIMIT):
    return pltpu.CompilerParams(dimension_semantics=("arbitrary",) * n_axes, vmem_limit_bytes=vmem)


def _dot(a, b):
    return jnp.dot(a, b, preferred_element_type=F32)


def _dot_nt(a, b):
    return lax.dot_general(a, b, (((1,), (1,)), ((), ())), preferred_element_type=F32)


def _dot_tn(a, b):
    return lax.dot_general(a, b, (((0,), (0,)), ((), ())), preferred_element_type=F32)


def _rms(x, g):
    return x * lax.rsqrt(jnp.mean(x * x, axis=-1, keepdims=True) + EPS) * g


def _ada_kernel(c_ref, w_ref, b_ref, o_ref):
    c = c_ref[...]
    cond = c * jax.nn.sigmoid(c)
    o_ref[...] = jnp.dot(cond, w_ref[...], preferred_element_type=F32, precision=HIGHEST) + b_ref[...]


def _ada(c, w, b):
    B, D = c.shape
    N = w.shape[1]
    tn = 1024
    return pl.pallas_call(
        _ada_kernel,
        grid=(N // tn,),
        in_specs=[pl.BlockSpec((B, D), lambda j: (0, 0)),
                  pl.BlockSpec((D, tn), lambda j: (0, j)),
                  pl.BlockSpec((1, tn), lambda j: (0, j))],
        out_specs=pl.BlockSpec((B, tn), lambda j: (0, j)),
        out_shape=jax.ShapeDtypeStruct((B, N), F32),
        compiler_params=_cparams(1),
        name="ada",
    )(c, w, b.reshape(1, N))


def _split3(f):
    hi = f.astype(BF16).astype(F32)
    r = f - hi
    mid = r.astype(BF16).astype(F32)
    return hi, mid, r - mid


def _proj_kernel(x_ref, mod_ref, g_ref, wqkv_ref, wf_ref, bf_ref, wh_ref, tri_ref,
                 qt_ref, ka_ref, vt_ref, hq_ref, hf_ref, hi_ref, hg_ref, carry_ref,
                 *, tiles_per_batch):
    i = pl.program_id(0)
    D = x_ref.shape[1]
    tm = x_ref.shape[0]
    Dh = FOX_HEAD_DIM
    x = x_ref[...]
    shift = mod_ref[0, :, 0:D]
    scale = mod_ref[0, :, D:2 * D]
    h = (_rms(x, g_ref[...]) * (1.0 + scale) + shift).astype(BF16)

    ff = _dot(h, wf_ref[...]) + bf_ref[...]
    logf = jnp.minimum(ff, 0.0) - jnp.log(1.0 + jnp.exp(-jnp.abs(ff)))

    @pl.when(i % tiles_per_batch == 0)
    def _():
        carry_ref[...] = jnp.zeros_like(carry_ref)

    cs = jnp.dot(tri_ref[...], logf, preferred_element_type=F32, precision=HIGHEST) + carry_ref[...]
    carry_ref[...] = cs[tm - 1:tm, :]

    qkv = _dot(h, wqkv_ref[...])
    lane = lax.broadcasted_iota(jnp.int32, (tm, Dh), 1)
    zpad = jnp.zeros((tm, Dh), F32)
    for hd in range(FOX_HEADS):
        lo = hd * Dh
        fhi, fmid, flo = _split3(cs[:, hd:hd + 1] * LOG2E)
        pieces = lambda o: jnp.where(lane == o, fhi, jnp.where(lane == o + 1, fmid, jnp.where(lane == o + 2, flo, 0.0)))
        q_aux = jnp.where(lane < 3, -1.0, pieces(3))
        k_aux = jnp.where((lane >= 3) & (lane < 6), 1.0, pieces(0))
        qa = jnp.concatenate([qkv[:, lo:lo + Dh] * (Dh ** -0.5 * LOG2E), q_aux], axis=1)
        ka = jnp.concatenate([qkv[:, FOX_WIDTH + lo:FOX_WIDTH + lo + Dh], k_aux], axis=1)
        va = jnp.concatenate([qkv[:, 2 * FOX_WIDTH + lo:2 * FOX_WIDTH + lo + Dh], zpad], axis=1)
        qt_ref[0, hd] = qa.T.astype(BF16)
        ka_ref[0, hd] = ka.astype(BF16)
        vt_ref[0, hd] = va.T[0:Dh, :].astype(BF16)

    hh = _dot(h, wh_ref[...])
    hq_ref[...] = hh[:, 0:HGRN_WIDTH]
    hf_ref[...] = hh[:, HGRN_WIDTH:2 * HGRN_WIDTH]
    hi_ref[...] = hh[:, 2 * HGRN_WIDTH:3 * HGRN_WIDTH]
    hg_ref[...] = hh[:, 3 * HGRN_WIDTH:4 * HGRN_WIDTH]


def _proj(x2d, mod3, g, wqkv, wf, bf, wh, B, S):
    T, D = x2d.shape
    tm = 256
    tpb = S // tm
    tri = (jnp.arange(tm)[:, None] >= jnp.arange(tm)[None, :]).astype(F32)
    H, Dh = FOX_HEADS, FOX_HEAD_DIM
    fox_shapes = [jax.ShapeDtypeStruct((B, H, LANES, S), BF16), jax.ShapeDtypeStruct((B, H, S, LANES), BF16),
                  jax.ShapeDtypeStruct((B, H, Dh, S), BF16)]
    fox_specs = [pl.BlockSpec((1, H, LANES, tm), lambda i: (i // tpb, 0, 0, i % tpb)),
                 pl.BlockSpec((1, H, tm, LANES), lambda i: (i // tpb, 0, i % tpb, 0)),
                 pl.BlockSpec((1, H, Dh, tm), lambda i: (i // tpb, 0, 0, i % tpb))]
    wide = jax.ShapeDtypeStruct((T, HGRN_WIDTH), F32)
    wspec = pl.BlockSpec((tm, HGRN_WIDTH), lambda i: (i, 0))
    const = lambda shape: pl.BlockSpec(shape, lambda i: (0,) * len(shape))
    return pl.pallas_call(
        functools.partial(_proj_kernel, tiles_per_batch=tpb),
        grid=(T // tm,),
        in_specs=[pl.BlockSpec((tm, D), lambda i: (i, 0)),
                  pl.BlockSpec((1, 1, N_MOD * D), lambda i: (i // tpb, 0, 0)),
                  const((1, D)), const(wqkv.shape), const(wf.shape), const(bf.shape), const(wh.shape),
                  const((tm, tm))],
        out_specs=fox_specs + [wspec, wspec, wspec, wspec],
        out_shape=fox_shapes + [wide, wide, wide, wide],
        scratch_shapes=[pltpu.VMEM((1, LANES), F32)],
        compiler_params=_cparams(1),
        name="proj",
    )(x2d, mod3, g, wqkv, wf, bf, wh, tri)


FOX_HB = 8


def _fox_kernel(qt_ref, ka_ref, vt_ref, o_ref, m_ref, l_ref, acc_ref, *, tq):
    qi = pl.program_id(2)
    m_ref[...] = jnp.full_like(m_ref, -jnp.inf)
    l_ref[...] = jnp.zeros_like(l_ref)
    acc_ref[...] = jnp.zeros_like(acc_ref)

    def step(j, masked):
        off = pl.multiple_of(j * tq, tq)
        hs = range(FOX_HB)
        s = [_dot(ka_ref[0, hh, pl.ds(off, tq), :], qt_ref[0, hh]) for hh in hs]
        if masked:
            key = lax.broadcasted_iota(jnp.int32, (tq, tq), 0)
            qry = lax.broadcasted_iota(jnp.int32, (tq, tq), 1)
            s = [jnp.where(key <= qry, sh, -jnp.inf) for sh in s]
        m_old = [m_ref[hh] for hh in hs]
        m_new = [jnp.maximum(m_old[hh], jnp.max(s[hh], axis=0, keepdims=True)) for hh in hs]
        p = [jnp.exp2(s[hh] - m_new[hh]) for hh in hs]
        alpha = [jnp.exp2(m_old[hh] - m_new[hh]) for hh in hs]
        pv = [_dot(vt_ref[0, hh, :, pl.ds(off, tq)], p[hh].astype(BF16)) for hh in hs]
        for hh in hs:
            l_ref[hh] = alpha[hh] * l_ref[hh] + jnp.sum(p[hh], axis=0, keepdims=True)
            acc_ref[hh] = alpha[hh] * acc_ref[hh] + pv[hh]
            m_ref[hh] = m_new[hh]

    step(qi, True)

    def body(j, _):
        step(2 * j, False)
        step(2 * j + 1, False)
        return 0

    lax.fori_loop(0, qi // 2, body, 0)

    @pl.when(qi % 2 == 1)
    def _():
        step(qi - 1, False)
    o = [(acc_ref[hh] / l_ref[hh]).T for hh in range(FOX_HB)]
    o_ref[0] = jnp.concatenate(o, axis=1).astype(o_ref.dtype)


def _fox(qt, ka, vt):
    B, H, S, _ = ka.shape
    Dh = vt.shape[2]
    tq = 512
    hb = FOX_HB
    return pl.pallas_call(
        functools.partial(_fox_kernel, tq=tq),
        grid=(B, H // hb, S // tq),
        in_specs=[pl.BlockSpec((1, hb, LANES, tq), lambda b, h, i: (b, h, 0, i)),
                  pl.BlockSpec((1, hb, S, LANES), lambda b, h, i: (b, h, 0, 0)),
                  pl.BlockSpec((1, hb, Dh, S), lambda b, h, i: (b, h, 0, 0))],
        out_specs=pl.BlockSpec((1, tq, hb * Dh), lambda b, h, i: (b, i, h)),
        out_shape=jax.ShapeDtypeStruct((B, S, H * Dh), BF16),
        scratch_shapes=[pltpu.VMEM((hb, 1, tq), F32), pltpu.VMEM((hb, 1, tq), F32), pltpu.VMEM((hb, Dh, tq), F32)],
        compiler_params=_cparams(3),
        name="fox",
    )(qt, ka, vt)


def _hgrn_kernel(hq_ref, hf_ref, hi_ref, hg_ref, gam_ref, ng_ref, bd_ref, bl_ref, o_ref, st_ref, os_ref):
    j = pl.program_id(1)
    tb = hq_ref.shape[0]
    C = HGRN_CHUNK
    Dk = HGRN_HEAD_DIM

    @pl.when(j == 0)
    def _():
        st_ref[...] = jnp.zeros_like(st_ref)

    gam = gam_ref[...]
    e = jnp.exp(gam - jnp.max(gam, axis=0, keepdims=True))
    lb = e[0:1, :] / jnp.sum(e, axis=0, keepdims=True)
    f = lb + (1.0 - lb) * jax.nn.sigmoid(hf_ref[...])
    lf = jnp.log(f)
    A = jnp.dot(bd_ref[...], lf, preferred_element_type=F32, precision=HIGHEST)
    AL = jnp.dot(bl_ref[...], lf, preferred_element_type=F32, precision=HIGHEST)
    q = hq_ref[...] * (Dk ** -0.5)
    kk = 1.0 - f
    qa = q * jnp.exp(A)
    ka = kk * jnp.exp(AL - A)
    eal = jnp.exp(AL)
    inp = hi_ref[...]

    tri3 = (lax.broadcasted_iota(jnp.int32, (C, C, Dk), 0) >= lax.broadcasted_iota(jnp.int32, (C, C, Dk), 1))
    heads = range(HGRN_HEADS)
    chunks = range(tb // C)
    blk = lambda a, c, hd: a[c * C:(c + 1) * C, hd * Dk:(hd + 1) * Dk]
    intra = {}
    for c in chunks:
        for hd in heads:
            Ac = blk(A, c, hd)
            diff = Ac[:, None, :] - Ac[None, :, :]
            dec = jnp.exp(jnp.where(tri3, diff, -jnp.inf))
            sc = jnp.sum(blk(q, c, hd)[:, None, :] * blk(kk, c, hd)[None, :, :] * dec, axis=-1)
            intra[c, hd] = _dot(sc, blk(inp, c, hd))
    st = [st_ref[hd] for hd in heads]
    for c in chunks:
        for hd in heads:
            os_ref[c * C:(c + 1) * C, hd * Dk:(hd + 1) * Dk] = _dot_nt(blk(qa, c, hd), st[hd]) + intra[c, hd]
            st[hd] = st[hd] * blk(eal, c, hd)[0:1, :] + _dot_tn(blk(inp, c, hd), blk(ka, c, hd))
    for hd in heads:
        st_ref[hd] = st[hd]

    o_all = os_ref[...]
    hg = hg_ref[...]
    ng = ng_ref[...]
    for hd in range(HGRN_HEADS):
        lo = hd * Dk
        oh = _rms(o_all[:, lo:lo + Dk], ng[:, lo:lo + Dk])
        g = hg[:, lo:lo + Dk]
        o_ref[:, lo:lo + Dk] = (oh * (g * jax.nn.sigmoid(g))).astype(o_ref.dtype)


def _hgrn(hq, hf, hi, hg, gamma, norm_g, B, S):
    T, W = hq.shape
    tb = 256
    nb = S // tb
    t = jnp.arange(tb)
    same = (t[:, None] // HGRN_CHUNK) == (t[None, :] // HGRN_CHUNK)
    bd = (same & (t[:, None] >= t[None, :])).astype(F32)
    bl = same.astype(F32)
    wspec = pl.BlockSpec((tb, W), lambda b, j: (b * nb + j, 0))
    const = lambda shape: pl.BlockSpec(shape, lambda b, j: (0,) * len(shape))
    return pl.pallas_call(
        _hgrn_kernel,
        grid=(B, nb),
        in_specs=[wspec, wspec, wspec, wspec, const(gamma.shape), const((1, W)), const((tb, tb)), const((tb, tb))],
        out_specs=wspec,
        out_shape=jax.ShapeDtypeStruct((T, W), BF16),
        scratch_shapes=[pltpu.VMEM((HGRN_HEADS, HGRN_HEAD_DIM, HGRN_HEAD_DIM), F32), pltpu.VMEM((tb, W), F32)],
        compiler_params=_cparams(2),
        name="hgrn",
    )(hq, hf, hi, hg, gamma, norm_g, bd, bl)


def _outproj_kernel(x_ref, of_ref, oh_ref, mod_ref, wo_ref, pmg_ref, pfg_ref, wq_ref, keys_ref,
                    x1_ref, h2_ref, sc_ref):
    D = x_ref.shape[1]
    mix = _dot(jnp.concatenate([of_ref[...], oh_ref[...]], axis=1), wo_ref[...])
    gate_a = mod_ref[0, :, 2 * D:3 * D]
    shift_f = mod_ref[0, :, 3 * D:4 * D]
    scale_f = mod_ref[0, :, 4 * D:5 * D]
    x1 = x_ref[...] + gate_a * _rms(mix, pmg_ref[...])
    x1_ref[...] = x1
    h2 = _rms(x1, pfg_ref[...]) * (1.0 + scale_f) + shift_f
    h2_ref[...] = h2
    qp = _dot(h2.astype(BF16), wq_ref[...])
    for g in range(2 * PEER_HEADS):
        qg = qp[:, g * PEER_HALF:(g + 1) * PEER_HALF]
        qn = qg * lax.rsqrt(jnp.mean(qg * qg, axis=-1, keepdims=True) + EPS)
        sc_ref[g] = _dot_nt(keys_ref[g], qn.astype(BF16))


def _outproj(x2d, ofox, oh, mod3, wo, pmg, pfg, wq, keys, B, S):
    T, D = x2d.shape
    tm = 512
    tpb = S // tm
    G = 2 * PEER_HEADS
    const = lambda shape: pl.BlockSpec(shape, lambda i: (0,) * len(shape))
    row = pl.BlockSpec((tm, D), lambda i: (i, 0))
    return pl.pallas_call(
        _outproj_kernel,
        grid=(T // tm,),
        in_specs=[row,
                  pl.BlockSpec((tm, FOX_WIDTH), lambda i: (i, 0)),
                  pl.BlockSpec((tm, HGRN_WIDTH), lambda i: (i, 0)),
                  pl.BlockSpec((1, 1, N_MOD * D), lambda i: (i // tpb, 0, 0)),
                  const(wo.shape), const((1, D)), const((1, D)), const(wq.shape), const(keys.shape)],
        out_specs=[row, row, pl.BlockSpec((G, N_KEYS, tm), lambda i: (0, 0, i))],
        out_shape=[jax.ShapeDtypeStruct((T, D), F32), jax.ShapeDtypeStruct((T, D), F32),
                   jax.ShapeDtypeStruct((G, N_KEYS, T), F32)],
        compiler_params=_cparams(1),
        name="outproj",
    )(x2d, ofox, oh, mod3, wo, pmg, pfg, wq, keys)


TOPK_TM = SUBLANES * LANES


def _cx(a, b, desc):
    (va, ia), (vb, ib) = a, b
    swap = (vb > va) if desc else (vb < va)
    return ((jnp.where(swap, vb, va), jnp.where(swap, ib, ia)),
            (jnp.where(swap, va, vb), jnp.where(swap, ia, ib)))


def _bitonic_merge(x, desc):
    n = len(x)
    x = list(x)
    d = n // 2
    while d >= 1:
        for i in range(n):
            if (i // d) % 2 == 0:
                x[i], x[i + d] = _cx(x[i], x[i + d], desc)
        d //= 2
    return x


def _bitonic_sort(x, desc):
    n = len(x)
    if n == 1:
        return list(x)
    lo = _bitonic_sort(x[:n // 2], True)
    hi = _bitonic_sort(x[n // 2:], False)
    return _bitonic_merge(lo + hi, desc)


def _tile(ref, row):
    return ref[pl.ds(pl.multiple_of(row * SUBLANES, SUBLANES), SUBLANES), :]


def _put(ref, row, x):
    ref[pl.ds(pl.multiple_of(row * SUBLANES, SUBLANES), SUBLANES), :] = x


def _top16_network(s_ref, n, p_ref, va_ref, ia_ref, vb_ref, ib_ref):
    K = PEER_TOPK
    shape = (SUBLANES, LANES)

    def store_block(v_ref, i_ref, blk, pairs):
        rev = blk % 2
        for k, (v, i) in enumerate(pairs):
            pos = blk * K + k + rev * (K - 1 - 2 * k)
            _put(v_ref, pos, v)
            _put(i_ref, pos, i)

    def sort_block(blk, c):
        tag = (lambda r: jnp.full(shape, r, jnp.int32)) if p_ref is None else (lambda r: _tile(p_ref, r))
        pairs = [(_tile(s_ref, blk * K + k), tag(blk * K + k)) for k in range(K)]
        store_block(va_ref, ia_ref, blk, _bitonic_sort(pairs, True))
        return c

    lax.fori_loop(0, n // K, sort_block, 0)

    def merge_round(src_v, src_i, dst_v, dst_i, n_pairs):
        def merge(p, c):
            top = []
            for k in range(K):
                va, ia = _tile(src_v, 2 * p * K + k), _tile(src_i, 2 * p * K + k)
                vb, ib = _tile(src_v, (2 * p + 1) * K + k), _tile(src_i, (2 * p + 1) * K + k)
                take_b = vb > va
                top.append((jnp.where(take_b, vb, va), jnp.where(take_b, ib, ia)))
            store_block(dst_v, dst_i, p, _bitonic_merge(top, True))
            return c
        lax.fori_loop(0, n_pairs, merge, 0)

    src, dst = (va_ref, ia_ref), (vb_ref, ib_ref)
    blocks = n // K
    while blocks > 1:
        merge_round(*src, *dst, blocks // 2)
        src, dst = dst, src
        blocks //= 2
    return ([src[0][k * SUBLANES:(k + 1) * SUBLANES, :] for k in range(K)],
            [src[1][k * SUBLANES:(k + 1) * SUBLANES, :] for k in range(K)])


def _untied(s_ref, n, vals):
    ok = vals[0] > vals[1]
    for a in range(1, PEER_TOPK - 1):
        ok = ok & (vals[a] > vals[a + 1])
    last = vals[-1]

    def count(c, cnt):
        for k in range(SUBLANES):
            cnt = cnt + jnp.where(_tile(s_ref, c * SUBLANES + k) >= last, 1, 0)
        return cnt
    cnt = lax.fori_loop(0, n // SUBLANES, count, jnp.zeros(last.shape, jnp.int32))
    return ok & (cnt == PEER_TOPK)


def _top16_exact_loop(s_ref, n, p_ref):
    shape = (SUBLANES, LANES)
    U = SUBLANES

    def over_rows(fn, init):
        def trip(c, acc):
            for u in range(U):
                acc = fn(c * U + u, acc)
            return acc
        return lax.fori_loop(0, n // U, trip, init)

    vals, picks = [], []
    for _ in range(PEER_TOPK):
        m = over_rows(lambda k, m: jnp.maximum(m, _tile(s_ref, k)), jnp.full(shape, -jnp.inf, F32))
        ix = over_rows(lambda k, ix: jnp.minimum(ix, jnp.where(_tile(s_ref, k) == m, k, n)),
                       jnp.full(shape, n, jnp.int32))

        def remove(k, pick):
            hit = ix == k
            _put(s_ref, k, jnp.where(hit, -jnp.inf, _tile(s_ref, k)))
            return pick if p_ref is None else pick + jnp.where(hit, _tile(p_ref, k), 0)
        pick = over_rows(remove, jnp.zeros(shape, jnp.int32))
        vals.append(m)
        picks.append(ix if p_ref is None else pick)
    return vals, picks


def _top16(s_ref, n, p_ref, bufs):
    vals, tags = _top16_network(s_ref, n, p_ref, *bufs)
    ok = jnp.min(jnp.where(_untied(s_ref, n, vals), 1, 0)) > 0
    pack = lambda v, i: (jnp.stack(v), jnp.stack(i))
    v, i = lax.cond(ok, lambda: pack(vals, tags), lambda: pack(*_top16_exact_loop(s_ref, n, p_ref)))
    return [v[a] for a in range(PEER_TOPK)], [i[a] for a in range(PEER_TOPK)]


N_CAND = 64


def _topk_kernel(sc_ref, off_ref, par_ref, gate_ref, s_ref, p_ref, va_ref, ia_ref, vb_ref, ib_ref, e_ref):
    K = PEER_TOPK
    tm = TOPK_TM
    bufs = (va_ref, ia_ref, vb_ref, ib_ref)
    shape = (SUBLANES, LANES)

    def half_topk(g):
        x = sc_ref[g]
        blocks = [x[:, s * LANES:(s + 1) * LANES].reshape(N_KEYS // SUBLANES, SUBLANES, LANES)
                  for s in range(SUBLANES)]
        s_ref[...] = pltpu.einshape("gskl->gksl", jnp.stack(blocks, axis=1)).reshape(N_KEYS * SUBLANES, LANES)
        return _top16(s_ref, N_KEYS, None, bufs)

    for hd in range(PEER_HEADS):
        s1, i1 = half_topk(2 * hd)
        s2, i2 = half_topk(2 * hd + 1)
        r = 0
        for a in range(K):
            for b in range(K // (a + 1)):
                _put(s_ref, r, s1[a] + s2[b])
                _put(p_ref, r, i1[a] * N_KEYS + i2[b])
                r += 1
        for r in range(r, N_CAND):
            _put(s_ref, r, jnp.full(shape, -jnp.inf, F32))
            _put(p_ref, r, jnp.zeros(shape, jnp.int32))
        best, eidx = _top16(s_ref, N_CAND, p_ref, bufs)
        p = [jnp.exp(bv - best[0]) for bv in best]
        tot = functools.reduce(jnp.add, p)
        for a in range(K):
            e = hd * K + a
            e_ref[0, e * SUBLANES:(e + 1) * SUBLANES, :] = eidx[a].astype(F32)
            e_ref[1, e * SUBLANES:(e + 1) * SUBLANES, :] = p[a] / tot

    def token_major(x):
        t = pltpu.einshape("gesl->gsel", x.reshape(PEER_E // SUBLANES, SUBLANES, SUBLANES, LANES))
        return jnp.concatenate([t[:, s].reshape(PEER_E, LANES).T for s in range(SUBLANES)], axis=0)

    n_half = N_KEYS * N_KEYS // 2
    idx = token_major(e_ref[0]).astype(jnp.int32)
    gate_ref[...] = token_major(e_ref[1])
    par_ref[...] = (idx >= n_half).astype(F32)
    off = (idx & (n_half - 1)) * SUBLANES
    off_ref[...] = pltpu.einshape("abl->bal", off.reshape(tm // PEER_GROUP, PEER_GROUP, PEER_E))


def _topk(scores):
    G, NK, T = scores.shape
    tm = TOPK_TM
    out = pl.BlockSpec((tm, PEER_E), lambda i: (i, 0))
    return pl.pallas_call(
        _topk_kernel,
        grid=(T // tm,),
        in_specs=[pl.BlockSpec((G, NK, tm), lambda i: (0, 0, i))],
        out_specs=[pl.BlockSpec((PEER_GROUP, tm // PEER_GROUP, PEER_E), lambda i: (0, i, 0)), out, out],
        out_shape=[jax.ShapeDtypeStruct((PEER_GROUP, T // PEER_GROUP, PEER_E), jnp.int32),
                   jax.ShapeDtypeStruct((T, PEER_E), F32), jax.ShapeDtypeStruct((T, PEER_E), F32)],
        scratch_shapes=[pltpu.VMEM((N_KEYS * SUBLANES, LANES), F32), pltpu.VMEM((N_CAND * SUBLANES, LANES), jnp.int32),
                        pltpu.VMEM((N_KEYS * SUBLANES, LANES), F32), pltpu.VMEM((N_KEYS * SUBLANES, LANES), jnp.int32),
                        pltpu.VMEM((N_KEYS * SUBLANES // 2, LANES), F32),
                        pltpu.VMEM((N_KEYS * SUBLANES // 2, LANES), jnp.int32),
                        pltpu.VMEM((2, PEER_E * SUBLANES, LANES), F32)],
        compiler_params=_cparams(1),
        name="topk",
    )(scores)


PEER_TM = 16 * PEER_GROUP
TILE_ROWS = 2 * SUBLANES
STACK = PEER_E * TILE_ROWS


def _pack_kernel(lo_ref, hi_ref, o_ref):
    nb = lo_ref.shape[0]
    lo = lax.bitcast_convert_type(lo_ref[...].astype(BF16).astype(F32), jnp.uint32)
    hi = lax.bitcast_convert_type(hi_ref[...].astype(BF16).astype(F32), jnp.uint32)
    word = (hi & jnp.uint32(0xFFFF0000)) | lax.shift_right_logical(lo, jnp.uint32(16))
    for s in range(SUBLANES):
        o_ref[pl.ds(s, nb, stride=SUBLANES), :] = word[:, s * LANES:(s + 1) * LANES]


def _pack_table(w):
    n, d = w.shape
    nb = 256
    steps = n // 2 // nb
    return pl.pallas_call(
        _pack_kernel,
        grid=(steps,),
        in_specs=[pl.BlockSpec((nb, d), lambda i: (i, 0)), pl.BlockSpec((nb, d), lambda i: (i + steps, 0))],
        out_specs=pl.BlockSpec((nb * SUBLANES, LANES), lambda i: (i, 0)),
        out_shape=jax.ShapeDtypeStruct((n // 2 * SUBLANES, LANES), jnp.uint32),
        compiler_params=_cparams(1),
        name="pack",
    )(w, w)


def _peer_consts():
    c = jnp.arange(STACK)
    expand = (c[None, :] // TILE_ROWS == jnp.arange(PEER_E)[:, None])
    fold = ((c[None, :] % TILE_ROWS) // 2 == jnp.arange(SUBLANES)[:, None])
    half = (c % 2).reshape(1, STACK)
    return expand.astype(BF16), expand.T.astype(BF16), fold.astype(F32), half.astype(F32)


def _stack2(tab_ref, off_a, off_b, g):
    def tile(o):
        return pltpu.bitcast(tab_ref[pl.ds(pl.multiple_of(o, SUBLANES), SUBLANES), :], BF16)
    w0 = jnp.concatenate([tile(off_a[g, j]) for j in range(PEER_E)], axis=0)
    w1 = jnp.concatenate([tile(off_b[g, j]) for j in range(PEER_E)], axis=0)
    return jnp.concatenate([w0, w1], axis=1)


def _split2_dot(x, w):
    hi = x.astype(BF16)
    lo = (x - hi.astype(F32)).astype(BF16)
    return _dot(hi, w) + _dot(lo, w)


def _expand_sel(par_ref, expand_ref, half_ref):
    par = _dot(par_ref[...].astype(BF16), expand_ref[...])
    return jnp.where(par == half_ref[...], 1.0, 0.0)


def _fold_rows(x):
    tm = x.shape[0]
    chunks = [x[:, c * LANES:(c + 1) * LANES].reshape(tm // SUBLANES, SUBLANES, LANES) for c in range(SUBLANES)]
    return pltpu.einshape("gctl->gtcl", jnp.stack(chunks, axis=1)).reshape(tm * SUBLANES, LANES)


def _unfold_rows(xf):
    tm = xf.shape[0] // SUBLANES
    t = pltpu.einshape("gtcl->gctl", xf.reshape(tm // SUBLANES, SUBLANES, SUBLANES, LANES))
    return jnp.concatenate([t[:, c].reshape(tm, LANES) for c in range(SUBLANES)], axis=1)


def _peer_u_kernel(*refs):
    offs = refs[:PEER_GROUP]
    (par_ref, x_ref, gate_ref, expand_ref, collapse_ref, fold_ref, half_ref, tab_ref,
     act_ref, sel_ref, g_ref, xf_ref) = refs[PEER_GROUP:]
    tm = x_ref.shape[0]
    sel_ref[...] = _expand_sel(par_ref, expand_ref, half_ref)
    xf_ref[...] = _fold_rows(x_ref[...])
    fold = fold_ref[...]
    zero = jnp.zeros((SUBLANES, LANES), BF16)

    def pair(g, p):
        t = PEER_GROUP * g + 2 * p
        w2 = _stack2(tab_ref, offs[2 * p], offs[2 * p + 1], g)
        x0 = xf_ref[pl.ds(pl.multiple_of(t * SUBLANES, SUBLANES), SUBLANES), :].astype(BF16)
        x1 = xf_ref[pl.ds(pl.multiple_of((t + 1) * SUBLANES, SUBLANES), SUBLANES), :].astype(BF16)
        lhs = jnp.concatenate([jnp.concatenate([x0, zero], axis=1),
                               jnp.concatenate([zero, x1], axis=1)], axis=0)
        gm = _dot_nt(lhs, w2)
        g_ref[pl.ds(t, 1), :] = jnp.sum(gm[0:SUBLANES] * fold, axis=0, keepdims=True) * sel_ref[pl.ds(t, 1), :]
        g_ref[pl.ds(t + 1, 1), :] = (jnp.sum(gm[SUBLANES:] * fold, axis=0, keepdims=True)
                                     * sel_ref[pl.ds(t + 1, 1), :])

    def group(g, _):
        for p in range(PEER_GROUP // 2):
            pair(g, p)
        return 0

    lax.fori_loop(0, tm // PEER_GROUP, group, 0, unroll=2)
    a = _split2_dot(g_ref[...], collapse_ref[...])
    gelu = 0.5 * a * (1.0 + lax.erf(a * (2.0 ** -0.5)))
    act_ref[...] = gate_ref[...] * gelu


def _peer_v_kernel(*refs):
    offs = refs[:PEER_GROUP]
    (par_ref, act_ref, expand_ref, fold_ref, half_ref, tab_ref, x1_ref, mod_ref, g_ref,
     o_ref, ce_ref, yf_ref) = refs[PEER_GROUP:]
    tm, D = x1_ref.shape
    ce_ref[...] = _dot(act_ref[...].astype(BF16), expand_ref[...]) * _expand_sel(par_ref, expand_ref, half_ref)
    fold = fold_ref[...]

    def pair(g, p):
        t = PEER_GROUP * g + 2 * p
        w2 = _stack2(tab_ref, offs[2 * p], offs[2 * p + 1], g)
        coef = jnp.concatenate([(ce_ref[pl.ds(t, 1), :] * fold).astype(BF16),
                                (ce_ref[pl.ds(t + 1, 1), :] * fold).astype(BF16)], axis=0)
        out = _dot(coef, w2)
        yf_ref[pl.ds(pl.multiple_of(t * SUBLANES, SUBLANES), SUBLANES), :] = out[0:SUBLANES, 0:LANES]
        yf_ref[pl.ds(pl.multiple_of((t + 1) * SUBLANES, SUBLANES), SUBLANES), :] = out[SUBLANES:, LANES:]

    def group(g, _):
        for p in range(PEER_GROUP // 2):
            pair(g, p)
        return 0

    lax.fori_loop(0, tm // PEER_GROUP, group, 0)
    y = _unfold_rows(yf_ref[...])
    gate_f = mod_ref[0, :, 5 * D:6 * D]
    o_ref[...] = x1_ref[...] + gate_f * _rms(y, g_ref[...])


def _peer_specs(tab):
    tm = PEER_TM
    smem = [pl.BlockSpec((None, tm // PEER_GROUP, PEER_E), functools.partial(lambda k, i: (k, i, 0), k),
                         memory_space=pltpu.SMEM) for k in range(PEER_GROUP)]
    row = pl.BlockSpec((tm, PEER_E), lambda i: (i, 0))
    const = lambda shape: pl.BlockSpec(shape, lambda i: (0,) * len(shape))
    table = pl.BlockSpec(tab.shape, lambda i: (0, 0), pipeline_mode=pl.Buffered(1))
    return tm, smem, row, const, table


def _peer_u(off, par, h2, gate, tab):
    T, D = h2.shape
    tm, smem, row, const, table = _peer_specs(tab)
    expand, collapse, fold, half = _peer_consts()
    return pl.pallas_call(
        _peer_u_kernel,
        grid=(T // tm,),
        in_specs=smem + [row, pl.BlockSpec((tm, D), lambda i: (i, 0)), row,
                  const(expand.shape), const(collapse.shape), const(fold.shape), const(half.shape), table],
        out_specs=row,
        out_shape=jax.ShapeDtypeStruct((T, PEER_E), F32),
        scratch_shapes=[pltpu.VMEM((tm, STACK), F32), pltpu.VMEM((tm, STACK), F32),
                        pltpu.VMEM((tm * SUBLANES, LANES), F32)],
        compiler_params=_cparams(1),
        name="peer_u",
    )(*([off] * PEER_GROUP), par, h2, gate, expand, collapse, fold, half, tab)


def _peer_v(off, par, act, tab, x1, mod3, g, S):
    T, D = x1.shape
    tm, smem, row, const, table = _peer_specs(tab)
    tpb = S // tm
    expand, _, fold, half = _peer_consts()
    wide = pl.BlockSpec((tm, D), lambda i: (i, 0))
    return pl.pallas_call(
        _peer_v_kernel,
        grid=(T // tm,),
        in_specs=smem + [row, row, const(expand.shape), const(fold.shape), const(half.shape), table,
                  wide, pl.BlockSpec((1, 1, N_MOD * D), lambda i: (i // tpb, 0, 0)), const((1, D))],
        out_specs=wide,
        out_shape=jax.ShapeDtypeStruct((T, D), F32),
        scratch_shapes=[pltpu.VMEM((tm, STACK), F32), pltpu.VMEM((tm * SUBLANES, LANES), F32)],
        compiler_params=_cparams(1),
        name="peer_v",
    )(*([off] * PEER_GROUP), par, act, expand, fold, half, tab, x1, mod3, g)


def kernel(x, c, w_ada, b_ada, pre_mix_g, post_mix_g, w_in, b_fox_f, hgrn_gamma, hgrn_norm_g, w_out, pre_ffn_g,
           post_ffn_g, peer_w_q, peer_sub_keys, peer_u, peer_v):
    B, S, D = x.shape
    T = B * S
    depth = w_in.shape[0]
    assert depth == 1, "single-layer block"
    l = 0
    x2d = x.reshape(T, D)

    mod3 = _ada(c, w_ada[l], b_ada[l]).reshape(B, 1, N_MOD * D)

    w = w_in[l]
    o1 = 3 * FOX_WIDTH
    wqkv = w[:, :o1].astype(BF16)
    wf = jnp.pad(w[:, o1:o1 + FOX_HEADS], ((0, 0), (0, LANES - FOX_HEADS))).astype(BF16)
    bf = jnp.pad(b_fox_f[l].astype(F32), (0, LANES - FOX_HEADS)).reshape(1, LANES)
    wh = w[:, o1 + FOX_HEADS:].astype(BF16)

    qt, ka, vt, hq, hf, hi, hg = _proj(x2d, mod3, pre_mix_g[l].reshape(1, D), wqkv, wf, bf, wh, B, S)

    o_fox = _fox(qt, ka, vt)

    o_h = _hgrn(hq, hf, hi, hg, hgrn_gamma[l:l + 2].astype(F32), hgrn_norm_g[l].reshape(1, HGRN_WIDTH), B, S)

    wo = w_out[l].astype(BF16)
    keys =peer_sub_keys[l].reshape(2 * PEER_HEADS, N_KEYS, PEER_HALF).astype(BF16)
    x1, h2, scores = _outproj(x2d, o_fox.reshape(T, FOX_WIDTH), o_h, mod3, wo, post_mix_g[l].reshape(1, D),
                              pre_ffn_g[l].reshape(1, D), peer_w_q[l].astype(BF16), keys, B, S)

    off, par, gate = _topk(scores)

    act = _peer_u(off, par, h2, gate, _pack_table(peer_u[l]))
    out = _peer_v(off, par, act, _pack_table(peer_v[l]), x1, mod3, post_ffn_g[l].reshape(1, D), S)
    return out.reshape(B, S, D)
```
